```python
import math, functools
import jax, jax.numpy as jnp
from jax import lax
import numpy as np

D_MODEL = 1024
BATCH = 8
SEQ = 2048
DEPTH = 1
DEC_BATCH = 32
DEC_SEQ = 1
PAST_LEN = 8192
PAGE_SIZE = 128

N_HEADS_A = 8
HEAD_DIM_A = 64
MOBA_BLOCK = 256
MOBA_TOPK = 3
N_HEADS_B = 4
HEAD_DIM_B = 64
V_DIM_B = 2 * HEAD_DIM_B
WIDTH_A = N_HEADS_A * HEAD_DIM_A
WIDTH_B = N_HEADS_B * 2 * HEAD_DIM_B
D_IN = 3 * WIDTH_A + 3 * WIDTH_B + 2 * D_MODEL
D_FF = 2816
N_MOD = 9
Q_BLOCK = 128
RMS_EPS = 1e-6

kernel_name = "hybrid_moba_diffattn_macaron_step"


def rms_norm(x, g):
    xf = x.astype(jnp.float32)
    y = xf * lax.rsqrt(jnp.mean(xf * xf, axis=-1, keepdims=True) + RMS_EPS)
    return (y * g.astype(jnp.float32)).astype(x.dtype)


def swiglu(h, w_gate, w_up, w_down):
    return (jax.nn.silu(h @ w_gate) * (h @ w_up)) @ w_down


def alibi_slopes(n_heads):
    return jnp.asarray([2.0 ** (-8.0 * (i + 1) / n_heads) for i in range(n_heads)], dtype=jnp.float32)


def lambda_init(layer):
    return 0.8 - 0.6 * math.exp(-0.3 * layer)


def moba_select(q, q_pos, k_means):
    H, Q, _ = q.shape
    nb = k_means.shape[1]
    q_blk = q_pos // MOBA_BLOCK
    score = jnp.einsum("hqd,hnd->hqn", q, k_means, preferred_element_type=jnp.float32)
    fully_past = jnp.arange(nb, dtype=jnp.int32)[None, :] < q_blk[:, None]
    score = jnp.where(fully_past[None], score, -jnp.inf)
    n_top = min(MOBA_TOPK, nb)
    _, top = lax.top_k(score, n_top)
    top_valid = jnp.arange(n_top, dtype=jnp.int32)[None, :] < q_blk[:, None]
    idx = jnp.concatenate([top.astype(jnp.int32),
                           jnp.broadcast_to(q_blk[None, :, None], (H, Q, 1))], axis=-1)
    valid = jnp.concatenate([jnp.broadcast_to(top_valid[None], (H, Q, n_top)),
                             jnp.ones((H, Q, 1), dtype=bool)], axis=-1)
    return idx, valid


def moba_attend(q, q_pos, k_sel, v_sel, key_pos, valid, slopes):
    H, Q, J, BS = key_pos.shape
    s = jnp.einsum("hqd,hqjkd->hqjk", q, k_sel, preferred_element_type=jnp.float32) * (q.shape[-1] ** -0.5)
    dist = (q_pos[None, :, None, None] - key_pos).astype(jnp.float32)
    s = s - slopes[:, None, None, None] * dist
    mask = valid[..., None] & (key_pos <= q_pos[None, :, None, None])
    s = jnp.where(mask, s, -jnp.inf)
    p = jax.nn.softmax(s.reshape(H, Q, J * BS), axis=-1).reshape(H, Q, J, BS)
    out = jnp.einsum("hqjk,hqjkd->hqd", p, v_sel.astype(jnp.float32))
    return out.astype(q.dtype)


def moba_prompt(q, k, v):
    B, S, H, DH = q.shape
    nb = -(-S // MOBA_BLOCK)
    pad = nb * MOBA_BLOCK - S
    padw = ((0, 0), (0, pad), (0, 0), (0, 0))
    kb = jnp.pad(k, padw).reshape(B, nb, MOBA_BLOCK, H, DH).transpose(0, 3, 1, 2, 4)
    vb = jnp.pad(v, padw).reshape(B, nb, MOBA_BLOCK, H, DH).transpose(0, 3, 1, 2, 4)
    k_means = jnp.mean(kb.astype(jnp.float32), axis=3)
    qh = q.transpose(0, 2, 1, 3)
    n_chunks = S // Q_BLOCK
    heads = jnp.arange(H, dtype=jnp.int32)[:, None, None]
    slopes = alibi_slopes(H)

    def one_block(i):
        b = i // n_chunks
        start = (i % n_chunks) * Q_BLOCK
        qc = lax.dynamic_slice_in_dim(qh[b], start, Q_BLOCK, axis=1)
        pos = start + jnp.arange(Q_BLOCK, dtype=jnp.int32)
        idx, valid = moba_select(qc, pos, k_means[b])
        key_pos = idx[..., None] * MOBA_BLOCK + jnp.arange(MOBA_BLOCK, dtype=jnp.int32)
        return moba_attend(qc, pos, kb[b, heads, idx], vb[b, heads, idx], key_pos, valid, slopes)

    out = lax.map(one_block, jnp.arange(B * n_chunks, dtype=jnp.int32))
    return out.reshape(B, n_chunks, H, Q_BLOCK, DH).transpose(0, 1, 3, 2, 4).reshape(B, S, H, DH)


def moba_sample(q, k_new, v_new, k_pool, v_pool, layer, page_table):
    DB, T, H, DH = q.shape
    n_pages = page_table.shape[1]
    past = n_pages * PAGE_SIZE
    nb = -(-(past + T) // MOBA_BLOCK)
    pad = nb * MOBA_BLOCK - past - T
    heads = jnp.arange(H, dtype=jnp.int32)[:, None, None]
    slopes = alibi_slopes(H)
    pos = past + jnp.arange(T, dtype=jnp.int32)

    def one_seq(q_s, k_s, v_s, pt):
        k_past = k_pool[layer, pt].reshape(past, H, DH)
        k_full = jnp.concatenate([k_past, k_s.astype(k_past.dtype),
                                  jnp.zeros((pad, H, DH), k_past.dtype)], axis=0)
        kb = k_full.reshape(nb, MOBA_BLOCK, H, DH).transpose(2, 0, 1, 3)
        k_means = jnp.mean(kb.astype(jnp.float32), axis=2)
        qh = q_s.transpose(1, 0, 2)
        idx, valid = moba_select(qh, pos, k_means)
        key_pos = idx[..., None] * MOBA_BLOCK + jnp.arange(MOBA_BLOCK, dtype=jnp.int32)
        k_sel = kb[heads, idx]
        in_past = key_pos < past
        page = pt[jnp.clip(key_pos // PAGE_SIZE, 0, n_pages - 1)]
        head_b = jnp.broadcast_to(heads[..., None], key_pos.shape)
        v_past = v_pool[layer, page, key_pos % PAGE_SIZE, head_b]
        v_cur = v_s[jnp.clip(key_pos - past, 0, T - 1), head_b]
        v_sel = jnp.where(in_past[..., None], v_past, v_cur.astype(v_past.dtype))
        return moba_attend(qh, pos, k_sel, v_sel, key_pos, valid, slopes)

    out = jax.vmap(one_seq)(q, k_new, v_new, page_table)
    return out.transpose(0, 2, 1, 3).astype(q.dtype)


def diff_attend(q, q_pos, k, v, key_pos, lam, slopes):
    s = jnp.einsum("bhcqd,bhckd->bhcqk", q, k, preferred_element_type=jnp.float32) * (q.shape[-1] ** -0.5)
    dist = (q_pos[:, None] - key_pos[None, :]).astype(jnp.float32)
    s = s - slopes[None, :, None, None, None] * dist
    s = jnp.where(key_pos[None, :] <= q_pos[:, None], s, -jnp.inf)
    p = jax.nn.softmax(s, axis=-1)
    w = p[:, :, 0] - lam * p[:, :, 1]
    return jnp.einsum("bhqk,bhkd->bhqd", w, v.astype(jnp.float32)).astype(q.dtype)


def diff_prompt(q, k, v, lam):
    B, S, H = q.shape[:3]
    qh = q.transpose(0, 2, 3, 1, 4)
    kh = k.transpose(0, 2, 3, 1, 4)
    vh = v.transpose(0, 2, 1, 3)
    key_pos = jnp.arange(S, dtype=jnp.int32)
    slopes = alibi_slopes(H)
    n_chunks = S // Q_BLOCK

    def one_block(c):
        start = c * Q_BLOCK
        qc = lax.dynamic_slice_in_dim(qh, start, Q_BLOCK, axis=3)
        pos = start + jnp.arange(Q_BLOCK, dtype=jnp.int32)
        return diff_attend(qc, pos, kh, vh, key_pos, lam, slopes)

    out = lax.map(one_block, jnp.arange(n_chunks, dtype=jnp.int32))
    return out.transpose(1, 0, 3, 2, 4).reshape(B, S, H, V_DIM_B)


def diff_sample(q, k_new, v_new, k_pool, v_pool, layer, page_table, lam):
    DB, T, H = q.shape[:3]
    past = page_table.shape[1] * PAGE_SIZE
    k_past = k_pool[layer, page_table].reshape(DB, past, H, 2, HEAD_DIM_B)
    v_past = v_pool[layer, page_table].reshape(DB, past, H, V_DIM_B)
    k_all = jnp.concatenate([k_past, k_new.astype(k_past.dtype)], axis=1)
    v_all = jnp.concatenate([v_past, v_new.astype(v_past.dtype)], axis=1)
    pos = past + jnp.arange(T, dtype=jnp.int32)
    key_pos = jnp.arange(past + T, dtype=jnp.int32)
    out = diff_attend(q.transpose(0, 2, 3, 1, 4), pos, k_all.transpose(0, 2, 3, 1, 4),
                      v_all.transpose(0, 2, 1, 3), key_pos, lam, alibi_slopes(H))
    return out.transpose(0, 2, 1, 3).astype(q.dtype)


def prompt_attend(q_a, k_a, v_a, q_b, k_b, v_b, lam):
    return moba_prompt(q_a, k_a, v_a), diff_prompt(q_b, k_b, v_b, lam)


def sample_attend(q_a, k_a, v_a, q_b, k_b, v_b, lam, k_moba_pool, v_moba_pool,
                  k_diff_pool, v_diff_pool, page_table, layer):
    o_a = moba_sample(q_a, k_a, v_a, k_moba_pool, v_moba_pool, layer, page_table)
    o_b = diff_sample(q_b, k_b, v_b, k_diff_pool, v_diff_pool, layer, page_table, lam)
    return o_a, o_b


def decoder_layer(x, c, lp, lam_init, attend):
    B, T, _ = x.shape
    mod = (jax.nn.silu(c) @ lp["w_ada"] + lp["b_ada"]).reshape(B, 1, N_MOD, D_MODEL)
    sh1, sc1, g1, sh2, sc2, g2, sh3, sc3, g3 = [mod[:, :, i] for i in range(N_MOD)]
    h = rms_norm(x, lp["norm_ffn1"]) * (1.0 + sc1) + sh1
    x = x + 0.5 * g1 * swiglu(h, lp["ffn1_w_gate"], lp["ffn1_w_up"], lp["ffn1_w_down"])
    h = rms_norm(x, lp["norm_mix"]) * (1.0 + sc2) + sh2
    proj = h @ lp["w_in"]
    cuts = [WIDTH_A, 2 * WIDTH_A, 3 * WIDTH_A, 3 * WIDTH_A + WIDTH_B,
            3 * WIDTH_A + 2 * WIDTH_B, 3 * WIDTH_A + 3 * WIDTH_B, 3 * WIDTH_A + 3 * WIDTH_B + D_MODEL]
    q_a, k_a, v_a, q_b, k_b, v_b, g_a, g_b = jnp.split(proj, cuts, axis=-1)
    q_a = rms_norm(q_a.reshape(B, T, N_HEADS_A, HEAD_DIM_A), lp["qn_moba"])
    k_a = rms_norm(k_a.reshape(B, T, N_HEADS_A, HEAD_DIM_A), lp["kn_moba"])
    v_a = v_a.reshape(B, T, N_HEADS_A, HEAD_DIM_A)
    q_b = rms_norm(q_b.reshape(B, T, N_HEADS_B, 2, HEAD_DIM_B), lp["qn_diff"])
    k_b = rms_norm(k_b.reshape(B, T, N_HEADS_B, 2, HEAD_DIM_B), lp["kn_diff"])
    v_b = v_b.reshape(B, T, N_HEADS_B, V_DIM_B)
    lam = (jnp.exp(jnp.sum(lp["lambda_q1"].astype(jnp.float32) * lp["lambda_k1"].astype(jnp.float32)))
           - jnp.exp(jnp.sum(lp["lambda_q2"].astype(jnp.float32) * lp["lambda_k2"].astype(jnp.float32)))
           + lam_init)
    o_a, o_b = attend(q_a, k_a, v_a, q_b, k_b, v_b, lam)
    o_b = rms_norm(o_b, lp["subln_diff"]) * (1.0 - lam_init)
    y_a = o_a.reshape(B, T, WIDTH_A) @ lp["w_branch_moba"]
    y_b = o_b.reshape(B, T, WIDTH_B) @ lp["w_branch_diff"]
    mixed = (jax.nn.sigmoid(g_a) * y_a + jax.nn.sigmoid(g_b) * y_b) @ lp["w_out"]
    x = x + g2 * mixed
    h = rms_norm(x, lp["norm_ffn2"]) * (1.0 + sc3) + sh3
    x = x + 0.5 * g3 * swiglu(h, lp["ffn2_w_gate"], lp["ffn2_w_up"], lp["ffn2_w_down"])
    rows = (k_a, v_a, k_b.reshape(B, T, N_HEADS_B, 2 * HEAD_DIM_B), v_b)
    return x, rows


def setup_inputs(seed: int = 0) -> dict:
    key = jax.random.key(seed)
    ks = iter(jax.random.split(key, 48))
    f32 = jnp.float32
    n_pages = PAST_LEN // PAGE_SIZE
    n_used = DEC_BATCH * n_pages
    n_phys = n_used + max(1, n_used // 4)

    def nrm(shape, scale=1.0):
        return jax.random.normal(next(ks), shape, f32) * scale

    def gain(shape):
        return 1.0 + 0.02 * nrm(shape)

    page_table = jax.random.permutation(next(ks), n_phys)[:n_used].reshape(DEC_BATCH, n_pages).astype(jnp.int32)
    return {
        "x_prompt": nrm((BATCH, SEQ, D_MODEL)),
        "x_sample": nrm((DEC_BATCH, DEC_SEQ, D_MODEL)),
        "cache_k_moba": nrm((DEPTH, n_phys, PAGE_SIZE, N_HEADS_A, HEAD_DIM_A)),
        "cache_v_moba": nrm((DEPTH, n_phys, PAGE_SIZE, N_HEADS_A, HEAD_DIM_A)),
        "cache_k_diff": nrm((DEPTH, n_phys, PAGE_SIZE, N_HEADS_B, 2 * HEAD_DIM_B)),
        "cache_v_diff": nrm((DEPTH, n_phys, PAGE_SIZE, N_HEADS_B, V_DIM_B)),
        "page_table": page_table,
        "c_prompt": nrm((BATCH, D_MODEL)),
        "c_sample": nrm((DEC_BATCH, D_MODEL)),
        "w_ada": nrm((DEPTH, D_MODEL, N_MOD * D_MODEL), 0.5 * D_MODEL ** -0.5),
        "b_ada": nrm((DEPTH, N_MOD * D_MODEL), 0.02),
        "norm_ffn1": gain((DEPTH, D_MODEL)),
        "ffn1_w_gate": nrm((DEPTH, D_MODEL, D_FF), D_MODEL ** -0.5),
        "ffn1_w_up": nrm((DEPTH, D_MODEL, D_FF), D_MODEL ** -0.5),
        "ffn1_w_down": nrm((DEPTH, D_FF, D_MODEL), D_FF ** -0.5),
        "norm_mix": gain((DEPTH, D_MODEL)),
        "w_in": nrm((DEPTH, D_MODEL, D_IN), D_MODEL ** -0.5),
        "qn_moba": gain((DEPTH, HEAD_DIM_A)),
        "kn_moba": gain((DEPTH, HEAD_DIM_A)),
        "qn_diff": gain((DEPTH, HEAD_DIM_B)),
        "kn_diff": gain((DEPTH, HEAD_DIM_B)),
        "lambda_q1": nrm((DEPTH, HEAD_DIM_B), 0.1),
        "lambda_k1": nrm((DEPTH, HEAD_DIM_B), 0.1),
        "lambda_q2": nrm((DEPTH, HEAD_DIM_B), 0.1),
        "lambda_k2": nrm((DEPTH, HEAD_DIM_B), 0.1),
        "subln_diff": gain((DEPTH, V_DIM_B)),
        "w_branch_moba": nrm((DEPTH, WIDTH_A, D_MODEL), WIDTH_A ** -0.5),
        "w_branch_diff": nrm((DEPTH, WIDTH_B, D_MODEL), WIDTH_B ** -0.5),
        "w_out": nrm((DEPTH, D_MODEL, D_MODEL), D_MODEL ** -0.5),
        "norm_ffn2": gain((DEPTH, D_MODEL)),
        "ffn2_w_gate": nrm((DEPTH, D_MODEL, D_FF), D_MODEL ** -0.5),
        "ffn2_w_up": nrm((DEPTH, D_MODEL, D_FF), D_MODEL ** -0.5),
        "ffn2_w_down": nrm((DEPTH, D_FF, D_MODEL), D_FF ** -0.5),
    }


def reference(x_prompt, x_sample, cache_k_moba, cache_v_moba, cache_k_diff, cache_v_diff, page_table,
              c_prompt, c_sample, w_ada, b_ada, norm_ffn1, ffn1_w_gate, ffn1_w_up, ffn1_w_down,
              norm_mix, w_in, qn_moba, kn_moba, qn_diff, kn_diff, lambda_q1, lambda_k1, lambda_q2,
              lambda_k2, subln_diff, w_branch_moba, w_branch_diff, w_out, norm_ffn2, ffn2_w_gate,
              ffn2_w_up, ffn2_w_down):
    y_prompt, y_sample = x_prompt, x_sample
    rows_prompt, rows_sample = [], []
    for l in range(DEPTH):
        lp = {
            "w_ada": w_ada[l], "b_ada": b_ada[l],
            "norm_ffn1": norm_ffn1[l], "ffn1_w_gate": ffn1_w_gate[l], "ffn1_w_up": ffn1_w_up[l],
            "ffn1_w_down": ffn1_w_down[l], "norm_mix": norm_mix[l], "w_in": w_in[l],
            "qn_moba": qn_moba[l], "kn_moba": kn_moba[l], "qn_diff": qn_diff[l], "kn_diff": kn_diff[l],
            "lambda_q1": lambda_q1[l], "lambda_k1": lambda_k1[l], "lambda_q2": lambda_q2[l],
            "lambda_k2": lambda_k2[l], "subln_diff": subln_diff[l],
            "w_branch_moba": w_branch_moba[l], "w_branch_diff": w_branch_diff[l], "w_out": w_out[l],
            "norm_ffn2": norm_ffn2[l], "ffn2_w_gate": ffn2_w_gate[l], "ffn2_w_up": ffn2_w_up[l],
            "ffn2_w_down": ffn2_w_down[l],
        }
        lam_init = lambda_init(l)
        y_prompt, r_p = decoder_layer(y_prompt, c_prompt, lp, lam_init, prompt_attend)
        attend_s = functools.partial(sample_attend, k_moba_pool=cache_k_moba, v_moba_pool=cache_v_moba,
                                     k_diff_pool=cache_k_diff, v_diff_pool=cache_v_diff,
                                     page_table=page_table, layer=l)
        y_sample, r_s = decoder_layer(y_sample, c_sample, lp, lam_init, attend_s)
        rows_prompt.append(r_p)
        rows_sample.append(r_s)
    k_moba_prompt = jnp.stack([r[0] for r in rows_prompt])
    v_moba_prompt = jnp.stack([r[1] for r in rows_prompt])
    k_diff_prompt = jnp.stack([r[2] for r in rows_prompt])
    v_diff_prompt = jnp.stack([r[3] for r in rows_prompt])
    k_moba_sample = jnp.stack([r[0] for r in rows_sample])
    v_moba_sample = jnp.stack([r[1] for r in rows_sample])
    k_diff_sample = jnp.stack([r[2] for r in rows_sample])
    v_diff_sample = jnp.stack([r[3] for r in rows_sample])
    return (y_prompt, y_sample, k_moba_prompt, v_moba_prompt, k_diff_prompt, v_diff_prompt,
            k_moba_sample, v_moba_sample, k_diff_sample, v_diff_sample)
```

```python
import functools
import math

import jax
import jax.numpy as jnp
import numpy as np
from jax import lax
from jax.experimental import pallas as pl
from jax.experimental.pallas import tpu as pltpu

F32 = jnp.float32
BF16 = jnp.bfloat16

N_HEADS_A = 8
HEAD_DIM_A = 64
MOBA_BLOCK = 256
MOBA_TOPK = 3
N_HEADS_B = 4
HEAD_DIM_B = 64
V_DIM_B = 2 * HEAD_DIM_B
WIDTH = 512
N_GROUPS = WIDTH // 64
PAGE_SIZE = 128
N_MOD = 9
RMS_EPS = 1e-6
QK_SCALE = 0.125
NEG = -1e30

LANES = 128
MXU_DIM = 256
VMEM_LIMIT_BYTES = 56 * 1024 * 1024

TOKEN_TILE = 512
PAGES_PER_STEP = 16


def _slopes(n_heads):
    return [2.0 ** (-8.0 * (i + 1) / n_heads) for i in range(n_heads)]


def _lambda_init(layer):
    return 0.8 - 0.6 * math.exp(-0.3 * layer)


def _dot(a, b):
    return jnp.dot(a, b, preferred_element_type=F32)


def _dot_nt(a, b):
    return lax.dot_general(a, b, (((1,), (1,)), ((), ())), preferred_element_type=F32)


def _split3(x):
    hi = x.astype(BF16)
    r1 = x - hi.astype(F32)
    mid = r1.astype(BF16)
    lo = (r1 - mid.astype(F32)).astype(BF16)
    return hi, mid, lo


def _dot_split(x, e):
    hi, mid, lo = _split3(x)
    return _dot(hi, e) + _dot(mid, e) + _dot(lo, e)


def _dot_nt_split(x, e):
    hi, mid, lo = _split3(x)
    return _dot_nt(hi, e) + _dot_nt(mid, e) + _dot_nt(lo, e)


def _rms(x, w):
    ms = jnp.mean(x * x, axis=-1, keepdims=True)
    return x * lax.rsqrt(ms + RMS_EPS) * w


def _pick_ff_tile(d_ff):
    best = LANES
    for t in range(LANES, min(d_ff, 1408) + 1, LANES):
        if d_ff % t == 0:
            best = t
    return best


def _params(*sem):
    return pltpu.CompilerParams(dimension_semantics=sem, vmem_limit_bytes=VMEM_LIMIT_BYTES)


def _const_spec(shape):
    nd = len(shape)
    return pl.BlockSpec(shape, lambda *_: (0,) * nd)


def _ada_kernel(c_ref, w_ref, b_ref, o_ref):
    c = c_ref[...]
    s = c * jax.nn.sigmoid(c)
    o_ref[...] = _dot(s.astype(BF16), w_ref[...]) + b_ref[...]


def _ada(c, w_bf, b):
    m, d = c.shape
    n = w_bf.shape[1]
    tn = 1024 if n % 1024 == 0 else n
    return pl.pallas_call(
        _ada_kernel,
        grid=(n // tn,),
        in_specs=[pl.BlockSpec((m, d), lambda j: (0, 0)),
                  pl.BlockSpec((d, tn), lambda j: (0, j)),
                  pl.BlockSpec((1, tn), lambda j: (0, j))],
        out_specs=pl.BlockSpec((m, tn), lambda j: (0, j)),
        out_shape=jax.ShapeDtypeStruct((m, n), F32),
        compiler_params=_params("arbitrary"),
        name="ada_mod",
    )(c, w_bf, b.reshape(1, n))


def _ffn_update(x, shift, scale, gate, nw, wg_ref, wu_ref, wd_ref, tf):
    h = (_rms(x, nw) * (1.0 + scale) + shift).astype(BF16)
    d_ff = wg_ref.shape[1]
    acc = jnp.zeros(x.shape, F32)
    for j in range(d_ff // tf):
        g = _dot(h, wg_ref[:, j * tf:(j + 1) * tf])
        u = _dot(h, wu_ref[:, j * tf:(j + 1) * tf])
        a = (g * jax.nn.sigmoid(g) * u).astype(BF16)
        acc = acc + _dot(a, wd_ref[j * tf:(j + 1) * tf, :])
    return x + 0.5 * gate * acc


def _ffn_kernel(x_ref, mod_ref, nw_ref, wg_ref, wu_ref, wd_ref, o_ref, *, k0, tf):
    o_ref[...] = _ffn_update(x_ref[...], mod_ref[k0], mod_ref[k0 + 1], mod_ref[k0 + 2],
                             nw_ref[...], wg_ref, wu_ref, wd_ref, tf)


def _ffn(x, mod, nw, wg, wu, wd, k0, tm):
    b, s, d = x.shape
    r = mod.shape[2]
    d_ff = wg.shape[1]
    tf = _pick_ff_tile(d_ff)
    return pl.pallas_call(
        functools.partial(_ffn_kernel, k0=k0, tf=tf),
        grid=(b, s // tm),
        in_specs=[pl.BlockSpec((None, tm, d), lambda i, t: (i, t, 0)),
                  pl.BlockSpec((None, N_MOD, r, d), lambda i, t: (i, 0, 0, 0)),
                  _const_spec((1, d)),
                  _const_spec((d, d_ff)), _const_spec((d, d_ff)), _const_spec((d_ff, d))],
        out_specs=pl.BlockSpec((None, tm, d), lambda i, t: (i, t, 0)),
        out_shape=jax.ShapeDtypeStruct((b, s, d), F32),
        compiler_params=_params("arbitrary", "arbitrary"),
        name="ffn",
    )(x, mod, nw, wg, wu, wd)


def _head_norm(seg, gain_row, bd):
    sq = (seg * seg).astype(BF16)
    parts = [_dot(sq[:, c * MXU_DIM:(c + 1) * MXU_DIM], bd) for c in range(WIDTH // MXU_DIM)]
    ms = jnp.concatenate(parts, axis=1) * (1.0 / 64.0)
    return seg * lax.rsqrt(ms + RMS_EPS) * gain_row


def _proj_kernel(x_ref, mod_ref, nw_ref, win_ref, gains_ref, bd_ref, *outs, transposed):
    x = x_ref[...]
    d = x.shape[1]
    h = (_rms(x, nw_ref[...]) * (1.0 + mod_ref[4]) + mod_ref[3]).astype(BF16)
    bd = bd_ref[...]

    def seg(j):
        return _dot(h, win_ref[:, j * WIDTH:(j + 1) * WIDTH])

    q_a = _head_norm(seg(0), gains_ref[0:1, :], bd) * QK_SCALE
    k_a = _head_norm(seg(1), gains_ref[1:2, :], bd)
    v_a = seg(2)
    q_b = _head_norm(seg(3), gains_ref[2:3, :], bd) * QK_SCALE
    k_b = _head_norm(seg(4), gains_ref[3:4, :], bd)
    v_b = seg(5)
    g0 = 6 * WIDTH
    g_a = jax.nn.sigmoid(_dot(h, win_ref[:, g0:g0 + d]))
    g_b = jax.nn.sigmoid(_dot(h, win_ref[:, g0 + d:g0 + 2 * d]))

    if not transposed:
        (qa_ref, ka_ref, va_ref, qb_ref, kb_ref, vb_ref, ga_ref, gb_ref) = outs
        qa_ref[...] = q_a
        qb_ref[...] = q_b
    else:
        (ka_ref, va_ref, kb_ref, vb_ref, kabf_ref, kbbf_ref, qta_ref, qtb_ref,
         vta_ref, vtb_ref, km_ref, ga_ref, gb_ref) = outs
        for r in range(x.shape[0] // MOBA_BLOCK):
            rows = slice(r * MOBA_BLOCK, (r + 1) * MOBA_BLOCK)
            kabf_ref[r] = k_a[rows].astype(BF16)
            kbbf_ref[r] = k_b[rows].astype(BF16)
            qta_ref[r] = q_a[rows].T.astype(BF16)
            qtb_ref[r] = q_b[rows].T.astype(BF16)
            vta_ref[r] = v_a[rows].T.astype(BF16)
            vtb_ref[r] = v_b[rows].T.astype(BF16)
            km_ref[r] = jnp.sum(k_a[rows], axis=0, keepdims=True) * (1.0 / MOBA_BLOCK)
    ka_ref[...] = k_a
    va_ref[...] = v_a
    kb_ref[...] = k_b
    vb_ref[...] = v_b
    ga_ref[...] = g_a
    gb_ref[...] = g_b


def _proj(x, mod, nw, w_in, gains, bd, tm, transposed):
    b, s, d = x.shape
    r = mod.shape[2]
    d_in = w_in.shape[1]
    row_spec = lambda w: pl.BlockSpec((None, tm, w), lambda i, t: (i, t, 0))
    rows = lambda w, dt=F32: jax.ShapeDtypeStruct((b, s, w), dt)
    if transposed:
        nb, bpt = s // MOBA_BLOCK, tm // MOBA_BLOCK
        blk = lambda shp: pl.BlockSpec((None, bpt) + shp, lambda i, t: (i, t, 0, 0))
        out_specs = [row_spec(WIDTH)] * 4 + [blk((MOBA_BLOCK, WIDTH))] * 2 + [blk((WIDTH, MOBA_BLOCK))] * 4 \
            + [blk((1, WIDTH))] + [row_spec(d)] * 2
        out_shape = [rows(WIDTH)] * 4 \
            + [jax.ShapeDtypeStruct((b, nb, MOBA_BLOCK, WIDTH), BF16)] * 2 \
            + [jax.ShapeDtypeStruct((b, nb, WIDTH, MOBA_BLOCK), BF16)] * 4 \
            + [jax.ShapeDtypeStruct((b, nb, 1, WIDTH), F32)] + [rows(d)] * 2
    else:
        out_specs = [row_spec(WIDTH)] * 6 + [row_spec(d)] * 2
        out_shape = [rows(WIDTH)] * 6 + [rows(d)] * 2
    return pl.pallas_call(
        functools.partial(_proj_kernel, transposed=transposed),
        grid=(b, s // tm),
        in_specs=[pl.BlockSpec((None, tm, d), lambda i, t: (i, t, 0)),
                  pl.BlockSpec((None, N_MOD, r, d), lambda i, t: (i, 0, 0, 0)),
                  _const_spec((1, d)), _const_spec((d, d_in)),
                  _const_spec((4, WIDTH)), _const_spec((MXU_DIM, MXU_DIM))],
        out_specs=out_specs,
        out_shape=out_shape,
        compiler_params=_params("arbitrary", "arbitrary"),
        name="mixer_proj",
    )(x, mod, nw, w_in, gains, bd)


def _attend_block(kb, vtb, qms, dist, mask_of, m_sc, l_sc, acc_sc, slopes, k_lanes, v_rows):
    for g, qm in enumerate(qms):
        st = _dot(kb[:, k_lanes(g)], qm) - slopes[g] * dist
        mask = mask_of(g)
        if mask is not None:
            st = jnp.where(mask, st, NEG)
        m_old = m_sc[g:g + 1, :]
        m_new = jnp.maximum(m_old, jnp.max(st, axis=0, keepdims=True))
        alpha = jnp.exp(m_old - m_new)
        pt = jnp.exp(st - m_new)
        l_sc[g:g + 1, :] = alpha * l_sc[g:g + 1, :] + jnp.sum(pt, axis=0, keepdims=True)
        rows = v_rows(g)
        nr = rows.stop - rows.start
        acc_rows = slice(g * nr, (g + 1) * nr)
        acc_sc[acc_rows, :] = alpha * acc_sc[acc_rows, :] + _dot(vtb[rows, :], pt.astype(BF16))
        m_sc[g:g + 1, :] = m_new


def _masked_queries(qt):
    row = lax.broadcasted_iota(jnp.int32, (LANES, MOBA_BLOCK), 0)
    out = []
    for g in range(N_GROUPS):
        p, half = divmod(g, 2)
        pair = qt[LANES * p:LANES * (p + 1), :]
        keep = (row >= 64 * half) & (row < 64 * (half + 1))
        out.append(jnp.where(keep, pair, jnp.zeros_like(pair)))
    return out


def _moba_prompt_kernel(qt_ref, k_ref, vt_ref, km_ref, o_ref, sel_sc, m_sc, l_sc, acc_sc):
    qi = pl.program_id(1)
    nb = k_ref.shape[0]
    blk = MOBA_BLOCK
    slopes = _slopes(N_HEADS_A)
    qms = _masked_queries(qt_ref[...])
    key_i = lax.broadcasted_iota(jnp.int32, (blk, blk), 0)
    qry_i = lax.broadcasted_iota(jnp.int32, (blk, blk), 1)
    diag_dist = (qry_i - key_i).astype(F32)

    km = km_ref[...]
    km_hi = km.astype(BF16)
    km_lo = (km - km_hi.astype(F32)).astype(BF16)
    blk_i = lax.broadcasted_iota(jnp.int32, (nb, blk), 0)
    valid = blk_i < qi
    for g in range(N_GROUPS):
        lanes = slice(LANES * (g // 2), LANES * (g // 2 + 1))
        gate = _dot(km_hi[:, lanes], qms[g]) + _dot(km_lo[:, lanes], qms[g])
        gate = jnp.where(valid, gate, NEG)
        rank = jnp.zeros((nb, blk), jnp.int32)
        for m in range(nb):
            gm = gate[m:m + 1, :]
            beats = (gm > gate) | ((gm == gate) & (m < blk_i))
            rank = rank + beats.astype(jnp.int32)
        sel_sc[g] = (valid & (rank < MOBA_TOPK)).astype(F32)

    m_sc[...] = jnp.full(m_sc.shape, NEG, F32)
    l_sc[...] = jnp.zeros(l_sc.shape, F32)
    acc_sc[...] = jnp.zeros(acc_sc.shape, F32)
    k_lanes = lambda g: slice(LANES * (g // 2), LANES * (g // 2 + 1))
    v_rows = lambda g: slice(64 * g, 64 * (g + 1))

    causal = key_i <= qry_i
    _attend_block(k_ref[qi], vt_ref[qi], qms, diag_dist, lambda g: causal,
                  m_sc, l_sc, acc_sc, slopes, k_lanes, v_rows)

    def past(n, carry):
        dist = diag_dist + ((qi - n) * blk).astype(F32)
        _attend_block(k_ref[n], vt_ref[n], qms, dist,
                      lambda g: sel_sc[g, pl.ds(n, 1), :] > 0.5,
                      m_sc, l_sc, acc_sc, slopes, k_lanes, v_rows)
        return carry

    lax.fori_loop(0, qi, past, 0)

    parts = [acc_sc[64 * g:64 * (g + 1), :] * (1.0 / l_sc[g:g + 1, :]) for g in range(N_GROUPS)]
    o_ref[...] = jnp.concatenate(parts, axis=0).T.astype(BF16)


def _diff_prompt_kernel(qt_ref, k_ref, vt_ref, lam_ref, subln_ref, o_ref, m_sc, l_sc, acc_sc, *, lam_init):
    qi = pl.program_id(1)
    blk = MOBA_BLOCK
    slopes = [s for s in _slopes(N_HEADS_B) for _ in range(2)]
    qms = _masked_queries(qt_ref[...])
    key_i = lax.broadcasted_iota(jnp.int32, (blk, blk), 0)
    qry_i = lax.broadcasted_iota(jnp.int32, (blk, blk), 1)
    diag_dist = (qry_i - key_i).astype(F32)

    m_sc[...] = jnp.full(m_sc.shape, NEG, F32)
    l_sc[...] = jnp.zeros(l_sc.shape, F32)
    acc_sc[...] = jnp.zeros(acc_sc.shape, F32)
    k_lanes = lambda g: slice(LANES * (g // 2), LANES * (g // 2 + 1))
    v_rows = lambda g: slice(V_DIM_B * (g // 2), V_DIM_B * (g // 2 + 1))

    causal = key_i <= qry_i
    _attend_block(k_ref[qi], vt_ref[qi], qms, diag_dist, lambda g: causal,
                  m_sc, l_sc, acc_sc, slopes, k_lanes, v_rows)

    def past(n, carry):
        dist = diag_dist + ((qi - n) * blk).astype(F32)
        _attend_block(k_ref[n], vt_ref[n], qms, dist, lambda g: None,
                      m_sc, l_sc, acc_sc, slopes, k_lanes, v_rows)
        return carry

    lax.fori_loop(0, qi, past, 0)

    lam = _lambda(lam_ref, lam_init)
    parts = []
    for h in range(N_HEADS_B):
        o0 = acc_sc[V_DIM_B * (2 * h):V_DIM_B * (2 * h + 1), :] * (1.0 / l_sc[2 * h:2 * h + 1, :])
        o1 = acc_sc[V_DIM_B * (2 * h + 1):V_DIM_B * (2 * h + 2), :] * (1.0 / l_sc[2 * h + 1:2 * h + 2, :])
        o = o0 - lam * o1
        ms = jnp.mean(o * o, axis=0, keepdims=True)
        parts.append(o * lax.rsqrt(ms + RMS_EPS))
    o_t = jnp.concatenate(parts, axis=0).T
    o_ref[...] = (o_t * subln_ref[...] * (1.0 - lam_init)).astype(BF16)


def _lambda(lam_ref, lam_init):
    a = jnp.sum(lam_ref[0:1, :] * lam_ref[1:2, :], axis=-1, keepdims=True)
    b = jnp.sum(lam_ref[2:3, :] * lam_ref[3:4, :], axis=-1, keepdims=True)
    return jnp.exp(a) - jnp.exp(b) + lam_init


def _prompt_attention(qt_a, kbf_a, vt_a, km_a, qt_b, kbf_b, vt_b, lam_vecs, subln_row, lam_init):
    b, nb = qt_a.shape[:2]
    blk = MOBA_BLOCK
    s = nb * blk
    q_spec = pl.BlockSpec((None, None, WIDTH, blk), lambda i, t: (i, t, 0, 0))
    k_spec = pl.BlockSpec((None, nb, blk, WIDTH), lambda i, t: (i, 0, 0, 0))
    vt_spec = pl.BlockSpec((None, nb, WIDTH, blk), lambda i, t: (i, 0, 0, 0))
    o_spec = pl.BlockSpec((None, blk, WIDTH), lambda i, t: (i, t, 0))
    o_shape = jax.ShapeDtypeStruct((b, s, WIDTH), BF16)
    stat = pltpu.VMEM((N_GROUPS, blk), F32)
    o_a = pl.pallas_call(
        _moba_prompt_kernel,
        grid=(b, nb),
        in_specs=[q_spec, k_spec, vt_spec, pl.BlockSpec((None, nb, WIDTH), lambda i, t: (i, 0, 0))],
        out_specs=o_spec,
        out_shape=o_shape,
        scratch_shapes=[pltpu.VMEM((N_GROUPS, nb, blk), F32), stat, stat,
                        pltpu.VMEM((WIDTH, blk), F32)],
        compiler_params=_params("arbitrary", "arbitrary"),
        name="moba_prompt",
    )(qt_a, kbf_a, vt_a, km_a)
    o_b = pl.pallas_call(
        functools.partial(_diff_prompt_kernel, lam_init=lam_init),
        grid=(b, nb),
        in_specs=[q_spec, k_spec, vt_spec, _const_spec((4, HEAD_DIM_B)), _const_spec((1, WIDTH))],
        out_specs=o_spec,
        out_shape=o_shape,
        scratch_shapes=[stat, stat, pltpu.VMEM((N_GROUPS * V_DIM_B, blk), F32)],
        compiler_params=_params("arbitrary", "arbitrary"),
        name="diff_prompt",
    )(qt_b, kbf_b, vt_b, lam_vecs, subln_row)
    return o_a, o_b


def _mix_ffn_kernel(x_ref, oa_ref, ob_ref, ga_ref, gb_ref, mod_ref, wba_ref, wbd_ref, wout_ref,
                    nw_ref, wg_ref, wu_ref, wd_ref, o_ref, *, tf):
    y_a = _dot(oa_ref[...], wba_ref[...])
    y_b = _dot(ob_ref[...], wbd_ref[...])
    mixed = _dot((ga_ref[...] * y_a + gb_ref[...] * y_b).astype(BF16), wout_ref[...])
    x = x_ref[...] + mod_ref[5] * mixed
    o_ref[...] = _ffn_update(x, mod_ref[6], mod_ref[7], mod_ref[8], nw_ref[...],
                             wg_ref, wu_ref, wd_ref, tf)


def _mix_ffn(x, o_a, o_b, g_a, g_b, mod, w_ba, w_bd, w_out, nw, wg, wu, wd, tm):
    b, s, d = x.shape
    r = mod.shape[2]
    d_ff = wg.shape[1]
    tf = _pick_ff_tile(d_ff)
    row_spec = lambda w: pl.BlockSpec((None, tm, w), lambda i, t: (i, t, 0))
    return pl.pallas_call(
        functools.partial(_mix_ffn_kernel, tf=tf),
        grid=(b, s // tm),
        in_specs=[row_spec(d), row_spec(WIDTH), row_spec(WIDTH), row_spec(d), row_spec(d),
                  pl.BlockSpec((None, N_MOD, r, d), lambda i, t: (i, 0, 0, 0)),
                  _const_spec((WIDTH, d)), _const_spec((WIDTH, d)), _const_spec((d, d)),
                  _const_spec((1, d)),
                  _const_spec((d, d_ff)), _const_spec((d, d_ff)), _const_spec((d_ff, d))],
        out_specs=row_spec(d),
        out_shape=jax.ShapeDtypeStruct((b, s, d), F32),
        compiler_params=_params("arbitrary", "arbitrary"),
        name="mix_ffn",
    )(x, o_a, o_b, g_a, g_b, mod, w_ba, w_bd, w_out, nw, wg, wu, wd)


def _page_scores(k_page, qblk_t, slope_row, dead_row, t_iota, dist0):
    s = _dot_nt(k_page.astype(BF16), qblk_t)
    return s - slope_row * (dist0 - t_iota) + dead_row


def _moba_decode_kernel(pt_ref, q_ref, e_ref, slope_ref, dead_ref, *rest, past_len):
    del pt_ref
    npg = PAGES_PER_STEP
    ppb = MOBA_BLOCK // PAGE_SIZE
    k_refs, v_refs = rest[:npg], rest[npg:2 * npg]
    m_ref, l_ref, o_ref, km_ref = rest[2 * npg:]
    j = pl.program_id(1)
    e = e_ref[...]
    qblk_t = (q_ref[...] * e.astype(F32)).astype(BF16)
    slope_row, dead_row = slope_ref[...], dead_ref[...]
    t_iota = lax.broadcasted_iota(jnp.int32, (PAGE_SIZE, LANES), 0).astype(F32)
    for bi in range(npg // ppb):
        ksum = jnp.zeros((1, WIDTH), F32)
        scores = []
        for pi in range(ppb):
            idx = bi * ppb + pi
            k_page = k_refs[idx][...]
            ksum = ksum + jnp.sum(k_page, axis=0, keepdims=True)
            dist0 = (past_len - (j * npg + idx) * PAGE_SIZE).astype(F32)
            scores.append(_page_scores(k_page, qblk_t, slope_row, dead_row, t_iota, dist0))
        m = functools.reduce(jnp.maximum, [jnp.max(s, axis=0, keepdims=True) for s in scores])
        l = jnp.zeros((1, LANES), F32)
        o = jnp.zeros((1, WIDTH), F32)
        for pi in range(ppb):
            p = jnp.exp(scores[pi] - m)
            l = l + jnp.sum(p, axis=0, keepdims=True)
            w = _dot(p.astype(BF16), e)
            o = o + jnp.sum(w * v_refs[bi * ppb + pi][...], axis=0, keepdims=True)
        m_ref[bi:bi + 1, :] = m
        l_ref[bi:bi + 1, :] = l
        o_ref[bi:bi + 1, :] = o
        km_ref[bi:bi + 1, :] = ksum * (1.0 / MOBA_BLOCK)


def _diff_decode_kernel(pt_ref, q_ref, kn_ref, vn_ref, e_ref, e0_ref, e1_ref, slope_ref, dead_ref, *rest,
                        past_len):
    del pt_ref
    npg = PAGES_PER_STEP
    k_refs, v_refs = rest[:npg], rest[npg:2 * npg]
    o0_ref, o1_ref, m_sc, l_sc, a0_sc, a1_sc = rest[2 * npg:]
    j = pl.program_id(1)
    e, e0, e1 = e_ref[...], e0_ref[...], e1_ref[...]
    qblk_t = (q_ref[...] * e.astype(F32)).astype(BF16)
    slope_row, dead_row = slope_ref[...], dead_ref[...]
    t_iota = lax.broadcasted_iota(jnp.int32, (PAGE_SIZE, LANES), 0).astype(F32)

    @pl.when(j == 0)
    def _():
        k_new = jnp.broadcast_to(kn_ref[...], (8, WIDTH)).astype(BF16)
        m_sc[...] = _dot_nt(k_new, qblk_t)[0:1, :] + dead_row
        l_sc[...] = jnp.ones(l_sc.shape, F32)
        a0_sc[...] = vn_ref[...]
        a1_sc[...] = vn_ref[...]

    scores = []
    for idx in range(npg):
        dist0 = (past_len - (j * npg + idx) * PAGE_SIZE).astype(F32)
        scores.append(_page_scores(k_refs[idx][...], qblk_t, slope_row, dead_row, t_iota, dist0))
    m_old = m_sc[...]
    m_new = functools.reduce(jnp.maximum, [jnp.max(s, axis=0, keepdims=True) for s in scores] + [m_old])
    alpha = jnp.exp(m_old - m_new)
    alpha8 = jnp.broadcast_to(alpha, (8, LANES))
    l = alpha * l_sc[...]
    a0 = _dot_split(alpha8, e0)[0:1, :] * a0_sc[...]
    a1 = _dot_split(alpha8, e1)[0:1, :] * a1_sc[...]
    for idx in range(npg):
        p = jnp.exp(scores[idx] - m_new)
        l = l + jnp.sum(p, axis=0, keepdims=True)
        pb = p.astype(BF16)
        v_page = v_refs[idx][...]
        a0 = a0 + jnp.sum(_dot(pb, e0) * v_page, axis=0, keepdims=True)
        a1 = a1 + jnp.sum(_dot(pb, e1) * v_page, axis=0, keepdims=True)
    m_sc[...] = m_new
    l_sc[...] = l
    a0_sc[...] = a0
    a1_sc[...] = a1

    @pl.when(j == pl.num_programs(1) - 1)
    def _():
        inv8 = jnp.broadcast_to(1.0 / l, (8, LANES))
        o0_ref[...] = a0 * _dot_split(inv8, e0)[0:1, :]
        o1_ref[...] = a1 * _dot_split(inv8, e1)[0:1, :]


def _page_specs(n_phys_layer_base):
    def spec(i):
        return pl.BlockSpec((None, PAGE_SIZE, WIDTH),
                            lambda s, j, pt: (n_phys_layer_base + pt[s, j * PAGES_PER_STEP + i], 0, 0))
    return [spec(i) for i in range(PAGES_PER_STEP)]


def _decode_consts():
    e = np.zeros((LANES, WIDTH), np.float32)
    e0 = np.zeros((LANES, WIDTH), np.float32)
    e1 = np.zeros((LANES, WIDTH), np.float32)
    for g in range(N_GROUPS):
        e[g, 64 * g:64 * (g + 1)] = 1.0
    for h in range(N_HEADS_B):
        e0[2 * h, V_DIM_B * h:V_DIM_B * (h + 1)] = 1.0
        e1[2 * h + 1, V_DIM_B * h:V_DIM_B * (h + 1)] = 1.0
    dead = np.where(np.arange(LANES) < N_GROUPS, 0.0, NEG).astype(np.float32).reshape(1, LANES)
    slope_a = np.zeros((1, LANES), np.float32)
    slope_b = np.zeros((1, LANES), np.float32)
    slope_a[0, :N_GROUPS] = _slopes(N_HEADS_A)
    slope_b[0, :N_GROUPS] = [s for s in _slopes(N_HEADS_B) for _ in range(2)]
    as_bf = lambda a: jnp.asarray(a, BF16)
    return as_bf(e), as_bf(e0), as_bf(e1), jnp.asarray(dead), jnp.asarray(slope_a), jnp.asarray(slope_b)


def _moba_decode(q, k_pool, v_pool, page_table, layer_base, consts):
    e, _, _, dead, slope_a, _ = consts
    db, n_pages = page_table.shape
    nblk = n_pages * PAGE_SIZE // MOBA_BLOCK
    bps = PAGES_PER_STEP * PAGE_SIZE // MOBA_BLOCK
    row = lambda w: pl.BlockSpec((None, bps, w), lambda s, j, pt: (s, j, 0))
    grid_spec = pltpu.PrefetchScalarGridSpec(
        num_scalar_prefetch=1,
        grid=(db, n_pages // PAGES_PER_STEP),
        in_specs=[pl.BlockSpec((None, 1, WIDTH), lambda s, j, pt: (s, 0, 0)),
                  pl.BlockSpec((LANES, WIDTH), lambda s, j, pt: (0, 0)),
                  pl.BlockSpec((1, LANES), lambda s, j, pt: (0, 0)),
                  pl.BlockSpec((1, LANES), lambda s, j, pt: (0, 0))]
        + _page_specs(layer_base) + _page_specs(layer_base),
        out_specs=[row(LANES), row(LANES), row(WIDTH), row(WIDTH)],
    )
    stat = jax.ShapeDtypeStruct((db, nblk, LANES), F32)
    wide = jax.ShapeDtypeStruct((db, nblk, WIDTH), F32)
    return pl.pallas_call(
        functools.partial(_moba_decode_kernel, past_len=n_pages * PAGE_SIZE),
        grid_spec=grid_spec,
        out_shape=[stat, stat, wide, wide],
        compiler_params=_params("arbitrary", "arbitrary"),
        name="moba_decode",
    )(page_table, q, e, slope_a, dead, *([k_pool] * PAGES_PER_STEP), *([v_pool] * PAGES_PER_STEP))


def _diff_decode(q, k_new, v_new, k_pool, v_pool, page_table, layer_base, consts):
    e, e0, e1, dead, _, slope_b = consts
    db, n_pages = page_table.shape
    seq_row = pl.BlockSpec((None, 1, WIDTH), lambda s, j, pt: (s, 0, 0))
    cst = lambda shp: pl.BlockSpec(shp, lambda s, j, pt: (0, 0))
    grid_spec = pltpu.PrefetchScalarGridSpec(
        num_scalar_prefetch=1,
        grid=(db, n_pages // PAGES_PER_STEP),
        in_specs=[seq_row, seq_row, seq_row, cst((LANES, WIDTH)), cst((LANES, WIDTH)), cst((LANES, WIDTH)),
                  cst((1, LANES)), cst((1, LANES))]
        + _page_specs(layer_base) + _page_specs(layer_base),
        out_specs=[seq_row, seq_row],
        scratch_shapes=[pltpu.VMEM((1, LANES), F32), pltpu.VMEM((1, LANES), F32),
                        pltpu.VMEM((1, WIDTH), F32), pltpu.VMEM((1, WIDTH), F32)],
    )
    wide = jax.ShapeDtypeStruct((db, 1, WIDTH), F32)
    return pl.pallas_call(
        functools.partial(_diff_decode_kernel, past_len=n_pages * PAGE_SIZE),
        grid_spec=grid_spec,
        out_shape=[wide, wide],
        compiler_params=_params("arbitrary", "arbitrary"),
        name="diff_decode",
    )(page_table, q, k_new, v_new, e, e0, e1, slope_b, dead,
      *([k_pool] * PAGES_PER_STEP), *([v_pool] * PAGES_PER_STEP))


def _sample_combine_kernel(q_ref, kn_ref, vn_ref, m_ref, l_ref, o_ref, km_ref, o0_ref, o1_ref,
                           e_ref, dead_ref, bd_ref, lam_ref, subln_ref, oa_ref, ob_ref, *, lam_init):
    db, nblk = m_ref.shape[:2]
    e = e_ref[...]
    q = q_ref[...]

    gate = _dot_nt_split((km_ref[...] * q[:, None, :]).reshape(db * nblk, WIDTH), e)
    gate = gate.reshape(db, nblk, LANES)
    blk_i = lax.broadcasted_iota(jnp.int32, (db, nblk, LANES), 1)
    rank = jnp.zeros((db, nblk, LANES), jnp.int32)
    for m in range(nblk):
        gm = gate[:, m:m + 1, :]
        beats = (gm > gate) | ((gm == gate) & (m < blk_i))
        rank = rank + beats.astype(jnp.int32)
    sel = rank < MOBA_TOPK
    s_self = _dot_nt_split(q * kn_ref[...], e) + dead_ref[...]
    m_blk = jnp.where(sel, m_ref[...], NEG)
    m_tot = jnp.maximum(jnp.max(m_blk, axis=1), s_self)
    w_blk = jnp.where(sel, jnp.exp(m_blk - m_tot[:, None, :]), 0.0)
    w_self = jnp.exp(s_self - m_tot)
    l_tot = jnp.sum(w_blk * l_ref[...], axis=1) + w_self
    w_wide = _dot_split(w_blk.reshape(db * nblk, LANES), e).reshape(db, nblk, WIDTH)
    acc = jnp.sum(w_wide * o_ref[...], axis=1) + _dot_split(w_self, e) * vn_ref[...]
    oa_ref[...] = (acc * _dot_split(1.0 / l_tot, e)).astype(BF16)

    lam = _lambda(lam_ref, lam_init)
    o = o0_ref[...] - lam * o1_ref[...]
    bd = bd_ref[...]
    parts = [_dot_split((o * o)[:, LANES * h:LANES * (h + 1)], bd) for h in range(N_HEADS_B)]
    ms = jnp.concatenate(parts, axis=1) * (1.0 / V_DIM_B)
    ob_ref[...] = (o * lax.rsqrt(ms + RMS_EPS) * subln_ref[...] * (1.0 - lam_init)).astype(BF16)


def _sample_combine(q_a, k_new, v_new, m, l, o, km, o0, o1, consts, lam_vecs, subln_row, lam_init):
    e, _, _, dead, _, _ = consts
    db = q_a.shape[0]
    ones = jnp.ones((LANES, LANES), BF16)
    args = (q_a, k_new, v_new, m, l, o, km, o0, o1, e, dead, ones, lam_vecs, subln_row)
    out = jax.ShapeDtypeStruct((db, WIDTH), BF16)
    return pl.pallas_call(
        functools.partial(_sample_combine_kernel, lam_init=lam_init),
        grid=(1,),
        in_specs=[_const_spec(a.shape) for a in args],
        out_specs=[_const_spec((db, WIDTH))] * 2,
        out_shape=[out, out],
        compiler_params=_params("arbitrary"),
        name="sample_combine",
    )(*args)


def _block_diag_ones():
    i = np.arange(MXU_DIM) // 64
    return jnp.asarray((i[:, None] == i[None, :]).astype(np.float32), BF16)


def kernel(x_prompt, x_sample, cache_k_moba, cache_v_moba, cache_k_diff, cache_v_diff, page_table, c_prompt, c_sample, w_ada, b_ada, norm_ffn1, ffn1_w_gate, ffn1_w_up, ffn1_w_down, norm_mix, w_in, qn_moba, kn_moba, qn_diff, kn_diff, lambda_q1, lambda_k1, lambda_q2, lambda_k2, subln_diff, w_branch_moba, w_branch_diff, w_out, norm_ffn2, ffn2_w_gate, ffn2_w_up, ffn2_w_down):
    depth = w_ada.shape[0]
    b, s, d = x_prompt.shape
    db, t_new, _ = x_sample.shape
    assert t_new == 1 and s % TOKEN_TILE == 0 and db % 8 == 0
    n_phys = cache_k_moba.shape[1]
    n_pages = page_table.shape[1]
    assert n_pages % PAGES_PER_STEP == 0
    pools = [c.reshape(depth * n_phys, PAGE_SIZE, WIDTH)
             for c in (cache_k_moba, cache_v_moba, cache_k_diff, cache_v_diff)]
    consts = _decode_consts()
    bd = _block_diag_ones()
    tile8 = lambda v: jnp.tile(v, WIDTH // v.shape[0]).reshape(1, WIDTH)

    y_p, y_s = x_prompt, x_sample.reshape(1, db, d)
    rows_p, rows_s = [], []
    for l in range(depth):
        lam_init = _lambda_init(l)
        bf = lambda w: w[l].astype(BF16)
        row = lambda v: v[l].reshape(1, -1)
        mod = _ada(jnp.concatenate([c_prompt, c_sample], axis=0), bf(w_ada), b_ada[l])
        mod_p = mod[:b].reshape(b, N_MOD, 1, d)
        mod_s = mod[b:].reshape(db, N_MOD, d).transpose(1, 0, 2).reshape(1, N_MOD, db, d)
        ffn1 = (row(norm_ffn1), bf(ffn1_w_gate), bf(ffn1_w_up), bf(ffn1_w_down))
        ffn2 = (row(norm_ffn2), bf(ffn2_w_gate), bf(ffn2_w_up), bf(ffn2_w_down))
        w_in_bf = bf(w_in)
        gains = jnp.concatenate([tile8(qn_moba[l]), tile8(kn_moba[l]), tile8(qn_diff[l]), tile8(kn_diff[l])], 0)
        lam_vecs = jnp.stack([lambda_q1[l], lambda_k1[l], lambda_q2[l], lambda_k2[l]])
        subln_row = tile8(subln_diff[l])
        mix_w = (bf(w_branch_moba), bf(w_branch_diff), bf(w_out))

        x1 = _ffn(y_p, mod_p, *ffn1, k0=0, tm=TOKEN_TILE)
        (k_a, v_a, k_b, v_b, kbf_a, kbf_b, qt_a, qt_b, vt_a, vt_b, km_a, g_a, g_b) = _proj(
            x1, mod_p, row(norm_mix), w_in_bf, gains, bd, TOKEN_TILE, True)
        o_a, o_b = _prompt_attention(qt_a, kbf_a, vt_a, km_a.reshape(b, -1, WIDTH), qt_b, kbf_b, vt_b,
                                     lam_vecs, subln_row, lam_init)
        y_p = _mix_ffn(x1, o_a, o_b, g_a, g_b, mod_p, *mix_w, *ffn2, tm=TOKEN_TILE)
        rows_p.append((k_a, v_a, k_b, v_b))

        x1s = _ffn(y_s, mod_s, *ffn1, k0=0, tm=db)
        (q_as, k_as, v_as, q_bs, k_bs, v_bs, g_as, g_bs) = _proj(
            x1s, mod_s, row(norm_mix), w_in_bf, gains, bd, db, False)
        per_seq = lambda a: a.reshape(db, 1, WIDTH)
        m_blk, l_blk, o_blk, km_blk = _moba_decode(per_seq(q_as), pools[0], pools[1], page_table,
                                                   l * n_phys, consts)
        o0, o1 = _diff_decode(per_seq(q_bs), per_seq(k_bs), per_seq(v_bs), pools[2], pools[3], page_table,
                              l * n_phys, consts)
        flat = lambda a: a.reshape(db, WIDTH)
        o_as, o_bs = _sample_combine(flat(q_as), flat(k_as), flat(v_as), m_blk, l_blk, o_blk, km_blk,
                                     flat(o0), flat(o1), consts, lam_vecs, subln_row, lam_init)
        y_s = _mix_ffn(x1s, o_as.reshape(1, db, WIDTH), o_bs.reshape(1, db, WIDTH), g_as, g_bs, mod_s,
                       *mix_w, *ffn2, tm=db)
        rows_s.append((k_as, v_as, k_bs, v_bs))

    def stack(rows, i, lead, heads, hd):
        return jnp.stack([r[i].reshape(lead + (heads, hd)) for r in rows])

    lp, ls = (b, s), (db, 1)
    return (y_p, y_s.reshape(db, 1, d),
            stack(rows_p, 0, lp, N_HEADS_A, HEAD_DIM_A), stack(rows_p, 1, lp, N_HEADS_A, HEAD_DIM_A),
            stack(rows_p, 2, lp, N_HEADS_B, V_DIM_B), stack(rows_p, 3, lp, N_HEADS_B, V_DIM_B),
            stack(rows_s, 0, ls, N_HEADS_A, HEAD_DIM_A), stack(rows_s, 1, ls, N_HEADS_A, HEAD_DIM_A),
            stack(rows_s, 2, ls, N_HEADS_B, V_DIM_B), stack(rows_s, 3, ls, N_HEADS_B, V_DIM_B))
```

```python
import functools
import math

import jax
import jax.numpy as jnp
import numpy as np
from jax import lax
from jax.experimental import pallas as pl
from jax.experimental.pallas import tpu as pltpu

F32 = jnp.float32
BF16 = jnp.bfloat16

N_HEADS_A = 8
HEAD_DIM_A = 64
MOBA_BLOCK = 256
MOBA_TOPK = 3
N_HEADS_B = 4
HEAD_DIM_B = 64
V_DIM_B = 2 * HEAD_DIM_B
WIDTH = 512
N_GROUPS = WIDTH // 64
PAGE_SIZE = 128
N_MOD = 9
RMS_EPS = 1e-6
QK_SCALE = 0.125
NEG = -1e30

LANES = 128
SUBLANES = 8
MXU_DIM = 256
VMEM_LIMIT_BYTES = 56 * 1024 * 1024

TOKEN_TILE = 512
PAGES_PER_STEP = 8
SEQS_PER_COMBINE_STEP = 8


def _slopes(n_heads):
    return [2.0 ** (-8.0 * (i + 1) / n_heads) for i in range(n_heads)]


def _lambda_init(layer):
    return 0.8 - 0.6 * math.exp(-0.3 * layer)


def _dot(a, b):
    return jnp.dot(a, b, preferred_element_type=F32)


def _rms(x, w):
    ms = jnp.mean(x * x, axis=-1, keepdims=True)
    return x * lax.rsqrt(ms + RMS_EPS) * w


def _pick_ff_tile(d_ff):
    best = LANES
    for t in range(LANES, min(d_ff, 1408) + 1, LANES):
        if d_ff % t == 0:
            best = t
    return best


def _params(*sem):
    return pltpu.CompilerParams(dimension_semantics=sem, vmem_limit_bytes=VMEM_LIMIT_BYTES)


def _const_spec(shape):
    nd = len(shape)
    return pl.BlockSpec(shape, lambda *_: (0,) * nd)


def _ada_kernel(c_ref, w_ref, b_ref, o_ref):
    c = c_ref[...]
    s = c * jax.nn.sigmoid(c)
    o_ref[...] = _dot(s.astype(BF16), w_ref[...]) + b_ref[...]


def _ada(c, w_bf, b):
    m, d = c.shape
    n = w_bf.shape[1]
    tn = 1024 if n % 1024 == 0 else n
    return pl.pallas_call(
        _ada_kernel,
        grid=(n // tn,),
        in_specs=[pl.BlockSpec((m, d), lambda j: (0, 0)),
                  pl.BlockSpec((d, tn), lambda j: (0, j)),
                  pl.BlockSpec((1, tn), lambda j: (0, j))],
        out_specs=pl.BlockSpec((m, tn), lambda j: (0, j)),
        out_shape=jax.ShapeDtypeStruct((m, n), F32),
        compiler_params=_params("arbitrary"),
        name="ada_mod",
    )(c, w_bf, b.reshape(1, n))


def _ffn_update(x, shift, scale, gate, nw, wg_ref, wu_ref, wd_ref, tf):
    h = (_rms(x, nw) * (1.0 + scale) + shift).astype(BF16)
    d_ff = wg_ref.shape[1]
    acc = jnp.zeros(x.shape, F32)
    for j in range(d_ff // tf):
        g = _dot(h, wg_ref[:, j * tf:(j + 1) * tf])
        u = _dot(h, wu_ref[:, j * tf:(j + 1) * tf])
        a = (g * jax.nn.sigmoid(g) * u).astype(BF16)
        acc = acc + _dot(a, wd_ref[j * tf:(j + 1) * tf, :])
    return x + 0.5 * gate * acc


def _ffn_kernel(x_ref, mod_ref, nw_ref, wg_ref, wu_ref, wd_ref, o_ref, *, k0, tf):
    o_ref[...] = _ffn_update(x_ref[...], mod_ref[k0], mod_ref[k0 + 1], mod_ref[k0 + 2],
                             nw_ref[...], wg_ref, wu_ref, wd_ref, tf)


def _ffn(x, mod, nw, wg, wu, wd, k0, tm):
    b, s, d = x.shape
    r = mod.shape[2]
    d_ff = wg.shape[1]
    tf = _pick_ff_tile(d_ff)
    return pl.pallas_call(
        functools.partial(_ffn_kernel, k0=k0, tf=tf),
        grid=(b, s // tm),
        in_specs=[pl.BlockSpec((None, tm, d), lambda i, t: (i, t, 0)),
                  pl.BlockSpec((None, N_MOD, r, d), lambda i, t: (i, 0, 0, 0)),
                  _const_spec((1, d)),
                  _const_spec((d, d_ff)), _const_spec((d, d_ff)), _const_spec((d_ff, d))],
        out_specs=pl.BlockSpec((None, tm, d), lambda i, t: (i, t, 0)),
        out_shape=jax.ShapeDtypeStruct((b, s, d), F32),
        compiler_params=_params("arbitrary", "arbitrary"),
        name="ffn",
    )(x, mod, nw, wg, wu, wd)


def _head_norm(seg, gain_row, bd):
    sq = (seg * seg).astype(BF16)
    parts = [_dot(sq[:, c * MXU_DIM:(c + 1) * MXU_DIM], bd) for c in range(WIDTH // MXU_DIM)]
    ms = jnp.concatenate(parts, axis=1) * (1.0 / 64.0)
    return seg * lax.rsqrt(ms + RMS_EPS) * gain_row


def _proj_kernel(x_ref, mod_ref, nw_ref, win_ref, gains_ref, bd_ref, *outs, transposed):
    x = x_ref[...]
    d = x.shape[1]
    h = (_rms(x, nw_ref[...]) * (1.0 + mod_ref[4]) + mod_ref[3]).astype(BF16)
    bd = bd_ref[...]

    def seg(j):
        return _dot(h, win_ref[:, j * WIDTH:(j + 1) * WIDTH])

    q_a = _head_norm(seg(0), gains_ref[0:1, :], bd) * QK_SCALE
    k_a = _head_norm(seg(1), gains_ref[1:2, :], bd)
    v_a = seg(2)
    q_b = _head_norm(seg(3), gains_ref[2:3, :], bd) * QK_SCALE
    k_b = _head_norm(seg(4), gains_ref[3:4, :], bd)
    v_b = seg(5)
    g0 = 6 * WIDTH
    g_a = jax.nn.sigmoid(_dot(h, win_ref[:, g0:g0 + d]))
    g_b = jax.nn.sigmoid(_dot(h, win_ref[:, g0 + d:g0 + 2 * d]))

    if not transposed:
        (qa_ref, ka_ref, va_ref, qb_ref, kb_ref, vb_ref, ga_ref, gb_ref) = outs
        qa_ref[...] = q_a
        qb_ref[...] = q_b
    else:
        (ka_ref, va_ref, kb_ref, vb_ref, kabf_ref, kbbf_ref, qta_ref, qtb_ref,
         vta_ref, vtb_ref, km_ref, ga_ref, gb_ref) = outs
        for r in range(x.shape[0] // MOBA_BLOCK):
            rows = slice(r * MOBA_BLOCK, (r + 1) * MOBA_BLOCK)
            kabf_ref[r] = k_a[rows].astype(BF16)
            kbbf_ref[r] = k_b[rows].astype(BF16)
            qta_ref[r] = q_a[rows].T.astype(BF16)
            qtb_ref[r] = q_b[rows].T.astype(BF16)
            vta_ref[r] = v_a[rows].T.astype(BF16)
            vtb_ref[r] = v_b[rows].T.astype(BF16)
            km_ref[r] = jnp.sum(k_a[rows], axis=0, keepdims=True) * (1.0 / MOBA_BLOCK)
    ka_ref[...] = k_a
    va_ref[...] = v_a
    kb_ref[...] = k_b
    vb_ref[...] = v_b
    ga_ref[...] = g_a
    gb_ref[...] = g_b


def _proj(x, mod, nw, w_in, gains, bd, tm, transposed):
    b, s, d = x.shape
    r = mod.shape[2]
    d_in = w_in.shape[1]
    row_spec = lambda w: pl.BlockSpec((None, tm, w), lambda i, t: (i, t, 0))
    rows = lambda w, dt=F32: jax.ShapeDtypeStruct((b, s, w), dt)
    if transposed:
        nb, bpt = s // MOBA_BLOCK, tm // MOBA_BLOCK
        blk = lambda shp: pl.BlockSpec((None, bpt) + shp, lambda i, t: (i, t, 0, 0))
        out_specs = [row_spec(WIDTH)] * 4 + [blk((MOBA_BLOCK, WIDTH))] * 2 + [blk((WIDTH, MOBA_BLOCK))] * 4 \
            + [blk((1, WIDTH))] + [row_spec(d)] * 2
        out_shape = [rows(WIDTH)] * 4 \
            + [jax.ShapeDtypeStruct((b, nb, MOBA_BLOCK, WIDTH), BF16)] * 2 \
            + [jax.ShapeDtypeStruct((b, nb, WIDTH, MOBA_BLOCK), BF16)] * 4 \
            + [jax.ShapeDtypeStruct((b, nb, 1, WIDTH), F32)] + [rows(d)] * 2
    else:
        out_specs = [row_spec(WIDTH)] * 6 + [row_spec(d)] * 2
        out_shape = [rows(WIDTH)] * 6 + [rows(d)] * 2
    return pl.pallas_call(
        functools.partial(_proj_kernel, transposed=transposed),
        grid=(b, s // tm),
        in_specs=[pl.BlockSpec((None, tm, d), lambda i, t: (i, t, 0)),
                  pl.BlockSpec((None, N_MOD, r, d), lambda i, t: (i, 0, 0, 0)),
                  _const_spec((1, d)), _const_spec((d, d_in)),
                  _const_spec((4, WIDTH)), _const_spec((MXU_DIM, MXU_DIM))],
        out_specs=out_specs,
        out_shape=out_shape,
        compiler_params=_params("arbitrary", "arbitrary"),
        name="mixer_proj",
    )(x, mod, nw, w_in, gains, bd)


def _attend_block(kb, vtb, qms, dist, mask_of, m_sc, l_sc, acc_sc, slopes, k_lanes, v_rows):
    for g, qm in enumerate(qms):
        st = _dot(kb[:, k_lanes(g)], qm) - slopes[g] * dist
        mask = mask_of(g)
        if mask is not None:
            st = jnp.where(mask, st, NEG)
        m_old = m_sc[g:g + 1, :]
        m_new = jnp.maximum(m_old, jnp.max(st, axis=0, keepdims=True))
        alpha = jnp.exp(m_old - m_new)
        pt = jnp.exp(st - m_new)
        l_sc[g:g + 1, :] = alpha * l_sc[g:g + 1, :] + jnp.sum(pt, axis=0, keepdims=True)
        rows = v_rows(g)
        nr = rows.stop - rows.start
        acc_rows = slice(g * nr, (g + 1) * nr)
        acc_sc[acc_rows, :] = alpha * acc_sc[acc_rows, :] + _dot(vtb[rows, :], pt.astype(BF16))
        m_sc[g:g + 1, :] = m_new


def _masked_queries(qt):
    row = lax.broadcasted_iota(jnp.int32, (LANES, MOBA_BLOCK), 0)
    out = []
    for g in range(N_GROUPS):
        p, half = divmod(g, 2)
        pair = qt[LANES * p:LANES * (p + 1), :]
        keep = (row >= 64 * half) & (row < 64 * (half + 1))
        out.append(jnp.where(keep, pair, jnp.zeros_like(pair)))
    return out


def _moba_prompt_kernel(qt_ref, k_ref, vt_ref, km_ref, o_ref, sel_sc, m_sc, l_sc, acc_sc):
    qi = pl.program_id(1)
    nb = k_ref.shape[0]
    blk = MOBA_BLOCK
    slopes = _slopes(N_HEADS_A)
    qms = _masked_queries(qt_ref[...])
    key_i = lax.broadcasted_iota(jnp.int32, (blk, blk), 0)
    qry_i = lax.broadcasted_iota(jnp.int32, (blk, blk), 1)
    diag_dist = (qry_i - key_i).astype(F32)

    km = km_ref[...].astype(BF16)
    blk_i = lax.broadcasted_iota(jnp.int32, (nb, blk), 0)
    valid = blk_i < qi
    for g in range(N_GROUPS):
        lanes = slice(LANES * (g // 2), LANES * (g // 2 + 1))
        gate = _dot(km[:, lanes], qms[g])
        gate = jnp.where(valid, gate, NEG)
        rank = jnp.zeros((nb, blk), jnp.int32)
        for m in range(nb):
            gm = gate[m:m + 1, :]
            beats = (gm > gate) | ((gm == gate) & (m < blk_i))
            rank = rank + beats.astype(jnp.int32)
        sel_sc[g] = (valid & (rank < MOBA_TOPK)).astype(F32)

    m_sc[...] = jnp.full(m_sc.shape, NEG, F32)
    l_sc[...] = jnp.zeros(l_sc.shape, F32)
    acc_sc[...] = jnp.zeros(acc_sc.shape, F32)
    k_lanes = lambda g: slice(LANES * (g // 2), LANES * (g // 2 + 1))
    v_rows = lambda g: slice(64 * g, 64 * (g + 1))

    causal = key_i <= qry_i
    _attend_block(k_ref[qi], vt_ref[qi], qms, diag_dist, lambda g: causal,
                  m_sc, l_sc, acc_sc, slopes, k_lanes, v_rows)

    def past(n, carry):
        dist = diag_dist + ((qi - n) * blk).astype(F32)
        _attend_block(k_ref[n], vt_ref[n], qms, dist,
                      lambda g: sel_sc[g, pl.ds(n, 1), :] > 0.5,
                      m_sc, l_sc, acc_sc, slopes, k_lanes, v_rows)
        return carry

    lax.fori_loop(0, qi, past, 0)

    parts = [acc_sc[64 * g:64 * (g + 1), :] * (1.0 / l_sc[g:g + 1, :]) for g in range(N_GROUPS)]
    o_ref[...] = jnp.concatenate(parts, axis=0).T.astype(BF16)


def _lambda(lam_ref, lam_init):
    a = jnp.sum(lam_ref[0:1, :] * lam_ref[1:2, :], axis=-1, keepdims=True)
    b = jnp.sum(lam_ref[2:3, :] * lam_ref[3:4, :], axis=-1, keepdims=True)
    return jnp.exp(a) - jnp.exp(b) + lam_init


def _diff_prompt_kernel(qt_ref, k_ref, vt_ref, lam_ref, subln_ref, o_ref, m_sc, l_sc, acc_sc, *, lam_init):
    qi = pl.program_id(1)
    blk = MOBA_BLOCK
    slopes = [s for s in _slopes(N_HEADS_B) for _ in range(2)]
    qms = _masked_queries(qt_ref[...])
    key_i = lax.broadcasted_iota(jnp.int32, (blk, blk), 0)
    qry_i = lax.broadcasted_iota(jnp.int32, (blk, blk), 1)
    diag_dist = (qry_i - key_i).astype(F32)

    m_sc[...] = jnp.full(m_sc.shape, NEG, F32)
    l_sc[...] = jnp.zeros(l_sc.shape, F32)
    acc_sc[...] = jnp.zeros(acc_sc.shape, F32)
    k_lanes = lambda g: slice(LANES * (g // 2), LANES * (g // 2 + 1))
    v_rows = lambda g: slice(V_DIM_B * (g // 2), V_DIM_B * (g // 2 + 1))

    causal = key_i <= qry_i
    _attend_block(k_ref[qi], vt_ref[qi], qms, diag_dist, lambda g: causal,
                  m_sc, l_sc, acc_sc, slopes, k_lanes, v_rows)

    def past(n, carry):
        dist = diag_dist + ((qi - n) * blk).astype(F32)
        _attend_block(k_ref[n], vt_ref[n], qms, dist, lambda g: None,
                      m_sc, l_sc, acc_sc, slopes, k_lanes, v_rows)
        return carry

    lax.fori_loop(0, qi, past, 0)

    lam = _lambda(lam_ref, lam_init)
    parts = []
    for h in range(N_HEADS_B):
        o0 = acc_sc[V_DIM_B * (2 * h):V_DIM_B * (2 * h + 1), :] * (1.0 / l_sc[2 * h:2 * h + 1, :])
        o1 = acc_sc[V_DIM_B * (2 * h + 1):V_DIM_B * (2 * h + 2), :] * (1.0 / l_sc[2 * h + 1:2 * h + 2, :])
        o = o0 - lam * o1
        ms = jnp.mean(o * o, axis=0, keepdims=True)
        parts.append(o * lax.rsqrt(ms + RMS_EPS))
    o_t = jnp.concatenate(parts, axis=0).T
    o_ref[...] = (o_t * subln_ref[...] * (1.0 - lam_init)).astype(BF16)


def _prompt_attention(qt_a, kbf_a, vt_a, km_a, qt_b, kbf_b, vt_b, lam_vecs, subln_row, lam_init):
    b, nb = qt_a.shape[:2]
    blk = MOBA_BLOCK
    s = nb * blk
    q_spec = pl.BlockSpec((None, None, WIDTH, blk), lambda i, t: (i, t, 0, 0))
    k_spec = pl.BlockSpec((None, nb, blk, WIDTH), lambda i, t: (i, 0, 0, 0))
    vt_spec = pl.BlockSpec((None, nb, WIDTH, blk), lambda i, t: (i, 0, 0, 0))
    o_spec = pl.BlockSpec((None, blk, WIDTH), lambda i, t: (i, t, 0))
    o_shape = jax.ShapeDtypeStruct((b, s, WIDTH), BF16)
    stat = pltpu.VMEM((N_GROUPS, blk), F32)
    o_a = pl.pallas_call(
        _moba_prompt_kernel,
        grid=(b, nb),
        in_specs=[q_spec, k_spec, vt_spec, pl.BlockSpec((None, nb, WIDTH), lambda i, t: (i, 0, 0))],
        out_specs=o_spec,
        out_shape=o_shape,
        scratch_shapes=[pltpu.VMEM((N_GROUPS, nb, blk), F32), stat, stat,
                        pltpu.VMEM((WIDTH, blk), F32)],
        compiler_params=_params("arbitrary", "arbitrary"),
        name="moba_prompt",
    )(qt_a, kbf_a, vt_a, km_a)
    o_b = pl.pallas_call(
        functools.partial(_diff_prompt_kernel, lam_init=lam_init),
        grid=(b, nb),
        in_specs=[q_spec, k_spec, vt_spec, _const_spec((4, HEAD_DIM_B)), _const_spec((1, WIDTH))],
        out_specs=o_spec,
        out_shape=o_shape,
        scratch_shapes=[stat, stat, pltpu.VMEM((N_GROUPS * V_DIM_B, blk), F32)],
        compiler_params=_params("arbitrary", "arbitrary"),
        name="diff_prompt",
    )(qt_b, kbf_b, vt_b, lam_vecs, subln_row)
    return o_a, o_b


def _mix_ffn_kernel(x_ref, oa_ref, ob_ref, ga_ref, gb_ref, mod_ref, wba_ref, wbd_ref, wout_ref,
                    nw_ref, wg_ref, wu_ref, wd_ref, o_ref, *, tf):
    y_a = _dot(oa_ref[...], wba_ref[...])
    y_b = _dot(ob_ref[...], wbd_ref[...])
    mixed = _dot((ga_ref[...] * y_a + gb_ref[...] * y_b).astype(BF16), wout_ref[...])
    x = x_ref[...] + mod_ref[5] * mixed
    o_ref[...] = _ffn_update(x, mod_ref[6], mod_ref[7], mod_ref[8], nw_ref[...],
                             wg_ref, wu_ref, wd_ref, tf)


def _mix_ffn(x, o_a, o_b, g_a, g_b, mod, w_ba, w_bd, w_out, nw, wg, wu, wd, tm):
    b, s, d = x.shape
    r = mod.shape[2]
    d_ff = wg.shape[1]
    tf = _pick_ff_tile(d_ff)
    row_spec = lambda w: pl.BlockSpec((None, tm, w), lambda i, t: (i, t, 0))
    return pl.pallas_call(
        functools.partial(_mix_ffn_kernel, tf=tf),
        grid=(b, s // tm),
        in_specs=[row_spec(d), row_spec(WIDTH), row_spec(WIDTH), row_spec(d), row_spec(d),
                  pl.BlockSpec((None, N_MOD, r, d), lambda i, t: (i, 0, 0, 0)),
                  _const_spec((WIDTH, d)), _const_spec((WIDTH, d)), _const_spec((d, d)),
                  _const_spec((1, d)),
                  _const_spec((d, d_ff)), _const_spec((d, d_ff)), _const_spec((d_ff, d))],
        out_specs=row_spec(d),
        out_shape=jax.ShapeDtypeStruct((b, s, d), F32),
        compiler_params=_params("arbitrary", "arbitrary"),
        name="mix_ffn",
    )(x, o_a, o_b, g_a, g_b, mod, w_ba, w_bd, w_out, nw, wg, wu, wd)


def _moba_decode_kernel(pt_ref, q_ref, slope_ref, ones_ref, *rest, past_len):
    del pt_ref
    npg = PAGES_PER_STEP
    ppb = MOBA_BLOCK // PAGE_SIZE
    k_refs, v_refs = rest[:npg], rest[npg:2 * npg]
    m_ref, l_ref, o_ref, km_ref = rest[2 * npg:]
    j = pl.program_id(1)
    q = q_ref[...]
    slope = slope_ref[...]
    ones = ones_ref[...]
    t_iota = lax.broadcasted_iota(jnp.int32, (PAGE_SIZE, N_HEADS_A, LANES), 0).astype(F32)
    for bi in range(npg // ppb):
        ksum = jnp.zeros((N_HEADS_A, HEAD_DIM_A), F32)
        scores = []
        for pi in range(ppb):
            idx = bi * ppb + pi
            k_page = k_refs[idx][...]
            ksum = ksum + jnp.sum(k_page, axis=0)
            prod = (k_page * q[None]).reshape(PAGE_SIZE * N_HEADS_A, HEAD_DIM_A).astype(BF16)
            s = _dot(prod, ones).reshape(PAGE_SIZE, N_HEADS_A, LANES)
            dist0 = (past_len - (j * npg + idx) * PAGE_SIZE).astype(F32)
            scores.append(s - slope[None] * (dist0 - t_iota))
        m = functools.reduce(jnp.maximum, [jnp.max(s, axis=0) for s in scores])
        l = jnp.zeros((N_HEADS_A, LANES), F32)
        o = jnp.zeros((N_HEADS_A, HEAD_DIM_A), F32)
        for pi in range(ppb):
            p = jnp.exp(scores[pi] - m[None])
            l = l + jnp.sum(p, axis=0)
            o = o + jnp.sum(p[:, :, :HEAD_DIM_A] * v_refs[bi * ppb + pi][...], axis=0)
        m_ref[bi] = m
        l_ref[bi] = l
        o_ref[bi] = o
        km_ref[bi] = ksum * (1.0 / MOBA_BLOCK)


def _diff_decode_kernel(pt_ref, q_ref, kn_ref, vn_ref, slope_ref, b01_ref, *rest, past_len):
    del pt_ref
    npg = PAGES_PER_STEP
    k_refs, v_refs = rest[:npg], rest[npg:2 * npg]
    o0_ref, o1_ref, m_sc, l_sc, a0_sc, a1_sc = rest[2 * npg:]
    j = pl.program_id(1)
    q = q_ref[...]
    b01 = b01_ref[...]
    t_iota = lax.broadcasted_iota(jnp.int32, (PAGE_SIZE, 2 * LANES), 0).astype(F32)
    fold = lambda x: jnp.sum(x.reshape(PAGE_SIZE // SUBLANES, SUBLANES, x.shape[-1]), axis=0)

    @pl.when(j == 0)
    def _():
        prod = (q * kn_ref[...]).astype(BF16)
        prod8 = jnp.concatenate([prod, jnp.zeros_like(prod)], axis=0)
        m_sc[...] = _dot(prod8, b01)[0:N_HEADS_B, :]
        l_sc[...] = jnp.ones(l_sc.shape, F32)
        a0_sc[...] = vn_ref[...]
        a1_sc[...] = vn_ref[...]

    for h in range(N_HEADS_B):
        q_h = q[h:h + 1, :]
        slope_h = slope_ref[h:h + 1, :]
        scores = []
        for idx in range(npg):
            prod = (k_refs[idx][:, h, :] * q_h).astype(BF16)
            dist0 = (past_len - (j * npg + idx) * PAGE_SIZE).astype(F32)
            scores.append(_dot(prod, b01) - slope_h * (dist0 - t_iota))
        m_old = m_sc[h:h + 1, :]
        m_new = functools.reduce(jnp.maximum, [jnp.max(s, axis=0, keepdims=True) for s in scores] + [m_old])
        alpha = jnp.exp(m_old - m_new)
        l8 = jnp.zeros((SUBLANES, 2 * LANES), F32)
        a0 = jnp.zeros((SUBLANES, LANES), F32)
        a1 = jnp.zeros((SUBLANES, LANES), F32)
        for idx in range(npg):
            p = jnp.exp(scores[idx] - m_new)
            v_h = v_refs[idx][:, h, :]
            l8 = l8 + fold(p)
            a0 = a0 + fold(p[:, :LANES] * v_h)
            a1 = a1 + fold(p[:, LANES:] * v_h)
        m_sc[h:h + 1, :] = m_new
        l_sc[h:h + 1, :] = alpha * l_sc[h:h + 1, :] + jnp.sum(l8, axis=0, keepdims=True)
        a0_sc[h:h + 1, :] = alpha[:, :LANES] * a0_sc[h:h + 1, :] + jnp.sum(a0, axis=0, keepdims=True)
        a1_sc[h:h + 1, :] = alpha[:, LANES:] * a1_sc[h:h + 1, :] + jnp.sum(a1, axis=0, keepdims=True)

    @pl.when(j == pl.num_programs(1) - 1)
    def _():
        inv = 1.0 / l_sc[...]
        o0_ref[...] = a0_sc[...] * inv[:, :LANES]
        o1_ref[...] = a1_sc[...] * inv[:, LANES:]


def _page_specs(layer, heads, dim):
    def spec(i):
        return pl.BlockSpec((None, None, PAGE_SIZE, heads, dim),
                            lambda s, j, pt: (layer, pt[s, j * PAGES_PER_STEP + i], 0, 0, 0))
    return [spec(i) for i in range(PAGES_PER_STEP)]


def _moba_decode(q, k_pool, v_pool, page_table, layer):
    db, n_pages = page_table.shape
    nblk = n_pages * PAGE_SIZE // MOBA_BLOCK
    bps = PAGES_PER_STEP * PAGE_SIZE // MOBA_BLOCK
    slope = jnp.asarray(np.repeat(np.asarray(_slopes(N_HEADS_A), np.float32)[:, None], LANES, axis=1))
    ones = jnp.ones((HEAD_DIM_A, LANES), BF16)
    blk = lambda w: pl.BlockSpec((None, bps, N_HEADS_A, w), lambda s, j, pt: (s, j, 0, 0))
    cst = lambda shp: pl.BlockSpec(shp, lambda s, j, pt: (0, 0))
    grid_spec = pltpu.PrefetchScalarGridSpec(
        num_scalar_prefetch=1,
        grid=(db, n_pages // PAGES_PER_STEP),
        in_specs=[pl.BlockSpec((None, N_HEADS_A, HEAD_DIM_A), lambda s, j, pt: (s, 0, 0)),
                  cst((N_HEADS_A, LANES)), cst((HEAD_DIM_A, LANES))]
        + _page_specs(layer, N_HEADS_A, HEAD_DIM_A) * 2,
        out_specs=[blk(LANES), blk(LANES), blk(HEAD_DIM_A), blk(HEAD_DIM_A)],
    )
    stat = jax.ShapeDtypeStruct((db, nblk, N_HEADS_A, LANES), F32)
    wide = jax.ShapeDtypeStruct((db, nblk, N_HEADS_A, HEAD_DIM_A), F32)
    return pl.pallas_call(
        functools.partial(_moba_decode_kernel, past_len=n_pages * PAGE_SIZE),
        grid_spec=grid_spec,
        out_shape=[stat, stat, wide, wide],
        compiler_params=_params("arbitrary", "arbitrary"),
        name="moba_decode",
    )(page_table, q, slope, ones, *([k_pool] * PAGES_PER_STEP), *([v_pool] * PAGES_PER_STEP))


def _diff_decode(q, k_new, v_new, k_pool, v_pool, page_table, layer):
    db, n_pages = page_table.shape
    slope_np = np.repeat(np.asarray(_slopes(N_HEADS_B), np.float32)[:, None], 2 * LANES, axis=1)
    half = np.arange(LANES) // HEAD_DIM_B
    b01_np = (half[:, None] == (np.arange(2 * LANES) // LANES)[None, :]).astype(np.float32)
    seq = pl.BlockSpec((None, N_HEADS_B, V_DIM_B), lambda s, j, pt: (s, 0, 0))
    cst = lambda shp: pl.BlockSpec(shp, lambda s, j, pt: (0, 0))
    grid_spec = pltpu.PrefetchScalarGridSpec(
        num_scalar_prefetch=1,
        grid=(db, n_pages // PAGES_PER_STEP),
        in_specs=[seq, seq, seq, cst((N_HEADS_B, 2 * LANES)), cst((LANES, 2 * LANES))]
        + _page_specs(layer, N_HEADS_B, V_DIM_B) * 2,
        out_specs=[seq, seq],
        scratch_shapes=[pltpu.VMEM((N_HEADS_B, 2 * LANES), F32), pltpu.VMEM((N_HEADS_B, 2 * LANES), F32),
                        pltpu.VMEM((N_HEADS_B, LANES), F32), pltpu.VMEM((N_HEADS_B, LANES), F32)],
    )
    wide = jax.ShapeDtypeStruct((db, N_HEADS_B, V_DIM_B), F32)
    return pl.pallas_call(
        functools.partial(_diff_decode_kernel, past_len=n_pages * PAGE_SIZE),
        grid_spec=grid_spec,
        out_shape=[wide, wide],
        compiler_params=_params("arbitrary", "arbitrary"),
        name="diff_decode",
    )(page_table, q, k_new, v_new, jnp.asarray(slope_np), jnp.asarray(b01_np, BF16),
      *([k_pool] * PAGES_PER_STEP), *([v_pool] * PAGES_PER_STEP))


def _bf16_round(x):
    return x.astype(BF16).astype(F32)


def _sample_combine_kernel(q_ref, kn_ref, vn_ref, m_ref, l_ref, o_ref, km_ref, o0_ref, o1_ref,
                           lam_ref, subln_ref, oa_ref, ob_ref, *, lam_init):
    nblk = m_ref.shape[1]
    q = q_ref[...]

    gate = jnp.sum(_bf16_round(km_ref[...]) * _bf16_round(q)[:, None], axis=-1, keepdims=True)
    blk_i = lax.broadcasted_iota(jnp.int32, gate.shape, 1)
    rank = jnp.zeros(gate.shape, jnp.int32)
    for m in range(nblk):
        gm = gate[:, m:m + 1]
        beats = (gm > gate) | ((gm == gate) & (m < blk_i))
        rank = rank + beats.astype(jnp.int32)
    sel = rank < MOBA_TOPK
    s_self = jnp.sum(_bf16_round(q) * _bf16_round(kn_ref[...]), axis=-1, keepdims=True)
    m_blk = jnp.where(sel, m_ref[...], NEG)
    m_tot = jnp.maximum(jnp.max(m_blk, axis=1), s_self)
    w_blk = jnp.where(sel, jnp.exp(m_blk - m_tot[:, None]), 0.0)
    w_self = jnp.exp(s_self - m_tot)
    l_tot = jnp.sum(w_blk * l_ref[...], axis=1) + w_self
    acc = jnp.sum(w_blk[..., :HEAD_DIM_A] * o_ref[...], axis=1) + w_self[..., :HEAD_DIM_A] * vn_ref[...]
    oa_ref[...] = acc * (1.0 / l_tot[..., :HEAD_DIM_A])

    lam = _lambda(lam_ref, lam_init)
    o = o0_ref[...] - lam * o1_ref[...]
    ms = jnp.mean(o * o, axis=-1, keepdims=True)
    ob_ref[...] = o * lax.rsqrt(ms + RMS_EPS) * subln_ref[...] * (1.0 - lam_init)


def _sample_combine(q_a, k_new, v_new, m, l, o, km, o0, o1, lam_vecs, subln, lam_init):
    db, nblk = m.shape[:2]
    ns = SEQS_PER_COMBINE_STEP
    seq3 = lambda h, w: pl.BlockSpec((ns, h, w), lambda i: (i, 0, 0))
    seq4 = lambda w: pl.BlockSpec((ns, nblk, N_HEADS_A, w), lambda i: (i, 0, 0, 0))
    a3, b3 = seq3(N_HEADS_A, HEAD_DIM_A), seq3(N_HEADS_B, V_DIM_B)
    return pl.pallas_call(
        functools.partial(_sample_combine_kernel, lam_init=lam_init),
        grid=(db // ns,),
        in_specs=[a3, a3, a3, seq4(LANES), seq4(LANES), seq4(HEAD_DIM_A), seq4(HEAD_DIM_A), b3, b3,
                  _const_spec((4, HEAD_DIM_B)), _const_spec((1, V_DIM_B))],
        out_specs=[a3, b3],
        out_shape=[jax.ShapeDtypeStruct((db, N_HEADS_A, HEAD_DIM_A), F32),
                   jax.ShapeDtypeStruct((db, N_HEADS_B, V_DIM_B), F32)],
        compiler_params=_params("arbitrary"),
        name="sample_combine",
    )(q_a, k_new, v_new, m, l, o, km, o0, o1, lam_vecs, subln)


def _block_diag_ones():
    i = np.arange(MXU_DIM) // 64
    return jnp.asarray((i[:, None] == i[None, :]).astype(np.float32), BF16)


def kernel(x_prompt, x_sample, cache_k_moba, cache_v_moba, cache_k_diff, cache_v_diff, page_table, c_prompt, c_sample, w_ada, b_ada, norm_ffn1, ffn1_w_gate, ffn1_w_up, ffn1_w_down, norm_mix, w_in, qn_moba, kn_moba, qn_diff, kn_diff, lambda_q1, lambda_k1, lambda_q2, lambda_k2, subln_diff, w_branch_moba, w_branch_diff, w_out, norm_ffn2, ffn2_w_gate, ffn2_w_up, ffn2_w_down):
    depth = w_ada.shape[0]
    b, s, d = x_prompt.shape
    db, t_new, _ = x_sample.shape
    assert t_new == 1 and s % TOKEN_TILE == 0 and db % SEQS_PER_COMBINE_STEP == 0
    n_pages = page_table.shape[1]
    assert n_pages % PAGES_PER_STEP == 0
    bd = _block_diag_ones()
    tile8 = lambda v: jnp.tile(v, WIDTH // v.shape[0]).reshape(1, WIDTH)

    y_p, y_s = x_prompt, x_sample.reshape(1, db, d)
    rows_p, rows_s = [], []
    for l in range(depth):
        lam_init = _lambda_init(l)
        bf = lambda w: w[l].astype(BF16)
        row = lambda v: v[l].reshape(1, -1)
        mod = _ada(jnp.concatenate([c_prompt, c_sample], axis=0), bf(w_ada), b_ada[l])
        mod_p = mod[:b].reshape(b, N_MOD, 1, d)
        mod_s = mod[b:].reshape(db, N_MOD, d).transpose(1, 0, 2).reshape(1, N_MOD, db, d)
        ffn1 = (row(norm_ffn1), bf(ffn1_w_gate), bf(ffn1_w_up), bf(ffn1_w_down))
        ffn2 = (row(norm_ffn2), bf(ffn2_w_gate), bf(ffn2_w_up), bf(ffn2_w_down))
        w_in_bf = bf(w_in)
        gains = jnp.concatenate([tile8(qn_moba[l]), tile8(kn_moba[l]), tile8(qn_diff[l]), tile8(kn_diff[l])], 0)
        lam_vecs = jnp.stack([lambda_q1[l], lambda_k1[l], lambda_q2[l], lambda_k2[l]])
        mix_w = (bf(w_branch_moba), bf(w_branch_diff), bf(w_out))

        x1 = _ffn(y_p, mod_p, *ffn1, k0=0, tm=TOKEN_TILE)
        (k_a, v_a, k_b, v_b, kbf_a, kbf_b, qt_a, qt_b, vt_a, vt_b, km_a, g_a, g_b) = _proj(
            x1, mod_p, row(norm_mix), w_in_bf, gains, bd, TOKEN_TILE, True)
        o_a, o_b = _prompt_attention(qt_a, kbf_a, vt_a, km_a.reshape(b, -1, WIDTH), qt_b, kbf_b, vt_b,
                                     lam_vecs, tile8(subln_diff[l]), lam_init)
        y_p = _mix_ffn(x1, o_a, o_b, g_a, g_b, mod_p, *mix_w, *ffn2, tm=TOKEN_TILE)
        rows_p.append((k_a, v_a, k_b, v_b))

        x1s = _ffn(y_s, mod_s, *ffn1, k0=0, tm=db)
        (q_as, k_as, v_as, q_bs, k_bs, v_bs, g_as, g_bs) = _proj(
            x1s, mod_s, row(norm_mix), w_in_bf, gains, bd, db, False)
        heads_a = lambda a: a.reshape(db, N_HEADS_A, HEAD_DIM_A)
        heads_b = lambda a: a.reshape(db, N_HEADS_B, V_DIM_B)
        m_blk, l_blk, o_blk, km_blk = _moba_decode(heads_a(q_as), cache_k_moba, cache_v_moba, page_table, l)
        o0, o1 = _diff_decode(heads_b(q_bs), heads_b(k_bs), heads_b(v_bs), cache_k_diff, cache_v_diff,
                              page_table, l)
        o_as, o_bs = _sample_combine(heads_a(q_as), heads_a(k_as), heads_a(v_as), m_blk, l_blk, o_blk, km_blk,
                                     o0, o1, lam_vecs, subln_diff[l].reshape(1, V_DIM_B), lam_init)
        as_rows = lambda a: a.reshape(1, db, WIDTH).astype(BF16)
        y_s = _mix_ffn(x1s, as_rows(o_as), as_rows(o_bs), g_as, g_bs, mod_s, *mix_w, *ffn2, tm=db)
        rows_s.append((k_as, v_as, k_bs, v_bs))

    def stack(rows, i, lead, heads, hd):
        return jnp.stack([r[i].reshape(lead + (heads, hd)) for r in rows])

    lp, ls = (b, s), (db, 1)
    return (y_p, y_s.reshape(db, 1, d),
            stack(rows_p, 0, lp, N_HEADS_A, HEAD_DIM_A), stack(rows_p, 1, lp, N_HEADS_A, HEAD_DIM_A),
            stack(rows_p, 2, lp, N_HEADS_B, V_DIM_B), stack(rows_p, 3, lp, N_HEADS_B, V_DIM_B),
            stack(rows_s, 0, ls, N_HEADS_A, HEAD_DIM_A), stack(rows_s, 1, ls, N_HEADS_A, HEAD_DIM_A),
            stack(rows_s, 2, ls, N_HEADS_B, V_DIM_B), stack(rows_s, 3, ls, N_HEADS_B, V_DIM_B))
```

```python
import functools
import math

import jax
import jax.numpy as jnp
import numpy as np
from jax import lax
from jax.experimental import pallas as pl
from jax.experimental.pallas import tpu as pltpu

F32 = jnp.float32
BF16 = jnp.bfloat16

N_HEADS_A = 8
HEAD_DIM_A = 64
MOBA_BLOCK = 256
MOBA_TOPK = 3
N_HEADS_B = 4
HEAD_DIM_B = 64
V_DIM_B = 2 * HEAD_DIM_B
WIDTH = 512
N_GROUPS = WIDTH // 64
PAGE_SIZE = 128
PAGES_PER_BLOCK = MOBA_BLOCK // PAGE_SIZE
N_MOD = 9
RMS_EPS = 1e-6
QK_SCALE = 0.125
NEG = -1e30

LANES = 128
SUBLANES = 8
MXU_DIM = 256
VMEM_LIMIT_BYTES = 56 * 1024 * 1024

TOKEN_TILE = 512
PAGES_PER_STEP = 16


def _slopes(n_heads):
    return [2.0 ** (-8.0 * (i + 1) / n_heads) for i in range(n_heads)]


def _lambda_init(layer):
    return 0.8 - 0.6 * math.exp(-0.3 * layer)


def _dot(a, b):
    return jnp.dot(a, b, preferred_element_type=F32)


def _bf16_round(x):
    return x.astype(BF16).astype(F32)


def _rms(x, w):
    ms = jnp.mean(x * x, axis=-1, keepdims=True)
    return x * lax.rsqrt(ms + RMS_EPS) * w


def _pick_ff_tile(d_ff):
    best = LANES
    for t in range(LANES, min(d_ff, 1408) + 1, LANES):
        if d_ff % t == 0:
            best = t
    return best


def _params(*sem):
    return pltpu.CompilerParams(dimension_semantics=sem, vmem_limit_bytes=VMEM_LIMIT_BYTES)


def _const_spec(shape):
    nd = len(shape)
    return pl.BlockSpec(shape, lambda *_: (0,) * nd)


def _ada_kernel(c_ref, w_ref, b_ref, o_ref):
    c = c_ref[...]
    s = c * jax.nn.sigmoid(c)
    o_ref[...] = _dot(s.astype(BF16), w_ref[...]) + b_ref[...]


def _ada(c, w_bf, b):
    m, d = c.shape
    n = w_bf.shape[1]
    tn = 1024 if n % 1024 == 0 else n
    return pl.pallas_call(
        _ada_kernel,
        grid=(n // tn,),
        in_specs=[pl.BlockSpec((m, d), lambda j: (0, 0)),
                  pl.BlockSpec((d, tn), lambda j: (0, j)),
                  pl.BlockSpec((1, tn), lambda j: (0, j))],
        out_specs=pl.BlockSpec((m, tn), lambda j: (0, j)),
        out_shape=jax.ShapeDtypeStruct((m, n), F32),
        compiler_params=_params("arbitrary"),
        name="ada_mod",
    )(c, w_bf, b.reshape(1, n))


def _ffn_update(x, shift, scale, gate, nw, wg_ref, wu_ref, wd_ref, tf):
    h = (_rms(x, nw) * (1.0 + scale) + shift).astype(BF16)
    d_ff = wg_ref.shape[1]
    acc = jnp.zeros(x.shape, F32)
    for j in range(d_ff // tf):
        g = _dot(h, wg_ref[:, j * tf:(j + 1) * tf])
        u = _dot(h, wu_ref[:, j * tf:(j + 1) * tf])
        a = (g * jax.nn.sigmoid(g) * u).astype(BF16)
        acc = acc + _dot(a, wd_ref[j * tf:(j + 1) * tf, :])
    return x + 0.5 * gate * acc


def _ffn_kernel(x_ref, mod_ref, nw_ref, wg_ref, wu_ref, wd_ref, o_ref, *, k0, tf):
    o_ref[...] = _ffn_update(x_ref[...], mod_ref[k0], mod_ref[k0 + 1], mod_ref[k0 + 2],
                             nw_ref[...], wg_ref, wu_ref, wd_ref, tf)


def _ffn(x, mod, nw, wg, wu, wd, k0, tm):
    b, s, d = x.shape
    r = mod.shape[2]
    d_ff = wg.shape[1]
    tf = _pick_ff_tile(d_ff)
    return pl.pallas_call(
        functools.partial(_ffn_kernel, k0=k0, tf=tf),
        grid=(b, s // tm),
        in_specs=[pl.BlockSpec((None, tm, d), lambda i, t: (i, t, 0)),
                  pl.BlockSpec((None, N_MOD, r, d), lambda i, t: (i, 0, 0, 0)),
                  _const_spec((1, d)),
                  _const_spec((d, d_ff)), _const_spec((d, d_ff)), _const_spec((d_ff, d))],
        out_specs=pl.BlockSpec((None, tm, d), lambda i, t: (i, t, 0)),
        out_shape=jax.ShapeDtypeStruct((b, s, d), F32),
        compiler_params=_params("arbitrary", "arbitrary"),
        name="ffn",
    )(x, mod, nw, wg, wu, wd)


def _head_norm(seg, gain_row, bd):
    sq = (seg * seg).astype(BF16)
    parts = [_dot(sq[:, c * MXU_DIM:(c + 1) * MXU_DIM], bd) for c in range(WIDTH // MXU_DIM)]
    ms = jnp.concatenate(parts, axis=1) * (1.0 / 64.0)
    return seg * lax.rsqrt(ms + RMS_EPS) * gain_row


def _proj_kernel(x_ref, mod_ref, nw_ref, win_ref, gains_ref, bd_ref, *outs, transposed):
    x = x_ref[...]
    d = x.shape[1]
    h = (_rms(x, nw_ref[...]) * (1.0 + mod_ref[4]) + mod_ref[3]).astype(BF16)
    bd = bd_ref[...]

    def seg(j):
        return _dot(h, win_ref[:, j * WIDTH:(j + 1) * WIDTH])

    q_a = _head_norm(seg(0), gains_ref[0:1, :], bd) * QK_SCALE
    k_a = _head_norm(seg(1), gains_ref[1:2, :], bd)
    v_a = seg(2)
    q_b = _head_norm(seg(3), gains_ref[2:3, :], bd) * QK_SCALE
    k_b = _head_norm(seg(4), gains_ref[3:4, :], bd)
    v_b = seg(5)
    g0 = 6 * WIDTH
    g_a = jax.nn.sigmoid(_dot(h, win_ref[:, g0:g0 + d]))
    g_b = jax.nn.sigmoid(_dot(h, win_ref[:, g0 + d:g0 + 2 * d]))

    if not transposed:
        (qa_ref, ka_ref, va_ref, qb_ref, kb_ref, vb_ref, ga_ref, gb_ref) = outs
        qa_ref[...] = q_a
        ka_ref[...] = k_a
        va_ref[...] = v_a
        qb_ref[...] = q_b
    else:
        (kta_ref, vta32_ref, kb_ref, vb_ref, kabf_ref, kbbf_ref, qta_ref, qtb_ref,
         vta_ref, vtb_ref, km_ref, ga_ref, gb_ref) = outs
        kta_ref[...] = k_a.T
        vt_a = v_a.T
        vta32_ref[...] = vt_a
        qt_a, qt_b, vt_b = q_a.T, q_b.T, v_b.T
        for r in range(x.shape[0] // MOBA_BLOCK):
            rows = slice(r * MOBA_BLOCK, (r + 1) * MOBA_BLOCK)
            kabf_ref[r] = k_a[rows].astype(BF16)
            kbbf_ref[r] = k_b[rows].astype(BF16)
            qta_ref[r] = qt_a[:, rows].astype(BF16)
            qtb_ref[r] = qt_b[:, rows].astype(BF16)
            vta_ref[r] = vt_a[:, rows].astype(BF16)
            vtb_ref[r] = vt_b[:, rows].astype(BF16)
            km_ref[r] = jnp.sum(k_a[rows], axis=0, keepdims=True) * (1.0 / MOBA_BLOCK)
    kb_ref[...] = k_b
    vb_ref[...] = v_b
    ga_ref[...] = g_a
    gb_ref[...] = g_b


def _proj(x, mod, nw, w_in, gains, bd, tm, transposed):
    b, s, d = x.shape
    r = mod.shape[2]
    d_in = w_in.shape[1]
    row_spec = lambda w: pl.BlockSpec((None, tm, w), lambda i, t: (i, t, 0))
    rows = lambda w, dt=F32: jax.ShapeDtypeStruct((b, s, w), dt)
    if transposed:
        nb, bpt = s // MOBA_BLOCK, tm // MOBA_BLOCK
        blk = lambda shp: pl.BlockSpec((None, bpt) + shp, lambda i, t: (i, t, 0, 0))
        col_spec = pl.BlockSpec((None, WIDTH, tm), lambda i, t: (i, 0, t))
        out_specs = [col_spec] * 2 + [row_spec(WIDTH)] * 2 + [blk((MOBA_BLOCK, WIDTH))] * 2 \
            + [blk((WIDTH, MOBA_BLOCK))] * 4 + [blk((1, WIDTH))] + [row_spec(d)] * 2
        out_shape = [jax.ShapeDtypeStruct((b, WIDTH, s), F32)] * 2 + [rows(WIDTH)] * 2 \
            + [jax.ShapeDtypeStruct((b, nb, MOBA_BLOCK, WIDTH), BF16)] * 2 \
            + [jax.ShapeDtypeStruct((b, nb, WIDTH, MOBA_BLOCK), BF16)] * 4 \
            + [jax.ShapeDtypeStruct((b, nb, 1, WIDTH), F32)] + [rows(d)] * 2
    else:
        out_specs = [row_spec(WIDTH)] * 6 + [row_spec(d)] * 2
        out_shape = [rows(WIDTH)] * 6 + [rows(d)] * 2
    return pl.pallas_call(
        functools.partial(_proj_kernel, transposed=transposed),
        grid=(b, s // tm),
        in_specs=[pl.BlockSpec((None, tm, d), lambda i, t: (i, t, 0)),
                  pl.BlockSpec((None, N_MOD, r, d), lambda i, t: (i, 0, 0, 0)),
                  _const_spec((1, d)), _const_spec((d, d_in)),
                  _const_spec((4, WIDTH)), _const_spec((MXU_DIM, MXU_DIM))],
        out_specs=out_specs,
        out_shape=out_shape,
        compiler_params=_params("arbitrary", "arbitrary"),
        name="mixer_proj",
    )(x, mod, nw, w_in, gains, bd)


def _attend_block(kb, vtb, qms, dist, mask_of, m_sc, l_sc, acc_sc, slopes, k_lanes, v_rows):
    for g, qm in enumerate(qms):
        st = _dot(kb[:, k_lanes(g)], qm) - slopes[g] * dist
        mask = mask_of(g)
        if mask is not None:
            st = jnp.where(mask, st, NEG)
        m_old = m_sc[g:g + 1, :]
        m_new = jnp.maximum(m_old, jnp.max(st, axis=0, keepdims=True))
        alpha = jnp.exp(m_old - m_new)
        pt = jnp.exp(st - m_new)
        l_sc[g:g + 1, :] = alpha * l_sc[g:g + 1, :] + jnp.sum(pt, axis=0, keepdims=True)
        rows = v_rows(g)
        nr = rows.stop - rows.start
        acc_rows = slice(g * nr, (g + 1) * nr)
        acc_sc[acc_rows, :] = alpha * acc_sc[acc_rows, :] + _dot(vtb[rows, :], pt.astype(BF16))
        m_sc[g:g + 1, :] = m_new


def _masked_queries(qt):
    row = lax.broadcasted_iota(jnp.int32, (LANES, MOBA_BLOCK), 0)
    out = []
    for g in range(N_GROUPS):
        p, half = divmod(g, 2)
        pair = qt[LANES * p:LANES * (p + 1), :]
        keep = (row >= 64 * half) & (row < 64 * (half + 1))
        out.append(jnp.where(keep, pair, jnp.zeros_like(pair)))
    return out


def _moba_prompt_kernel(qt_ref, k_ref, vt_ref, km_ref, o_ref, sel_sc, m_sc, l_sc, acc_sc):
    qi = pl.program_id(1)
    nb = k_ref.shape[0]
    blk = MOBA_BLOCK
    slopes = _slopes(N_HEADS_A)
    qms = _masked_queries(qt_ref[...])
    key_i = lax.broadcasted_iota(jnp.int32, (blk, blk), 0)
    qry_i = lax.broadcasted_iota(jnp.int32, (blk, blk), 1)
    diag_dist = (qry_i - key_i).astype(F32)

    km = km_ref[...].astype(BF16)
    blk_i = lax.broadcasted_iota(jnp.int32, (nb, blk), 0)
    valid = blk_i < qi
    for g in range(N_GROUPS):
        lanes = slice(LANES * (g // 2), LANES * (g // 2 + 1))
        gate = _dot(km[:, lanes], qms[g])
        gate = jnp.where(valid, gate, NEG)
        rank = jnp.zeros((nb, blk), jnp.int32)
        for m in range(nb):
            gm = gate[m:m + 1, :]
            beats = (gm > gate) | ((gm == gate) & (m < blk_i))
            rank = rank + beats.astype(jnp.int32)
        sel_sc[g] = (valid & (rank < MOBA_TOPK)).astype(F32)

    m_sc[...] = jnp.full(m_sc.shape, NEG, F32)
    l_sc[...] = jnp.zeros(l_sc.shape, F32)
    acc_sc[...] = jnp.zeros(acc_sc.shape, F32)
    k_lanes = lambda g: slice(LANES * (g // 2), LANES * (g // 2 + 1))
    v_rows = lambda g: slice(64 * g, 64 * (g + 1))

    causal = key_i <= qry_i
    _attend_block(k_ref[qi], vt_ref[qi], qms, diag_dist, lambda g: causal,
                  m_sc, l_sc, acc_sc, slopes, k_lanes, v_rows)

    def past(n, carry):
        dist = diag_dist + ((qi - n) * blk).astype(F32)
        _attend_block(k_ref[n], vt_ref[n], qms, dist,
                      lambda g: sel_sc[g, pl.ds(n, 1), :] > 0.5,
                      m_sc, l_sc, acc_sc, slopes, k_lanes, v_rows)
        return carry

    lax.fori_loop(0, qi, past, 0)

    parts = [acc_sc[64 * g:64 * (g + 1), :] * (1.0 / l_sc[g:g + 1, :]) for g in range(N_GROUPS)]
    o_ref[...] = jnp.concatenate(parts, axis=0).T.astype(BF16)


def _lambda(lam_ref, lam_init):
    a = jnp.sum(lam_ref[0:1, :] * lam_ref[1:2, :], axis=-1, keepdims=True)
    b = jnp.sum(lam_ref[2:3, :] * lam_ref[3:4, :], axis=-1, keepdims=True)
    return jnp.exp(a) - jnp.exp(b) + lam_init


def _diff_prompt_kernel(qt_ref, k_ref, vt_ref, lam_ref, subln_ref, o_ref, m_sc, l_sc, acc_sc, *, lam_init):
    qi = pl.program_id(1)
    blk = MOBA_BLOCK
    slopes = [s for s in _slopes(N_HEADS_B) for _ in range(2)]
    qms = _masked_queries(qt_ref[...])
    key_i = lax.broadcasted_iota(jnp.int32, (blk, blk), 0)
    qry_i = lax.broadcasted_iota(jnp.int32, (blk, blk), 1)
    diag_dist = (qry_i - key_i).astype(F32)

    m_sc[...] = jnp.full(m_sc.shape, NEG, F32)
    l_sc[...] = jnp.zeros(l_sc.shape, F32)
    acc_sc[...] = jnp.zeros(acc_sc.shape, F32)
    k_lanes = lambda g: slice(LANES * (g // 2), LANES * (g // 2 + 1))
    v_rows = lambda g: slice(V_DIM_B * (g // 2), V_DIM_B * (g // 2 + 1))

    causal = key_i <= qry_i
    _attend_block(k_ref[qi], vt_ref[qi], qms, diag_dist, lambda g: causal,
                  m_sc, l_sc, acc_sc, slopes, k_lanes, v_rows)

    def past(n, carry):
        dist = diag_dist + ((qi - n) * blk).astype(F32)
        _attend_block(k_ref[n], vt_ref[n], qms, dist, lambda g: None,
                      m_sc, l_sc, acc_sc, slopes, k_lanes, v_rows)
        return carry

    lax.fori_loop(0, qi, past, 0)

    lam = _lambda(lam_ref, lam_init)
    parts = []
    for h in range(N_HEADS_B):
        o0 = acc_sc[V_DIM_B * (2 * h):V_DIM_B * (2 * h + 1), :] * (1.0 / l_sc[2 * h:2 * h + 1, :])
        o1 = acc_sc[V_DIM_B * (2 * h + 1):V_DIM_B * (2 * h + 2), :] * (1.0 / l_sc[2 * h + 1:2 * h + 2, :])
        o = o0 - lam * o1
        ms = jnp.mean(o * o, axis=0, keepdims=True)
        parts.append(o * lax.rsqrt(ms + RMS_EPS))
    o_t = jnp.concatenate(parts, axis=0).T
    o_ref[...] = (o_t * subln_ref[...] * (1.0 - lam_init)).astype(BF16)


def _prompt_attention(qt_a, kbf_a, vt_a, km_a, qt_b, kbf_b, vt_b, lam_vecs, subln_row, lam_init):
    b, nb = qt_a.shape[:2]
    blk = MOBA_BLOCK
    s = nb * blk
    q_spec = pl.BlockSpec((None, None, WIDTH, blk), lambda i, t: (i, t, 0, 0))
    k_spec = pl.BlockSpec((None, nb, blk, WIDTH), lambda i, t: (i, 0, 0, 0))
    vt_spec = pl.BlockSpec((None, nb, WIDTH, blk), lambda i, t: (i, 0, 0, 0))
    o_spec = pl.BlockSpec((None, blk, WIDTH), lambda i, t: (i, t, 0))
    o_shape = jax.ShapeDtypeStruct((b, s, WIDTH), BF16)
    stat = pltpu.VMEM((N_GROUPS, blk), F32)
    o_a = pl.pallas_call(
        _moba_prompt_kernel,
        grid=(b, nb),
        in_specs=[q_spec, k_spec, vt_spec, pl.BlockSpec((None, nb, WIDTH), lambda i, t: (i, 0, 0))],
        out_specs=o_spec,
        out_shape=o_shape,
        scratch_shapes=[pltpu.VMEM((N_GROUPS, nb, blk), F32), stat, stat,
                        pltpu.VMEM((WIDTH, blk), F32)],
        compiler_params=_params("arbitrary", "arbitrary"),
        name="moba_prompt",
    )(qt_a, kbf_a, vt_a, km_a)
    o_b = pl.pallas_call(
        functools.partial(_diff_prompt_kernel, lam_init=lam_init),
        grid=(b, nb),
        in_specs=[q_spec, k_spec, vt_spec, _const_spec((4, HEAD_DIM_B)), _const_spec((1, WIDTH))],
        out_specs=o_spec,
        out_shape=o_shape,
        scratch_shapes=[stat, stat, pltpu.VMEM((N_GROUPS * V_DIM_B, blk), F32)],
        compiler_params=_params("arbitrary", "arbitrary"),
        name="diff_prompt",
    )(qt_b, kbf_b, vt_b, lam_vecs, subln_row)
    return o_a, o_b


def _mix_ffn_kernel(x_ref, oa_ref, ob_ref, ga_ref, gb_ref, mod_ref, wba_ref, wbd_ref, wout_ref,
                    nw_ref, wg_ref, wu_ref, wd_ref, o_ref, *, tf):
    y_a = _dot(oa_ref[...], wba_ref[...])
    y_b = _dot(ob_ref[...], wbd_ref[...])
    mixed = _dot((ga_ref[...] * y_a + gb_ref[...] * y_b).astype(BF16), wout_ref[...])
    x = x_ref[...] + mod_ref[5] * mixed
    o_ref[...] = _ffn_update(x, mod_ref[6], mod_ref[7], mod_ref[8], nw_ref[...],
                             wg_ref, wu_ref, wd_ref, tf)


def _mix_ffn(x, o_a, o_b, g_a, g_b, mod, w_ba, w_bd, w_out, nw, wg, wu, wd, tm):
    b, s, d = x.shape
    r = mod.shape[2]
    d_ff = wg.shape[1]
    tf = _pick_ff_tile(d_ff)
    row_spec = lambda w: pl.BlockSpec((None, tm, w), lambda i, t: (i, t, 0))
    return pl.pallas_call(
        functools.partial(_mix_ffn_kernel, tf=tf),
        grid=(b, s // tm),
        in_specs=[row_spec(d), row_spec(WIDTH), row_spec(WIDTH), row_spec(d), row_spec(d),
                  pl.BlockSpec((None, N_MOD, r, d), lambda i, t: (i, 0, 0, 0)),
                  _const_spec((WIDTH, d)), _const_spec((WIDTH, d)), _const_spec((d, d)),
                  _const_spec((1, d)),
                  _const_spec((d, d_ff)), _const_spec((d, d_ff)), _const_spec((d_ff, d))],
        out_specs=row_spec(d),
        out_shape=jax.ShapeDtypeStruct((b, s, d), F32),
        compiler_params=_params("arbitrary", "arbitrary"),
        name="mix_ffn",
    )(x, o_a, o_b, g_a, g_b, mod, w_ba, w_bd, w_out, nw, wg, wu, wd)


def _moba_gate_kernel(pt_ref, q_ref, *rest):
    del pt_ref
    k_refs, gate_ref = rest[:PAGES_PER_STEP], rest[PAGES_PER_STEP]
    q = _bf16_round(q_ref[...])
    for bi in range(PAGES_PER_STEP // PAGES_PER_BLOCK):
        pages = [k_refs[bi * PAGES_PER_BLOCK + i][...] for i in range(PAGES_PER_BLOCK)]
        ksum = jnp.sum(functools.reduce(jnp.add, pages), axis=-1, keepdims=True)
        gate_ref[bi] = jnp.sum(_bf16_round(ksum * (1.0 / MOBA_BLOCK)) * q, axis=1)


def _moba_select_kernel(gate_ref, sel_ref):
    gate = gate_ref[...]
    nblk = gate.shape[1]
    blk_i = lax.broadcasted_iota(jnp.int32, gate.shape, 1)
    rank = jnp.zeros(gate.shape, jnp.int32)
    for m in range(nblk):
        gm = gate[:, m:m + 1]
        beats = (gm > gate) | ((gm == gate) & (m < blk_i))
        rank = rank + beats.astype(jnp.int32)
    for j in range(MOBA_TOPK):
        sel_ref[:, j] = jnp.sum(jnp.where(rank == j, blk_i, 0), axis=1)


def _moba_attend_kernel(pt_ref, sel_ref, q_ref, kn_ref, vn_ref, *rest, past_len):
    del pt_ref
    nt = N_HEADS_A * MOBA_TOPK * PAGES_PER_BLOCK
    k_refs, v_refs, o_ref = rest[:nt], rest[nt:2 * nt], rest[2 * nt]
    s = pl.program_id(0)
    lane = lax.broadcasted_iota(jnp.int32, (1, PAGE_SIZE), 1).astype(F32)
    slopes = _slopes(N_HEADS_A)
    for h in range(N_HEADS_A):
        q_h = q_ref[h]
        tiles = range(h * MOBA_TOPK * PAGES_PER_BLOCK, (h + 1) * MOBA_TOPK * PAGES_PER_BLOCK)
        rows = []
        for j in range(MOBA_TOPK):
            blk = sel_ref[s, j * N_HEADS_A + h]
            for i in range(PAGES_PER_BLOCK):
                kt = k_refs[tiles[j * PAGES_PER_BLOCK + i]][...]
                dist0 = (past_len - blk * MOBA_BLOCK - i * PAGE_SIZE).astype(F32)
                rows.append(jnp.sum(kt * q_h, axis=0, keepdims=True) - slopes[h] * (dist0 - lane))
        s_self = jnp.sum(q_h * kn_ref[h], axis=0, keepdims=True)
        m = s_self
        for r in rows:
            m = jnp.maximum(m, jnp.max(r, axis=1, keepdims=True))
        w_self = jnp.exp(s_self - m)
        l = w_self
        acc = jnp.zeros((HEAD_DIM_A, PAGE_SIZE), F32)
        for t, r in zip(tiles, rows):
            p = jnp.exp(r - m)
            l = l + jnp.sum(p, axis=1, keepdims=True)
            acc = acc + p * v_refs[t][...]
        o = jnp.sum(acc, axis=1, keepdims=True) + w_self * vn_ref[h]
        o_ref[h] = o * (1.0 / l)


def _moba_decode(q, k_new, v_new, k_pool, v_pool, page_table, layer):
    db, n_pages = page_table.shape
    nblk = n_pages // PAGES_PER_BLOCK
    assert nblk >= MOBA_TOPK
    bps = PAGES_PER_STEP // PAGES_PER_BLOCK
    as_tiles = lambda pool: jnp.transpose(pool, (0, 1, 3, 4, 2))
    col = lambda a: a[..., None]
    kt_pool, vt_pool = as_tiles(k_pool), as_tiles(v_pool)
    tile_shape = (None, None, N_HEADS_A, HEAD_DIM_A, PAGE_SIZE)

    gate = pl.pallas_call(
        _moba_gate_kernel,
        grid_spec=pltpu.PrefetchScalarGridSpec(
            num_scalar_prefetch=1,
            grid=(db, n_pages // PAGES_PER_STEP),
            in_specs=[pl.BlockSpec((None, N_HEADS_A, HEAD_DIM_A, 1), lambda s, j, pt: (s, 0, 0, 0))]
            + [pl.BlockSpec(tile_shape, functools.partial(
                lambda s, j, pt, i: (layer, pt[s, j * PAGES_PER_STEP + i], 0, 0, 0), i=i))
               for i in range(PAGES_PER_STEP)],
            out_specs=pl.BlockSpec((None, bps, N_HEADS_A, 1), lambda s, j, pt: (s, j, 0, 0)),
        ),
        out_shape=jax.ShapeDtypeStruct((db, nblk, N_HEADS_A, 1), F32),
        compiler_params=_params("arbitrary", "arbitrary"),
        name="moba_gate",
    )(page_table, col(q), *([kt_pool] * PAGES_PER_STEP))

    sel = pl.pallas_call(
        _moba_select_kernel,
        grid=(1,),
        in_specs=[_const_spec(gate.shape)],
        out_specs=_const_spec((db, MOBA_TOPK, N_HEADS_A, 1)),
        out_shape=jax.ShapeDtypeStruct((db, MOBA_TOPK, N_HEADS_A, 1), jnp.int32),
        compiler_params=_params("arbitrary"),
        name="moba_select",
    )(gate).reshape(db, MOBA_TOPK * N_HEADS_A)

    def tile_spec(h, j, i):
        def index(s, pt, sel_):
            blk = jnp.clip(sel_[s, j * N_HEADS_A + h], 0, nblk - 1)
            return (layer, pt[s, PAGES_PER_BLOCK * blk + i], h, 0, 0)
        return pl.BlockSpec((None, None, None, HEAD_DIM_A, PAGE_SIZE), index)

    tile_specs = [tile_spec(h, j, i) for h in range(N_HEADS_A) for j in range(MOBA_TOPK)
                  for i in range(PAGES_PER_BLOCK)]
    seq = pl.BlockSpec((None, N_HEADS_A, HEAD_DIM_A, 1), lambda s, pt, sel_: (s, 0, 0, 0))
    o = pl.pallas_call(
        functools.partial(_moba_attend_kernel, past_len=n_pages * PAGE_SIZE),
        grid_spec=pltpu.PrefetchScalarGridSpec(
            num_scalar_prefetch=2,
            grid=(db,),
            in_specs=[seq, seq, seq] + tile_specs * 2,
            out_specs=seq,
        ),
        out_shape=jax.ShapeDtypeStruct((db, N_HEADS_A, HEAD_DIM_A, 1), F32),
        compiler_params=_params("arbitrary"),
        name="moba_attend",
    )(page_table, sel, col(q), col(k_new), col(v_new),
      *([kt_pool] * len(tile_specs)), *([vt_pool] * len(tile_specs)))
    return o[..., 0]


def _diff_decode_kernel(pt_ref, q_ref, kn_ref, vn_ref, slope_ref, tbias_ref, b01_ref, lam_ref, subln_ref,
                        *rest, past_len, lam_init):
    del pt_ref
    npg = PAGES_PER_STEP
    k_refs, v_refs = rest[:npg], rest[npg:2 * npg]
    o_ref, m_sc, l_sc, a0_sc, a1_sc = rest[2 * npg:]
    j = pl.program_id(1)
    q8 = q_ref[...]
    b01 = b01_ref[...]
    slope8 = slope_ref[...]
    nv = PAGE_SIZE * N_HEADS_B // SUBLANES
    tbias = tbias_ref[...].reshape(nv, SUBLANES, 2 * LANES)
    both = lambda x: x + pltpu.roll(x, N_HEADS_B, axis=0)

    @pl.when(j == 0)
    def _():
        m_sc[...] = _dot((q8 * kn_ref[...]).astype(BF16), b01)
        l_sc[...] = jnp.ones(l_sc.shape, F32)
        a0_sc[...] = vn_ref[...]
        a1_sc[...] = vn_ref[...]

    m_run, l_run, a0_run, a1_run = m_sc[...], l_sc[...], a0_sc[...], a1_sc[...]
    for idx in range(npg):
        k3 = k_refs[idx][...].reshape(nv, SUBLANES, LANES)
        prod = (k3 * q8[None]).reshape(nv * SUBLANES, LANES).astype(BF16)
        dist0 = (past_len - (j * npg + idx) * PAGE_SIZE).astype(F32)
        s = _dot(prod, b01).reshape(nv, SUBLANES, 2 * LANES) + tbias - (slope8 * dist0)[None]
        m_pg = jnp.max(s, axis=0)
        m_new = jnp.maximum(m_run, jnp.maximum(m_pg, pltpu.roll(m_pg, N_HEADS_B, axis=0)))
        alpha = jnp.exp(m_run - m_new)
        p = jnp.exp(s - m_new[None])
        v3 = v_refs[idx][...].reshape(nv, SUBLANES, LANES)
        l_run = alpha * l_run + both(jnp.sum(p, axis=0))
        a0_run = alpha[:, :LANES] * a0_run + both(jnp.sum(p[:, :, :LANES] * v3, axis=0))
        a1_run = alpha[:, LANES:] * a1_run + both(jnp.sum(p[:, :, LANES:] * v3, axis=0))
        m_run = m_new
    m_sc[...] = m_run
    l_sc[...] = l_run
    a0_sc[...] = a0_run
    a1_sc[...] = a1_run

    @pl.when(j == pl.num_programs(1) - 1)
    def _():
        inv = 1.0 / l_run
        o = a0_run * inv[:, :LANES] - _lambda(lam_ref, lam_init) * (a1_run * inv[:, LANES:])
        ms = jnp.mean(o * o, axis=-1, keepdims=True)
        o_ref[...] = o * lax.rsqrt(ms + RMS_EPS) * subln_ref[...] * (1.0 - lam_init)


def _diff_decode(q, k_new, v_new, k_pool, v_pool, page_table, layer, lam_vecs, subln, lam_init):
    db, n_pages = page_table.shape
    depth, n_phys = k_pool.shape[:2]
    rows = PAGE_SIZE * N_HEADS_B
    pages = lambda pool: pool.reshape(depth * n_phys, rows, V_DIM_B)
    twice = lambda a: jnp.concatenate([a, a], axis=1)
    slopes = np.asarray(_slopes(N_HEADS_B), np.float32)
    slope_np = np.repeat(np.tile(slopes, 2)[:, None], 2 * LANES, axis=1)
    r = np.arange(rows)
    tbias_np = np.repeat((slopes[r % N_HEADS_B] * (r // N_HEADS_B))[:, None], 2 * LANES, axis=1)
    half = np.arange(LANES) // HEAD_DIM_B
    b01_np = (half[:, None] == (np.arange(2 * LANES) // LANES)[None, :]).astype(np.float32)
    seq = pl.BlockSpec((None, SUBLANES, V_DIM_B), lambda s, j, pt: (s, 0, 0))
    cst = lambda shp: pl.BlockSpec(shp, lambda s, j, pt: (0, 0))
    page = lambda i: pl.BlockSpec(
        (None, rows, V_DIM_B), lambda s, j, pt: (layer * n_phys + pt[s, j * PAGES_PER_STEP + i], 0, 0))
    page_specs = [page(i) for i in range(PAGES_PER_STEP)]
    grid_spec = pltpu.PrefetchScalarGridSpec(
        num_scalar_prefetch=1,
        grid=(db, n_pages // PAGES_PER_STEP),
        in_specs=[seq, seq, seq, cst((SUBLANES, 2 * LANES)), cst((rows, 2 * LANES)), cst((LANES, 2 * LANES)),
                  cst((4, HEAD_DIM_B)), cst((1, V_DIM_B))]
        + page_specs * 2,
        out_specs=seq,
        scratch_shapes=[pltpu.VMEM((SUBLANES, 2 * LANES), F32), pltpu.VMEM((SUBLANES, 2 * LANES), F32),
                        pltpu.VMEM((SUBLANES, LANES), F32), pltpu.VMEM((SUBLANES, LANES), F32)],
    )
    o = pl.pallas_call(
        functools.partial(_diff_decode_kernel, past_len=n_pages * PAGE_SIZE, lam_init=lam_init),
        grid_spec=grid_spec,
        out_shape=jax.ShapeDtypeStruct((db, SUBLANES, V_DIM_B), F32),
        compiler_params=_params("arbitrary", "arbitrary"),
        name="diff_decode",
    )(page_table, twice(q), twice(k_new), twice(v_new), jnp.asarray(slope_np), jnp.asarray(tbias_np),
      jnp.asarray(b01_np, BF16), lam_vecs, subln,
      *([pages(k_pool)] * PAGES_PER_STEP), *([pages(v_pool)] * PAGES_PER_STEP))
    return o[:, :N_HEADS_B]


def _block_diag_ones():
    i = np.arange(MXU_DIM) // 64
    return jnp.asarray((i[:, None] == i[None, :]).astype(np.float32), BF16)


def kernel(x_prompt, x_sample, cache_k_moba, cache_v_moba, cache_k_diff, cache_v_diff, page_table, c_prompt, c_sample, w_ada, b_ada, norm_ffn1, ffn1_w_gate, ffn1_w_up, ffn1_w_down, norm_mix, w_in, qn_moba, kn_moba, qn_diff, kn_diff, lambda_q1, lambda_k1, lambda_q2, lambda_k2, subln_diff, w_branch_moba, w_branch_diff, w_out, norm_ffn2, ffn2_w_gate, ffn2_w_up, ffn2_w_down):
    depth = w_ada.shape[0]
    b, s, d = x_prompt.shape
    db, t_new, _ = x_sample.shape
    assert t_new == 1 and s % TOKEN_TILE == 0 and db % SUBLANES == 0
    n_pages = page_table.shape[1]
    assert n_pages % PAGES_PER_STEP == 0
    bd = _block_diag_ones()
    tile8 = lambda v: jnp.tile(v, WIDTH // v.shape[0]).reshape(1, WIDTH)

    y_p, y_s = x_prompt, x_sample.reshape(1, db, d)
    rows_p, rows_s = [], []
    for l in range(depth):
        lam_init = _lambda_init(l)
        bf = lambda w: w[l].astype(BF16)
        row = lambda v: v[l].reshape(1, -1)
        mod = _ada(jnp.concatenate([c_prompt, c_sample], axis=0), bf(w_ada), b_ada[l])
        mod_p = mod[:b].reshape(b, N_MOD, 1, d)
        mod_s = mod[b:].reshape(db, N_MOD, d).transpose(1, 0, 2).reshape(1, N_MOD, db, d)
        ffn1 = (row(norm_ffn1), bf(ffn1_w_gate), bf(ffn1_w_up), bf(ffn1_w_down))
        ffn2 = (row(norm_ffn2), bf(ffn2_w_gate), bf(ffn2_w_up), bf(ffn2_w_down))
        w_in_bf = bf(w_in)
        gains = jnp.concatenate([tile8(qn_moba[l]), tile8(kn_moba[l]), tile8(qn_diff[l]), tile8(kn_diff[l])], 0)
        lam_vecs = jnp.stack([lambda_q1[l], lambda_k1[l], lambda_q2[l], lambda_k2[l]])
        mix_w = (bf(w_branch_moba), bf(w_branch_diff), bf(w_out))

        x1 = _ffn(y_p, mod_p, *ffn1, k0=0, tm=TOKEN_TILE)
        (kt_a, vt32_a, k_b, v_b, kbf_a, kbf_b, qt_a, qt_b, vt_a, vt_b, km_a, g_a, g_b) = _proj(
            x1, mod_p, row(norm_mix), w_in_bf, gains, bd, TOKEN_TILE, True)
        o_a, o_b = _prompt_attention(qt_a, kbf_a, vt_a, km_a.reshape(b, -1, WIDTH), qt_b, kbf_b, vt_b,
                                     lam_vecs, tile8(subln_diff[l]), lam_init)
        y_p = _mix_ffn(x1, o_a, o_b, g_a, g_b, mod_p, *mix_w, *ffn2, tm=TOKEN_TILE)
        token_major = lambda a: a.reshape(b, N_HEADS_A, HEAD_DIM_A, s).transpose(0, 3, 1, 2)
        rows_p.append((token_major(kt_a), token_major(vt32_a), k_b, v_b))

        x1s = _ffn(y_s, mod_s, *ffn1, k0=0, tm=db)
        (q_as, k_as, v_as, q_bs, k_bs, v_bs, g_as, g_bs) = _proj(
            x1s, mod_s, row(norm_mix), w_in_bf, gains, bd, db, False)
        heads_a = lambda a: a.reshape(db, N_HEADS_A, HEAD_DIM_A)
        heads_b = lambda a: a.reshape(db, N_HEADS_B, V_DIM_B)
        o_as = _moba_decode(heads_a(q_as), heads_a(k_as), heads_a(v_as), cache_k_moba, cache_v_moba,
                            page_table, l)
        o_bs = _diff_decode(heads_b(q_bs), heads_b(k_bs), heads_b(v_bs), cache_k_diff, cache_v_diff,
                            page_table, l, lam_vecs, subln_diff[l].reshape(1, V_DIM_B), lam_init)
        as_rows = lambda a: a.reshape(1, db, WIDTH).astype(BF16)
        y_s = _mix_ffn(x1s, as_rows(o_as), as_rows(o_bs), g_as, g_bs, mod_s, *mix_w, *ffn2, tm=db)
        rows_s.append((k_as, v_as, k_bs, v_bs))

    def stack(rows, i, lead, heads, hd):
        return jnp.stack([r[i].reshape(lead + (heads, hd)) for r in rows])

    lp, ls = (b, s), (db, 1)
    return (y_p, y_s.reshape(db, 1, d),
            stack(rows_p, 0, lp, N_HEADS_A, HEAD_DIM_A), stack(rows_p, 1, lp, N_HEADS_A, HEAD_DIM_A),
            stack(rows_p, 2, lp, N_HEADS_B, V_DIM_B), stack(rows_p, 3, lp, N_HEADS_B, V_DIM_B),
            stack(rows_s, 0, ls, N_HEADS_A, HEAD_DIM_A), stack(rows_s, 1, ls, N_HEADS_A, HEAD_DIM_A),
            stack(rows_s, 2, ls, N_HEADS_B, V_DIM_B), stack(rows_s, 3, ls, N_HEADS_B, V_DIM_B))
```

```python
import functools
import math

import jax
import jax.numpy as jnp
import numpy as np
from jax import lax
from jax.experimental import pallas as pl
from jax.experimental.pallas import tpu as pltpu

F32 = jnp.float32
BF16 = jnp.bfloat16

N_HEADS_A = 8
HEAD_DIM_A = 64
MOBA_BLOCK = 256
MOBA_TOPK = 3
N_HEADS_B = 4
HEAD_DIM_B = 64
V_DIM_B = 2 * HEAD_DIM_B
WIDTH = 512
N_GROUPS = WIDTH // 64
PAGE_SIZE = 128
PAGES_PER_BLOCK = MOBA_BLOCK // PAGE_SIZE
N_MOD = 9
RMS_EPS = 1e-6
QK_SCALE = 0.125
NEG = -1e30

LANES = 128
SUBLANES = 8
MXU_DIM = 256
VMEM_LIMIT_BYTES = 56 * 1024 * 1024

TOKEN_TILE = 512
PAGES_PER_STEP = 16


def _slopes(n_heads):
    return [2.0 ** (-8.0 * (i + 1) / n_heads) for i in range(n_heads)]


def _lambda_init(layer):
    return 0.8 - 0.6 * math.exp(-0.3 * layer)


def _dot(a, b):
    return jnp.dot(a, b, preferred_element_type=F32)


def _bf16_round(x):
    return x.astype(BF16).astype(F32)


def _rms(x, w):
    ms = jnp.mean(x * x, axis=-1, keepdims=True)
    return x * lax.rsqrt(ms + RMS_EPS) * w


def _pick_ff_tile(d_ff):
    best = LANES
    for t in range(LANES, min(d_ff, 1408) + 1, LANES):
        if d_ff % t == 0:
            best = t
    return best


def _params(*sem):
    return pltpu.CompilerParams(dimension_semantics=sem, vmem_limit_bytes=VMEM_LIMIT_BYTES)


def _const_spec(shape):
    nd = len(shape)
    return pl.BlockSpec(shape, lambda *_: (0,) * nd)


def _ada_kernel(c_ref, w_ref, b_ref, o_ref):
    c = c_ref[...]
    s = c * jax.nn.sigmoid(c)
    o_ref[...] = _dot(s.astype(BF16), w_ref[...]) + b_ref[...]


def _ada(c, w_bf, b):
    m, d = c.shape
    n = w_bf.shape[1]
    tn = 1024 if n % 1024 == 0 else n
    return pl.pallas_call(
        _ada_kernel,
        grid=(n // tn,),
        in_specs=[pl.BlockSpec((m, d), lambda j: (0, 0)),
                  pl.BlockSpec((d, tn), lambda j: (0, j)),
                  pl.BlockSpec((1, tn), lambda j: (0, j))],
        out_specs=pl.BlockSpec((m, tn), lambda j: (0, j)),
        out_shape=jax.ShapeDtypeStruct((m, n), F32),
        compiler_params=_params("arbitrary"),
        name="ada_mod",
    )(c, w_bf, b.reshape(1, n))


def _ffn_update(x, shift, scale, gate, nw, wg_ref, wu_ref, wd_ref, tf):
    h = (_rms(x, nw) * (1.0 + scale) + shift).astype(BF16)
    d_ff = wg_ref.shape[1]
    acc = jnp.zeros(x.shape, F32)
    for j in range(d_ff // tf):
        g = _dot(h, wg_ref[:, j * tf:(j + 1) * tf])
        u = _dot(h, wu_ref[:, j * tf:(j + 1) * tf])
        a = (g * jax.nn.sigmoid(g) * u).astype(BF16)
        acc = acc + _dot(a, wd_ref[j * tf:(j + 1) * tf, :])
    return x + 0.5 * gate * acc


def _ffn_kernel(x_ref, mod_ref, nw_ref, wg_ref, wu_ref, wd_ref, o_ref, *, k0, tf):
    o_ref[...] = _ffn_update(x_ref[...], mod_ref[k0], mod_ref[k0 + 1], mod_ref[k0 + 2],
                             nw_ref[...], wg_ref, wu_ref, wd_ref, tf)


def _ffn(x, mod, nw, wg, wu, wd, k0, tm):
    b, s, d = x.shape
    r = mod.shape[2]
    d_ff = wg.shape[1]
    tf = _pick_ff_tile(d_ff)
    return pl.pallas_call(
        functools.partial(_ffn_kernel, k0=k0, tf=tf),
        grid=(b, s // tm),
        in_specs=[pl.BlockSpec((None, tm, d), lambda i, t: (i, t, 0)),
                  pl.BlockSpec((None, N_MOD, r, d), lambda i, t: (i, 0, 0, 0)),
                  _const_spec((1, d)),
                  _const_spec((d, d_ff)), _const_spec((d, d_ff)), _const_spec((d_ff, d))],
        out_specs=pl.BlockSpec((None, tm, d), lambda i, t: (i, t, 0)),
        out_shape=jax.ShapeDtypeStruct((b, s, d), F32),
        compiler_params=_params("arbitrary", "arbitrary"),
        name="ffn",
    )(x, mod, nw, wg, wu, wd)


def _head_norm(seg, gain_row, bd):
    sq = (seg * seg).astype(BF16)
    parts = [_dot(sq[:, c * MXU_DIM:(c + 1) * MXU_DIM], bd) for c in range(WIDTH // MXU_DIM)]
    ms = jnp.concatenate(parts, axis=1) * (1.0 / 64.0)
    return seg * lax.rsqrt(ms + RMS_EPS) * gain_row


def _proj_kernel(x_ref, mod_ref, nw_ref, win_ref, gains_ref, bd_ref, *outs, transposed):
    x = x_ref[...]
    d = x.shape[1]
    h = (_rms(x, nw_ref[...]) * (1.0 + mod_ref[4]) + mod_ref[3]).astype(BF16)
    bd = bd_ref[...]

    def seg(j):
        return _dot(h, win_ref[:, j * WIDTH:(j + 1) * WIDTH])

    q_a = _head_norm(seg(0), gains_ref[0:1, :], bd) * QK_SCALE
    k_a = _head_norm(seg(1), gains_ref[1:2, :], bd)
    v_a = seg(2)
    q_b = _head_norm(seg(3), gains_ref[2:3, :], bd) * QK_SCALE
    k_b = _head_norm(seg(4), gains_ref[3:4, :], bd)
    v_b = seg(5)
    g0 = 6 * WIDTH
    g_a = jax.nn.sigmoid(_dot(h, win_ref[:, g0:g0 + d]))
    g_b = jax.nn.sigmoid(_dot(h, win_ref[:, g0 + d:g0 + 2 * d]))

    if not transposed:
        (qa_ref, ka_ref, va_ref, qb_ref, kb_ref, vb_ref, ga_ref, gb_ref) = outs
        qa_ref[...] = q_a
        ka_ref[...] = k_a
        va_ref[...] = v_a
        qb_ref[...] = q_b
    else:
        (kta_ref, vta32_ref, kb_ref, vb_ref, kabf_ref, kbbf_ref, qta_ref, qtb_ref,
         vta_ref, vtb_ref, km_ref, ga_ref, gb_ref) = outs
        kta_ref[...] = k_a.T
        vt_a = v_a.T
        vta32_ref[...] = vt_a
        qt_a, qt_b, vt_b = q_a.T, q_b.T, v_b.T
        for r in range(x.shape[0] // MOBA_BLOCK):
            rows = slice(r * MOBA_BLOCK, (r + 1) * MOBA_BLOCK)
            kabf_ref[r] = k_a[rows].astype(BF16)
            kbbf_ref[r] = k_b[rows].astype(BF16)
            qta_ref[r] = qt_a[:, rows].astype(BF16)
            qtb_ref[r] = qt_b[:, rows].astype(BF16)
            vta_ref[r] = vt_a[:, rows].astype(BF16)
            vtb_ref[r] = vt_b[:, rows].astype(BF16)
            km_ref[r] = jnp.sum(k_a[rows], axis=0, keepdims=True) * (1.0 / MOBA_BLOCK)
    kb_ref[...] = k_b
    vb_ref[...] = v_b
    ga_ref[...] = g_a
    gb_ref[...] = g_b


def _proj(x, mod, nw, w_in, gains, bd, tm, transposed):
    b, s, d = x.shape
    r = mod.shape[2]
    d_in = w_in.shape[1]
    row_spec = lambda w: pl.BlockSpec((None, tm, w), lambda i, t: (i, t, 0))
    rows = lambda w, dt=F32: jax.ShapeDtypeStruct((b, s, w), dt)
    if transposed:
        nb, bpt = s // MOBA_BLOCK, tm // MOBA_BLOCK
        blk = lambda shp: pl.BlockSpec((None, bpt) + shp, lambda i, t: (i, t, 0, 0))
        col_spec = pl.BlockSpec((None, WIDTH, tm), lambda i, t: (i, 0, t))
        out_specs = [col_spec] * 2 + [row_spec(WIDTH)] * 2 + [blk((MOBA_BLOCK, WIDTH))] * 2 \
            + [blk((WIDTH, MOBA_BLOCK))] * 4 + [blk((1, WIDTH))] + [row_spec(d)] * 2
        out_shape = [jax.ShapeDtypeStruct((b, WIDTH, s), F32)] * 2 + [rows(WIDTH)] * 2 \
            + [jax.ShapeDtypeStruct((b, nb, MOBA_BLOCK, WIDTH), BF16)] * 2 \
            + [jax.ShapeDtypeStruct((b, nb, WIDTH, MOBA_BLOCK), BF16)] * 4 \
            + [jax.ShapeDtypeStruct((b, nb, 1, WIDTH), F32)] + [rows(d)] * 2
    else:
        out_specs = [row_spec(WIDTH)] * 6 + [row_spec(d)] * 2
        out_shape = [rows(WIDTH)] * 6 + [rows(d)] * 2
    return pl.pallas_call(
        functools.partial(_proj_kernel, transposed=transposed),
        grid=(b, s // tm),
        in_specs=[pl.BlockSpec((None, tm, d), lambda i, t: (i, t, 0)),
                  pl.BlockSpec((None, N_MOD, r, d), lambda i, t: (i, 0, 0, 0)),
                  _const_spec((1, d)), _const_spec((d, d_in)),
                  _const_spec((4, WIDTH)), _const_spec((MXU_DIM, MXU_DIM))],
        out_specs=out_specs,
        out_shape=out_shape,
        compiler_params=_params("arbitrary", "arbitrary"),
        name="mixer_proj",
    )(x, mod, nw, w_in, gains, bd)


FEATURE_ROWS = SUBLANES


def _key_features(n):
    lane = lax.broadcasted_iota(jnp.int32, (MOBA_BLOCK, LANES), 1)
    key = lax.broadcasted_iota(jnp.int32, (MOBA_BLOCK, LANES), 0).astype(F32)
    f = jnp.where(lane == 0, 1.0, jnp.where(lane == 1, key, jnp.where(lane == FEATURE_ROWS + n, 1.0, 0.0)))
    return f.astype(BF16)


def _query_features(slope, block_bias):
    qry = lax.broadcasted_iota(jnp.int32, (FEATURE_ROWS, MOBA_BLOCK), 1).astype(F32)
    row = lax.broadcasted_iota(jnp.int32, (FEATURE_ROWS, MOBA_BLOCK), 0)
    head = jnp.where(row == 0, -slope * qry, jnp.where(row == 1, slope, 0.0))
    pad = jnp.zeros((LANES - FEATURE_ROWS - block_bias.shape[0], MOBA_BLOCK), F32)
    return jnp.concatenate([head, block_bias, pad], axis=0).astype(BF16)


def _augmented_query_pairs(qms, slopes, block_biases):
    q_aug = [jnp.concatenate([qm, _query_features(s, bb)], axis=0)
             for qm, s, bb in zip(qms, slopes, block_biases)]
    return [jnp.concatenate(q_aug[2 * p:2 * p + 2], axis=1) for p in range(N_GROUPS // 2)]


def _pair_scores(kb, kfeat, q_pairs, p):
    k_aug = jnp.concatenate([kb[:, LANES * p:LANES * (p + 1)], kfeat], axis=1)
    return _dot(k_aug, q_pairs[p])


def _attend_block(kb, vtb, kfeat, q_pairs, causal, m_all, l_all, acc_sc, v_rows, st_first, s0_sc, k_next):
    m_out, l_out = [], []
    n_pairs = len(q_pairs)
    st_next = _pair_scores(kb, kfeat, q_pairs, 0) if st_first is None else st_first
    for p in range(n_pairs):
        st_pair = st_next
        if p + 1 < n_pairs:
            st_next = _pair_scores(kb, kfeat, q_pairs, p + 1)
        else:
            s0_sc[...] = _pair_scores(*k_next, q_pairs, 0)
        for half in range(2):
            g = 2 * p + half
            st = st_pair[:, MOBA_BLOCK * half:MOBA_BLOCK * (half + 1)]
            if causal is not None:
                st = jnp.where(causal, st, NEG)
            m_old = m_all[g:g + 1, :]
            m_new = jnp.maximum(m_old, jnp.max(st, axis=0, keepdims=True))
            alpha = jnp.exp(m_old - m_new)
            pt = jnp.exp(st - m_new)
            l_out.append(alpha * l_all[g:g + 1, :] + jnp.sum(pt, axis=0, keepdims=True))
            rows = v_rows(g)
            nr = rows.stop - rows.start
            acc_rows = slice(g * nr, (g + 1) * nr)
            acc_sc[acc_rows, :] = alpha * acc_sc[acc_rows, :] + _dot(vtb[rows, :], pt.astype(BF16))
            m_out.append(m_new)
    return jnp.concatenate(m_out, axis=0), jnp.concatenate(l_out, axis=0)


def _attend_all_blocks(qi, k_ref, vt_ref, q_pairs, acc_sc, s0_sc, v_rows):
    key_i = lax.broadcasted_iota(jnp.int32, (MOBA_BLOCK, MOBA_BLOCK), 0)
    qry_i = lax.broadcasted_iota(jnp.int32, (MOBA_BLOCK, MOBA_BLOCK), 1)
    acc_sc[...] = jnp.zeros(acc_sc.shape, F32)
    m0 = jnp.full((N_GROUPS, MOBA_BLOCK), NEG, F32)
    l0 = jnp.zeros((N_GROUPS, MOBA_BLOCK), F32)
    m1, l1 = _attend_block(k_ref[qi], vt_ref[qi], _key_features(qi), q_pairs, key_i <= qry_i,
                           m0, l0, acc_sc, v_rows, None, s0_sc, (k_ref[0], _key_features(0)))

    def past(n, carry):
        return _attend_block(k_ref[n], vt_ref[n], _key_features(n), q_pairs, None, *carry, acc_sc, v_rows,
                             s0_sc[...], s0_sc, (k_ref[n + 1], _key_features(n + 1)))

    _, l_fin = lax.fori_loop(0, qi, past, (m1, l1))
    return l_fin


def _masked_queries(qt):
    row = lax.broadcasted_iota(jnp.int32, (LANES, MOBA_BLOCK), 0)
    out = []
    for g in range(N_GROUPS):
        p, half = divmod(g, 2)
        pair = qt[LANES * p:LANES * (p + 1), :]
        keep = (row >= 64 * half) & (row < 64 * (half + 1))
        out.append(jnp.where(keep, pair, jnp.zeros_like(pair)))
    return out


def _block_distance(qi, nbp):
    blk_i = lax.broadcasted_iota(jnp.int32, (nbp, MOBA_BLOCK), 0)
    return blk_i, ((qi - blk_i) * MOBA_BLOCK).astype(F32)


def _moba_prompt_kernel(qt_ref, k_ref, vt_ref, km_ref, o_ref, acc_sc, s0_sc):
    qi = pl.program_id(1)
    nb = k_ref.shape[0]
    nbp = -(-nb // SUBLANES) * SUBLANES
    blk = MOBA_BLOCK
    slopes = _slopes(N_HEADS_A)
    qms = _masked_queries(qt_ref[...])

    km = km_ref[...].astype(BF16)
    blk_i, blk_dist = _block_distance(qi, nbp)
    valid = blk_i[:nb] < qi
    biases = []
    for g in range(N_GROUPS):
        lanes = slice(LANES * (g // 2), LANES * (g // 2 + 1))
        gate = _dot(km[:, lanes], qms[g])
        gate = jnp.where(valid, gate, NEG)
        rank = jnp.zeros((nb, blk), jnp.int32)
        for m in range(nb):
            gm = gate[m:m + 1, :]
            beats = (gm > gate) | ((gm == gate) & (m < blk_i[:nb]))
            rank = rank + beats.astype(jnp.int32)
        drop = jnp.where(valid & (rank < MOBA_TOPK), 0.0, NEG)
        if nbp > nb:
            drop = jnp.concatenate([drop, jnp.zeros((nbp - nb, blk), F32)], axis=0)
        biases.append(jnp.where(blk_i == qi, 0.0, drop - slopes[g] * blk_dist))

    q_pairs = _augmented_query_pairs(qms, slopes, biases)
    l_fin = _attend_all_blocks(qi, k_ref, vt_ref, q_pairs, acc_sc, s0_sc,
                               lambda g: slice(64 * g, 64 * (g + 1)))
    parts = [acc_sc[64 * g:64 * (g + 1), :] * (1.0 / l_fin[g:g + 1, :]) for g in range(N_GROUPS)]
    o_ref[...] = jnp.concatenate(parts, axis=0).T.astype(BF16)


def _lambda(lam_ref, lam_init):
    a = jnp.sum(lam_ref[0:1, :] * lam_ref[1:2, :], axis=-1, keepdims=True)
    b = jnp.sum(lam_ref[2:3, :] * lam_ref[3:4, :], axis=-1, keepdims=True)
    return jnp.exp(a) - jnp.exp(b) + lam_init


def _diff_prompt_kernel(qt_ref, k_ref, vt_ref, lam_ref, subln_ref, o_ref, acc_sc, s0_sc, *, lam_init):
    qi = pl.program_id(1)
    nb = k_ref.shape[0]
    nbp = -(-nb // SUBLANES) * SUBLANES
    slopes = [s for s in _slopes(N_HEADS_B) for _ in range(2)]
    qms = _masked_queries(qt_ref[...])
    _, blk_dist = _block_distance(qi, nbp)
    q_pairs = _augmented_query_pairs(qms, slopes, [-s * blk_dist for s in slopes])
    v_rows = lambda g: slice(V_DIM_B * (g // 2), V_DIM_B * (g // 2 + 1))
    l_fin = _attend_all_blocks(qi, k_ref, vt_ref, q_pairs, acc_sc, s0_sc, v_rows)

    lam = _lambda(lam_ref, lam_init)
    parts = []
    for h in range(N_HEADS_B):
        o0 = acc_sc[V_DIM_B * (2 * h):V_DIM_B * (2 * h + 1), :] * (1.0 / l_fin[2 * h:2 * h + 1, :])
        o1 = acc_sc[V_DIM_B * (2 * h + 1):V_DIM_B * (2 * h + 2), :] * (1.0 / l_fin[2 * h + 1:2 * h + 2, :])
        o = o0 - lam * o1
        ms = jnp.mean(o * o, axis=0, keepdims=True)
        parts.append(o * lax.rsqrt(ms + RMS_EPS))
    o_t = jnp.concatenate(parts, axis=0).T
    o_ref[...] = (o_t * subln_ref[...] * (1.0 - lam_init)).astype(BF16)


def _prompt_attention(qt_a, kbf_a, vt_a, km_a, qt_b, kbf_b, vt_b, lam_vecs, subln_row, lam_init):
    b, nb = qt_a.shape[:2]
    blk = MOBA_BLOCK
    s = nb * blk
    q_spec = pl.BlockSpec((None, None, WIDTH, blk), lambda i, t: (i, t, 0, 0))
    k_spec = pl.BlockSpec((None, nb, blk, WIDTH), lambda i, t: (i, 0, 0, 0))
    vt_spec = pl.BlockSpec((None, nb, WIDTH, blk), lambda i, t: (i, 0, 0, 0))
    o_spec = pl.BlockSpec((None, blk, WIDTH), lambda i, t: (i, t, 0))
    o_shape = jax.ShapeDtypeStruct((b, s, WIDTH), BF16)
    o_a = pl.pallas_call(
        _moba_prompt_kernel,
        grid=(b, nb),
        in_specs=[q_spec, k_spec, vt_spec, pl.BlockSpec((None, nb, WIDTH), lambda i, t: (i, 0, 0))],
        out_specs=o_spec,
        out_shape=o_shape,
        scratch_shapes=[pltpu.VMEM((WIDTH, blk), F32), pltpu.VMEM((blk, 2 * blk), F32)],
        compiler_params=_params("arbitrary", "arbitrary"),
        name="moba_prompt",
    )(qt_a, kbf_a, vt_a, km_a)
    o_b = pl.pallas_call(
        functools.partial(_diff_prompt_kernel, lam_init=lam_init),
        grid=(b, nb),
        in_specs=[q_spec, k_spec, vt_spec, _const_spec((4, HEAD_DIM_B)), _const_spec((1, WIDTH))],
        out_specs=o_spec,
        out_shape=o_shape,
        scratch_shapes=[pltpu.VMEM((N_GROUPS * V_DIM_B, blk), F32), pltpu.VMEM((blk, 2 * blk), F32)],
        compiler_params=_params("arbitrary", "arbitrary"),
        name="diff_prompt",
    )(qt_b, kbf_b, vt_b, lam_vecs, subln_row)
    return o_a, o_b


def _mix_ffn_kernel(x_ref, oa_ref, ob_ref, ga_ref, gb_ref, mod_ref, wba_ref, wbd_ref, wout_ref,
                    nw_ref, wg_ref, wu_ref, wd_ref, o_ref, *, tf):
    y_a = _dot(oa_ref[...], wba_ref[...])
    y_b = _dot(ob_ref[...], wbd_ref[...])
    mixed = _dot((ga_ref[...] * y_a + gb_ref[...] * y_b).astype(BF16), wout_ref[...])
    x = x_ref[...] + mod_ref[5] * mixed
    o_ref[...] = _ffn_update(x, mod_ref[6], mod_ref[7], mod_ref[8], nw_ref[...],
                             wg_ref, wu_ref, wd_ref, tf)


def _mix_ffn(x, o_a, o_b, g_a, g_b, mod, w_ba, w_bd, w_out, nw, wg, wu, wd, tm):
    b, s, d = x.shape
    r = mod.shape[2]
    d_ff = wg.shape[1]
    tf = _pick_ff_tile(d_ff)
    row_spec = lambda w: pl.BlockSpec((None, tm, w), lambda i, t: (i, t, 0))
    return pl.pallas_call(
        functools.partial(_mix_ffn_kernel, tf=tf),
        grid=(b, s // tm),
        in_specs=[row_spec(d), row_spec(WIDTH), row_spec(WIDTH), row_spec(d), row_spec(d),
                  pl.BlockSpec((None, N_MOD, r, d), lambda i, t: (i, 0, 0, 0)),
                  _const_spec((WIDTH, d)), _const_spec((WIDTH, d)), _const_spec((d, d)),
                  _const_spec((1, d)),
                  _const_spec((d, d_ff)), _const_spec((d, d_ff)), _const_spec((d_ff, d))],
        out_specs=row_spec(d),
        out_shape=jax.ShapeDtypeStruct((b, s, d), F32),
        compiler_params=_params("arbitrary", "arbitrary"),
        name="mix_ffn",
    )(x, o_a, o_b, g_a, g_b, mod, w_ba, w_bd, w_out, nw, wg, wu, wd)


def _moba_gate_kernel(pt_ref, q_ref, *rest):
    del pt_ref
    k_refs, gate_ref = rest[:PAGES_PER_STEP], rest[PAGES_PER_STEP]
    q = _bf16_round(q_ref[...])
    for bi in range(PAGES_PER_STEP // PAGES_PER_BLOCK):
        pages = [k_refs[bi * PAGES_PER_BLOCK + i][...] for i in range(PAGES_PER_BLOCK)]
        ksum = jnp.sum(functools.reduce(jnp.add, pages), axis=-1, keepdims=True)
        gate_ref[bi] = jnp.sum(_bf16_round(ksum * (1.0 / MOBA_BLOCK)) * q, axis=1)


def _moba_select_kernel(gate_ref, sel_ref):
    gate = gate_ref[...]
    nblk = gate.shape[1]
    blk_i = lax.broadcasted_iota(jnp.int32, gate.shape, 1)
    rank = jnp.zeros(gate.shape, jnp.int32)
    for m in range(nblk):
        gm = gate[:, m:m + 1]
        beats = (gm > gate) | ((gm == gate) & (m < blk_i))
        rank = rank + beats.astype(jnp.int32)
    for j in range(MOBA_TOPK):
        sel_ref[:, j] = jnp.sum(jnp.where(rank == j, blk_i, 0), axis=1)


def _moba_attend_kernel(pt_ref, sel_ref, q_ref, kn_ref, vn_ref, *rest, past_len):
    del pt_ref
    nt = N_HEADS_A * MOBA_TOPK * PAGES_PER_BLOCK
    k_refs, v_refs, o_ref = rest[:nt], rest[nt:2 * nt], rest[2 * nt]
    s = pl.program_id(0)
    lane = lax.broadcasted_iota(jnp.int32, (1, PAGE_SIZE), 1).astype(F32)
    slopes = _slopes(N_HEADS_A)
    for h in range(N_HEADS_A):
        q_h = q_ref[h]
        tiles = range(h * MOBA_TOPK * PAGES_PER_BLOCK, (h + 1) * MOBA_TOPK * PAGES_PER_BLOCK)
        rows = []
        for j in range(MOBA_TOPK):
            blk = sel_ref[s, j * N_HEADS_A + h]
            for i in range(PAGES_PER_BLOCK):
                kt = k_refs[tiles[j * PAGES_PER_BLOCK + i]][...]
                dist0 = (past_len - blk * MOBA_BLOCK - i * PAGE_SIZE).astype(F32)
                rows.append(jnp.sum(kt * q_h, axis=0, keepdims=True) - slopes[h] * (dist0 - lane))
        s_self = jnp.sum(q_h * kn_ref[h], axis=0, keepdims=True)
        m = s_self
        for r in rows:
            m = jnp.maximum(m, jnp.max(r, axis=1, keepdims=True))
        w_self = jnp.exp(s_self - m)
        l = w_self
        acc = jnp.zeros((HEAD_DIM_A, PAGE_SIZE), F32)
        for t, r in zip(tiles, rows):
            p = jnp.exp(r - m)
            l = l + jnp.sum(p, axis=1, keepdims=True)
            acc = acc + p * v_refs[t][...]
        o = jnp.sum(acc, axis=1, keepdims=True) + w_self * vn_ref[h]
        o_ref[h] = o * (1.0 / l)


def _moba_decode(q, k_new, v_new, k_pool, v_pool, page_table, layer):
    db, n_pages = page_table.shape
    nblk = n_pages // PAGES_PER_BLOCK
    assert nblk >= MOBA_TOPK
    bps = PAGES_PER_STEP // PAGES_PER_BLOCK
    as_tiles = lambda pool: jnp.transpose(pool, (0, 1, 3, 4, 2))
    col = lambda a: a[..., None]
    kt_pool, vt_pool = as_tiles(k_pool), as_tiles(v_pool)
    tile_shape = (None, None, N_HEADS_A, HEAD_DIM_A, PAGE_SIZE)

    gate = pl.pallas_call(
        _moba_gate_kernel,
        grid_spec=pltpu.PrefetchScalarGridSpec(
            num_scalar_prefetch=1,
            grid=(db, n_pages // PAGES_PER_STEP),
            in_specs=[pl.BlockSpec((None, N_HEADS_A, HEAD_DIM_A, 1), lambda s, j, pt: (s, 0, 0, 0))]
            + [pl.BlockSpec(tile_shape, functools.partial(
                lambda s, j, pt, i: (layer, pt[s, j * PAGES_PER_STEP + i], 0, 0, 0), i=i))
               for i in range(PAGES_PER_STEP)],
            out_specs=pl.BlockSpec((None, bps, N_HEADS_A, 1), lambda s, j, pt: (s, j, 0, 0)),
        ),
        out_shape=jax.ShapeDtypeStruct((db, nblk, N_HEADS_A, 1), F32),
        compiler_params=_params("arbitrary", "arbitrary"),
        name="moba_gate",
    )(page_table, col(q), *([kt_pool] * PAGES_PER_STEP))

    sel = pl.pallas_call(
        _moba_select_kernel,
        grid=(1,),
        in_specs=[_const_spec(gate.shape)],
        out_specs=_const_spec((db, MOBA_TOPK, N_HEADS_A, 1)),
        out_shape=jax.ShapeDtypeStruct((db, MOBA_TOPK, N_HEADS_A, 1), jnp.int32),
        compiler_params=_params("arbitrary"),
        name="moba_select",
    )(gate).reshape(db, MOBA_TOPK * N_HEADS_A)

    def tile_spec(h, j, i):
        def index(s, pt, sel_):
            blk = jnp.clip(sel_[s, j * N_HEADS_A + h], 0, nblk - 1)
            return (layer, pt[s, PAGES_PER_BLOCK * blk + i], h, 0, 0)
        return pl.BlockSpec((None, None, None, HEAD_DIM_A, PAGE_SIZE), index)

    tile_specs = [tile_spec(h, j, i) for h in range(N_HEADS_A) for j in range(MOBA_TOPK)
                  for i in range(PAGES_PER_BLOCK)]
    seq = pl.BlockSpec((None, N_HEADS_A, HEAD_DIM_A, 1), lambda s, pt, sel_: (s, 0, 0, 0))
    o = pl.pallas_call(
        functools.partial(_moba_attend_kernel, past_len=n_pages * PAGE_SIZE),
        grid_spec=pltpu.PrefetchScalarGridSpec(
            num_scalar_prefetch=2,
            grid=(db,),
            in_specs=[seq, seq, seq] + tile_specs * 2,
            out_specs=seq,
        ),
        out_shape=jax.ShapeDtypeStruct((db, N_HEADS_A, HEAD_DIM_A, 1), F32),
        compiler_params=_params("arbitrary"),
        name="moba_attend",
    )(page_table, sel, col(q), col(k_new), col(v_new),
      *([kt_pool] * len(tile_specs)), *([vt_pool] * len(tile_specs)))
    return o[..., 0]


def _diff_decode_kernel(pt_ref, q_ref, kn_ref, vn_ref, slope_ref, tbias_ref, b01_ref, lam_ref, subln_ref,
                        *rest, past_len, lam_init):
    del pt_ref
    npg = PAGES_PER_STEP
    k_refs, v_refs = rest[:npg], rest[npg:2 * npg]
    o_ref, m_sc, l_sc, a0_sc, a1_sc = rest[2 * npg:]
    j = pl.program_id(1)
    q8 = q_ref[...]
    b01 = b01_ref[...]
    slope8 = slope_ref[...]
    nv = PAGE_SIZE * N_HEADS_B // SUBLANES
    tbias = tbias_ref[...].reshape(nv, SUBLANES, 2 * LANES)
    both = lambda x: x + pltpu.roll(x, N_HEADS_B, axis=0)

    @pl.when(j == 0)
    def _():
        m_sc[...] = _dot((q8 * kn_ref[...]).astype(BF16), b01)
        l_sc[...] = jnp.ones(l_sc.shape, F32)
        a0_sc[...] = vn_ref[...]
        a1_sc[...] = vn_ref[...]

    m_run, l_run, a0_run, a1_run = m_sc[...], l_sc[...], a0_sc[...], a1_sc[...]
    for idx in range(npg):
        k3 = k_refs[idx][...].reshape(nv, SUBLANES, LANES)
        prod = (k3 * q8[None]).reshape(nv * SUBLANES, LANES).astype(BF16)
        dist0 = (past_len - (j * npg + idx) * PAGE_SIZE).astype(F32)
        s = _dot(prod, b01).reshape(nv, SUBLANES, 2 * LANES) + tbias - (slope8 * dist0)[None]
        m_pg = jnp.max(s, axis=0)
        m_new = jnp.maximum(m_run, jnp.maximum(m_pg, pltpu.roll(m_pg, N_HEADS_B, axis=0)))
        alpha = jnp.exp(m_run - m_new)
        p = jnp.exp(s - m_new[None])
        v3 = v_refs[idx][...].reshape(nv, SUBLANES, LANES)
        l_run = alpha * l_run + both(jnp.sum(p, axis=0))
        a0_run = alpha[:, :LANES] * a0_run + both(jnp.sum(p[:, :, :LANES] * v3, axis=0))
        a1_run = alpha[:, LANES:] * a1_run + both(jnp.sum(p[:, :, LANES:] * v3, axis=0))
        m_run = m_new
    m_sc[...] = m_run
    l_sc[...] = l_run
    a0_sc[...] = a0_run
    a1_sc[...] = a1_run

    @pl.when(j == pl.num_programs(1) - 1)
    def _():
        inv = 1.0 / l_run
        o = a0_run * inv[:, :LANES] - _lambda(lam_ref, lam_init) * (a1_run * inv[:, LANES:])
        ms = jnp.mean(o * o, axis=-1, keepdims=True)
        o_ref[...] = o * lax.rsqrt(ms + RMS_EPS) * subln_ref[...] * (1.0 - lam_init)


def _diff_decode(q, k_new, v_new, k_pool, v_pool, page_table, layer, lam_vecs, subln, lam_init):
    db, n_pages = page_table.shape
    depth, n_phys = k_pool.shape[:2]
    rows = PAGE_SIZE * N_HEADS_B
    pages = lambda pool: pool.reshape(depth * n_phys, rows, V_DIM_B)
    twice = lambda a: jnp.concatenate([a, a], axis=1)
    slopes = np.asarray(_slopes(N_HEADS_B), np.float32)
    slope_np = np.repeat(np.tile(slopes, 2)[:, None], 2 * LANES, axis=1)
    r = np.arange(rows)
    tbias_np = np.repeat((slopes[r % N_HEADS_B] * (r // N_HEADS_B))[:, None], 2 * LANES, axis=1)
    half = np.arange(LANES) // HEAD_DIM_B
    b01_np = (half[:, None] == (np.arange(2 * LANES) // LANES)[None, :]).astype(np.float32)
    seq = pl.BlockSpec((None, SUBLANES, V_DIM_B), lambda s, j, pt: (s, 0, 0))
    cst = lambda shp: pl.BlockSpec(shp, lambda s, j, pt: (0, 0))
    page = lambda i: pl.BlockSpec(
        (None, rows, V_DIM_B), lambda s, j, pt: (layer * n_phys + pt[s, j * PAGES_PER_STEP + i], 0, 0))
    page_specs = [page(i) for i in range(PAGES_PER_STEP)]
    grid_spec = pltpu.PrefetchScalarGridSpec(
        num_scalar_prefetch=1,
        grid=(db, n_pages // PAGES_PER_STEP),
        in_specs=[seq, seq, seq, cst((SUBLANES, 2 * LANES)), cst((rows, 2 * LANES)), cst((LANES, 2 * LANES)),
                  cst((4, HEAD_DIM_B)), cst((1, V_DIM_B))]
        + page_specs * 2,
        out_specs=seq,
        scratch_shapes=[pltpu.VMEM((SUBLANES, 2 * LANES), F32), pltpu.VMEM((SUBLANES, 2 * LANES), F32),
                        pltpu.VMEM((SUBLANES, LANES), F32), pltpu.VMEM((SUBLANES, LANES), F32)],
    )
    o = pl.pallas_call(
        functools.partial(_diff_decode_kernel, past_len=n_pages * PAGE_SIZE, lam_init=lam_init),
        grid_spec=grid_spec,
        out_shape=jax.ShapeDtypeStruct((db, SUBLANES, V_DIM_B), F32),
        compiler_params=_params("arbitrary", "arbitrary"),
        name="diff_decode",
    )(page_table, twice(q), twice(k_new), twice(v_new), jnp.asarray(slope_np), jnp.asarray(tbias_np),
      jnp.asarray(b01_np, BF16), lam_vecs, subln,
      *([pages(k_pool)] * PAGES_PER_STEP), *([pages(v_pool)] * PAGES_PER_STEP))
    return o[:, :N_HEADS_B]


def _block_diag_ones():
    i = np.arange(MXU_DIM) // 64
    return jnp.asarray((i[:, None] == i[None, :]).astype(np.float32), BF16)


def kernel(x_prompt, x_sample, cache_k_moba, cache_v_moba, cache_k_diff, cache_v_diff, page_table, c_prompt, c_sample, w_ada, b_ada, norm_ffn1, ffn1_w_gate, ffn1_w_up, ffn1_w_down, norm_mix, w_in, qn_moba, kn_moba, qn_diff, kn_diff, lambda_q1, lambda_k1, lambda_q2, lambda_k2, subln_diff, w_branch_moba, w_branch_diff, w_out, norm_ffn2, ffn2_w_gate, ffn2_w_up, ffn2_w_down):
    depth = w_ada.shape[0]
    b, s, d = x_prompt.shape
    db, t_new, _ = x_sample.shape
    assert t_new == 1 and s % TOKEN_TILE == 0 and db % SUBLANES == 0
    n_pages = page_table.shape[1]
    assert n_pages % PAGES_PER_STEP == 0
    bd = _block_diag_ones()
    tile8 = lambda v: jnp.tile(v, WIDTH // v.shape[0]).reshape(1, WIDTH)

    y_p, y_s = x_prompt, x_sample.reshape(1, db, d)
    rows_p, rows_s = [], []
    for l in range(depth):
        lam_init = _lambda_init(l)
        bf = lambda w: w[l].astype(BF16)
        row = lambda v: v[l].reshape(1, -1)
        mod = _ada(jnp.concatenate([c_prompt, c_sample], axis=0), bf(w_ada), b_ada[l])
        mod_p = mod[:b].reshape(b, N_MOD, 1, d)
        mod_s = mod[b:].reshape(db, N_MOD, d).transpose(1, 0, 2).reshape(1, N_MOD, db, d)
        ffn1 = (row(norm_ffn1), bf(ffn1_w_gate), bf(ffn1_w_up), bf(ffn1_w_down))
        ffn2 = (row(norm_ffn2), bf(ffn2_w_gate), bf(ffn2_w_up), bf(ffn2_w_down))
        w_in_bf = bf(w_in)
        gains = jnp.concatenate([tile8(qn_moba[l]), tile8(kn_moba[l]), tile8(qn_diff[l]), tile8(kn_diff[l])], 0)
        lam_vecs = jnp.stack([lambda_q1[l], lambda_k1[l], lambda_q2[l], lambda_k2[l]])
        mix_w = (bf(w_branch_moba), bf(w_branch_diff), bf(w_out))

        x1 = _ffn(y_p, mod_p, *ffn1, k0=0, tm=TOKEN_TILE)
        (kt_a, vt32_a, k_b, v_b, kbf_a, kbf_b, qt_a, qt_b, vt_a, vt_b, km_a, g_a, g_b) = _proj(
            x1, mod_p, row(norm_mix), w_in_bf, gains, bd, TOKEN_TILE, True)
        o_a, o_b = _prompt_attention(qt_a, kbf_a, vt_a, km_a.reshape(b, -1, WIDTH), qt_b, kbf_b, vt_b,
                                     lam_vecs, tile8(subln_diff[l]), lam_init)
        y_p = _mix_ffn(x1, o_a, o_b, g_a, g_b, mod_p, *mix_w, *ffn2, tm=TOKEN_TILE)
        token_major = lambda a: a.reshape(b, N_HEADS_A, HEAD_DIM_A, s).transpose(0, 3, 1, 2)
        rows_p.append((token_major(kt_a), token_major(vt32_a), k_b, v_b))

        x1s = _ffn(y_s, mod_s, *ffn1, k0=0, tm=db)
        (q_as, k_as, v_as, q_bs, k_bs, v_bs, g_as, g_bs) = _proj(
            x1s, mod_s, row(norm_mix), w_in_bf, gains, bd, db, False)
        heads_a = lambda a: a.reshape(db, N_HEADS_A, HEAD_DIM_A)
        heads_b = lambda a: a.reshape(db, N_HEADS_B, V_DIM_B)
        o_as = _moba_decode(heads_a(q_as), heads_a(k_as), heads_a(v_as), cache_k_moba, cache_v_moba,
                            page_table, l)
        o_bs = _diff_decode(heads_b(q_bs), heads_b(k_bs), heads_b(v_bs), cache_k_diff, cache_v_diff,
                            page_table, l, lam_vecs, subln_diff[l].reshape(1, V_DIM_B), lam_init)
        as_rows = lambda a: a.reshape(1, db, WIDTH).astype(BF16)
        y_s = _mix_ffn(x1s, as_rows(o_as), as_rows(o_bs), g_as, g_bs, mod_s, *mix_w, *ffn2, tm=db)
        rows_s.append((k_as, v_as, k_bs, v_bs))

    def stack(rows, i, lead, heads, hd):
        return jnp.stack([r[i].reshape(lead + (heads, hd)) for r in rows])

    lp, ls = (b, s), (db, 1)
    return (y_p, y_s.reshape(db, 1, d),
            stack(rows_p, 0, lp, N_HEADS_A, HEAD_DIM_A), stack(rows_p, 1, lp, N_HEADS_A, HEAD_DIM_A),
            stack(rows_p, 2, lp, N_HEADS_B, V_DIM_B), stack(rows_p, 3, lp, N_HEADS_B, V_DIM_B),
            stack(rows_s, 0, ls, N_HEADS_A, HEAD_DIM_A), stack(rows_s, 1, ls, N_HEADS_A, HEAD_DIM_A),
            stack(rows_s, 2, ls, N_HEADS_B, V_DIM_B), stack(rows_s, 3, ls, N_HEADS_B, V_DIM_B))
```

```python
import functools
import math

import jax
import jax.numpy as jnp
import numpy as np
from jax import lax
from jax.experimental import pallas as pl
from jax.experimental.pallas import tpu as pltpu

F32 = jnp.float32
BF16 = jnp.bfloat16

N_HEADS_A = 8
HEAD_DIM_A = 64
MOBA_BLOCK = 256
MOBA_TOPK = 3
N_HEADS_B = 4
HEAD_DIM_B = 64
V_DIM_B = 2 * HEAD_DIM_B
WIDTH = 512
N_GROUPS = WIDTH // 64
PAGE_SIZE = 128
PAGES_PER_BLOCK = MOBA_BLOCK // PAGE_SIZE
N_MOD = 9
RMS_EPS = 1e-6
QK_SCALE = 0.125
NEG = -1e30

LANES = 128
SUBLANES = 8
MXU_DIM = 256
VMEM_LIMIT_BYTES = 56 * 1024 * 1024

TOKEN_TILE = 512
PAGES_PER_STEP = 16
GATE_PAGES_PER_STEP = 32


def _slopes(n_heads):
    return [2.0 ** (-8.0 * (i + 1) / n_heads) for i in range(n_heads)]


def _lambda_init(layer):
    return 0.8 - 0.6 * math.exp(-0.3 * layer)


def _dot(a, b):
    return jnp.dot(a, b, preferred_element_type=F32)


def _bf16_round(x):
    return x.astype(BF16).astype(F32)


def _rms(x, w):
    ms = jnp.mean(x * x, axis=-1, keepdims=True)
    return x * lax.rsqrt(ms + RMS_EPS) * w


def _pick_ff_tile(d_ff):
    best = LANES
    for t in range(LANES, min(d_ff, 1408) + 1, LANES):
        if d_ff % t == 0:
            best = t
    return best


def _params(*sem):
    return pltpu.CompilerParams(dimension_semantics=sem, vmem_limit_bytes=VMEM_LIMIT_BYTES)


def _const_spec(shape):
    nd = len(shape)
    return pl.BlockSpec(shape, lambda *_: (0,) * nd)


def _ada_kernel(c_ref, w_ref, b_ref, o_ref):
    c = c_ref[...]
    s = c * jax.nn.sigmoid(c)
    o_ref[...] = _dot(s.astype(BF16), w_ref[...]) + b_ref[...]


def _ada(c, w_bf, b):
    m, d = c.shape
    n = w_bf.shape[1]
    tn = 1024 if n % 1024 == 0 else n
    return pl.pallas_call(
        _ada_kernel,
        grid=(n // tn,),
        in_specs=[pl.BlockSpec((m, d), lambda j: (0, 0)),
                  pl.BlockSpec((d, tn), lambda j: (0, j)),
                  pl.BlockSpec((1, tn), lambda j: (0, j))],
        out_specs=pl.BlockSpec((m, tn), lambda j: (0, j)),
        out_shape=jax.ShapeDtypeStruct((m, n), F32),
        compiler_params=_params("arbitrary"),
        name="ada_mod",
    )(c, w_bf, b.reshape(1, n))


def _ffn_update(x, shift, scale, gate, nw, wg_ref, wu_ref, wd_ref, tf):
    h = (_rms(x, nw) * (1.0 + scale) + shift).astype(BF16)
    d_ff = wg_ref.shape[1]
    acc = jnp.zeros(x.shape, F32)
    for j in range(d_ff // tf):
        g = _dot(h, wg_ref[:, j * tf:(j + 1) * tf])
        u = _dot(h, wu_ref[:, j * tf:(j + 1) * tf])
        a = (g * jax.nn.sigmoid(g) * u).astype(BF16)
        acc = acc + _dot(a, wd_ref[j * tf:(j + 1) * tf, :])
    return x + 0.5 * gate * acc


def _ffn_kernel(x_ref, mod_ref, nw_ref, wg_ref, wu_ref, wd_ref, o_ref, *, k0, tf):
    o_ref[...] = _ffn_update(x_ref[...], mod_ref[k0], mod_ref[k0 + 1], mod_ref[k0 + 2],
                             nw_ref[...], wg_ref, wu_ref, wd_ref, tf)


def _ffn(x, mod, nw, wg, wu, wd, k0, tm):
    b, s, d = x.shape
    r = mod.shape[2]
    d_ff = wg.shape[1]
    tf = _pick_ff_tile(d_ff)
    return pl.pallas_call(
        functools.partial(_ffn_kernel, k0=k0, tf=tf),
        grid=(b, s // tm),
        in_specs=[pl.BlockSpec((None, tm, d), lambda i, t: (i, t, 0)),
                  pl.BlockSpec((None, N_MOD, r, d), lambda i, t: (i, 0, 0, 0)),
                  _const_spec((1, d)),
                  _const_spec((d, d_ff)), _const_spec((d, d_ff)), _const_spec((d_ff, d))],
        out_specs=pl.BlockSpec((None, tm, d), lambda i, t: (i, t, 0)),
        out_shape=jax.ShapeDtypeStruct((b, s, d), F32),
        compiler_params=_params("arbitrary", "arbitrary"),
        name="ffn",
    )(x, mod, nw, wg, wu, wd)


def _head_norm(seg, gain_row, bd):
    sq = (seg * seg).astype(BF16)
    parts = [_dot(sq[:, c * MXU_DIM:(c + 1) * MXU_DIM], bd) for c in range(WIDTH // MXU_DIM)]
    ms = jnp.concatenate(parts, axis=1) * (1.0 / 64.0)
    return seg * lax.rsqrt(ms + RMS_EPS) * gain_row


def _proj_kernel(x_ref, mod_ref, nw_ref, win_ref, gains_ref, bd_ref, *outs, transposed):
    x = x_ref[...]
    d = x.shape[1]
    h = (_rms(x, nw_ref[...]) * (1.0 + mod_ref[4]) + mod_ref[3]).astype(BF16)
    bd = bd_ref[...]

    def seg(j):
        return _dot(h, win_ref[:, j * WIDTH:(j + 1) * WIDTH])

    q_a = _head_norm(seg(0), gains_ref[0:1, :], bd) * QK_SCALE
    k_a = _head_norm(seg(1), gains_ref[1:2, :], bd)
    v_a = seg(2)
    q_b = _head_norm(seg(3), gains_ref[2:3, :], bd) * QK_SCALE
    k_b = _head_norm(seg(4), gains_ref[3:4, :], bd)
    v_b = seg(5)
    g0 = 6 * WIDTH
    g_a = jax.nn.sigmoid(_dot(h, win_ref[:, g0:g0 + d]))
    g_b = jax.nn.sigmoid(_dot(h, win_ref[:, g0 + d:g0 + 2 * d]))

    if not transposed:
        (qa_ref, ka_ref, va_ref, qb_ref, kb_ref, vb_ref, ga_ref, gb_ref) = outs
        qa_ref[...] = q_a
        ka_ref[...] = k_a
        va_ref[...] = v_a
        qb_ref[...] = q_b
        kb_ref[...] = k_b
        vb_ref[...] = v_b
    else:
        (kta_ref, vta32_ref, kb_ref, vb_ref, kabf_ref, kbbf_ref, qta_ref, qtb_ref,
         vta_ref, vtb_ref, km_ref, ga_ref, gb_ref) = outs
        kta_ref[...] = k_a.T
        vt_a = v_a.T
        vta32_ref[...] = vt_a
        qt_a, qt_b, vt_b = q_a.T, q_b.T, v_b.T
        for r in range(x.shape[0] // MOBA_BLOCK):
            rows = slice(r * MOBA_BLOCK, (r + 1) * MOBA_BLOCK)
            kabf_ref[r] = k_a[rows].astype(BF16)
            kbbf_ref[r] = k_b[rows].astype(BF16)
            qta_ref[r] = qt_a[:, rows].astype(BF16)
            qtb_ref[r] = qt_b[:, rows].astype(BF16)
            vta_ref[r] = vt_a[:, rows].astype(BF16)
            vtb_ref[r] = vt_b[:, rows].astype(BF16)
            km_ref[r] = jnp.sum(k_a[rows], axis=0, keepdims=True) * (1.0 / MOBA_BLOCK)
        for hd in range(N_HEADS_B):
            lanes = slice(V_DIM_B * hd, V_DIM_B * (hd + 1))
            kb_ref[pl.ds(hd, x.shape[0], stride=N_HEADS_B), :] = k_b[:, lanes]
            vb_ref[pl.ds(hd, x.shape[0], stride=N_HEADS_B), :] = v_b[:, lanes]
    ga_ref[...] = g_a
    gb_ref[...] = g_b


def _proj(x, mod, nw, w_in, gains, bd, tm, transposed):
    b, s, d = x.shape
    r = mod.shape[2]
    d_in = w_in.shape[1]
    row_spec = lambda w: pl.BlockSpec((None, tm, w), lambda i, t: (i, t, 0))
    rows = lambda w, dt=F32: jax.ShapeDtypeStruct((b, s, w), dt)
    if transposed:
        nb, bpt = s // MOBA_BLOCK, tm // MOBA_BLOCK
        blk = lambda shp: pl.BlockSpec((None, bpt) + shp, lambda i, t: (i, t, 0, 0))
        col_spec = pl.BlockSpec((None, WIDTH, tm), lambda i, t: (i, 0, t))
        head_rows = pl.BlockSpec((None, tm * N_HEADS_B, V_DIM_B), lambda i, t: (i, t, 0))
        out_specs = [col_spec] * 2 + [head_rows] * 2 + [blk((MOBA_BLOCK, WIDTH))] * 2 \
            + [blk((WIDTH, MOBA_BLOCK))] * 4 + [blk((1, WIDTH))] + [row_spec(d)] * 2
        out_shape = [jax.ShapeDtypeStruct((b, WIDTH, s), F32)] * 2 \
            + [jax.ShapeDtypeStruct((b, s * N_HEADS_B, V_DIM_B), F32)] * 2 \
            + [jax.ShapeDtypeStruct((b, nb, MOBA_BLOCK, WIDTH), BF16)] * 2 \
            + [jax.ShapeDtypeStruct((b, nb, WIDTH, MOBA_BLOCK), BF16)] * 4 \
            + [jax.ShapeDtypeStruct((b, nb, 1, WIDTH), F32)] + [rows(d)] * 2
    else:
        out_specs = [row_spec(WIDTH)] * 6 + [row_spec(d)] * 2
        out_shape = [rows(WIDTH)] * 6 + [rows(d)] * 2
    return pl.pallas_call(
        functools.partial(_proj_kernel, transposed=transposed),
        grid=(b, s // tm),
        in_specs=[pl.BlockSpec((None, tm, d), lambda i, t: (i, t, 0)),
                  pl.BlockSpec((None, N_MOD, r, d), lambda i, t: (i, 0, 0, 0)),
                  _const_spec((1, d)), _const_spec((d, d_in)),
                  _const_spec((4, WIDTH)), _const_spec((MXU_DIM, MXU_DIM))],
        out_specs=out_specs,
        out_shape=out_shape,
        compiler_params=_params("arbitrary", "arbitrary"),
        name="mixer_proj",
    )(x, mod, nw, w_in, gains, bd)


FEATURE_ROWS = SUBLANES


def _key_features(n):
    lane = lax.broadcasted_iota(jnp.int32, (MOBA_BLOCK, LANES), 1)
    key = lax.broadcasted_iota(jnp.int32, (MOBA_BLOCK, LANES), 0).astype(F32)
    f = jnp.where(lane == 0, 1.0, jnp.where(lane == 1, key, jnp.where(lane == FEATURE_ROWS + n, 1.0, 0.0)))
    return f.astype(BF16)


def _query_features(slope, block_bias):
    qry = lax.broadcasted_iota(jnp.int32, (FEATURE_ROWS, MOBA_BLOCK), 1).astype(F32)
    row = lax.broadcasted_iota(jnp.int32, (FEATURE_ROWS, MOBA_BLOCK), 0)
    head = jnp.where(row == 0, -slope * qry, jnp.where(row == 1, slope, 0.0))
    pad = jnp.zeros((LANES - FEATURE_ROWS - block_bias.shape[0], MOBA_BLOCK), F32)
    return jnp.concatenate([head, block_bias, pad], axis=0).astype(BF16)


def _augmented_query_pairs(qms, slopes, block_biases):
    q_aug = [jnp.concatenate([qm, _query_features(s, bb)], axis=0)
             for qm, s, bb in zip(qms, slopes, block_biases)]
    return [jnp.concatenate(q_aug[2 * p:2 * p + 2], axis=1) for p in range(N_GROUPS // 2)]


def _pair_scores(kb, kfeat, q_pairs, p):
    k_aug = jnp.concatenate([kb[:, LANES * p:LANES * (p + 1)], kfeat], axis=1)
    return _dot(k_aug, q_pairs[p])


def _attend_block(kb, vtb, kfeat, q_pairs, causal, m_all, l_all, acc_sc, v_rows, st_first, s0_sc, k_next):
    m_out, l_out = [], []
    n_pairs = len(q_pairs)
    st_next = _pair_scores(kb, kfeat, q_pairs, 0) if st_first is None else st_first
    for p in range(n_pairs):
        st_pair = st_next
        if p + 1 < n_pairs:
            st_next = _pair_scores(kb, kfeat, q_pairs, p + 1)
        else:
            s0_sc[...] = _pair_scores(*k_next, q_pairs, 0)
        for half in range(2):
            g = 2 * p + half
            st = st_pair[:, MOBA_BLOCK * half:MOBA_BLOCK * (half + 1)]
            if causal is not None:
                st = jnp.where(causal, st, NEG)
            m_old = m_all[g:g + 1, :]
            m_new = jnp.maximum(m_old, jnp.max(st, axis=0, keepdims=True))
            alpha = jnp.exp(m_old - m_new)
            pt = jnp.exp(st - m_new)
            l_out.append(alpha * l_all[g:g + 1, :] + jnp.sum(pt, axis=0, keepdims=True))
            rows = v_rows(g)
            nr = rows.stop - rows.start
            acc_rows = slice(g * nr, (g + 1) * nr)
            acc_sc[acc_rows, :] = alpha * acc_sc[acc_rows, :] + _dot(vtb[rows, :], pt.astype(BF16))
            m_out.append(m_new)
    return jnp.concatenate(m_out, axis=0), jnp.concatenate(l_out, axis=0)


def _attend_all_blocks(qi, k_ref, vt_ref, q_pairs, acc_sc, s0_sc, v_rows):
    key_i = lax.broadcasted_iota(jnp.int32, (MOBA_BLOCK, MOBA_BLOCK), 0)
    qry_i = lax.broadcasted_iota(jnp.int32, (MOBA_BLOCK, MOBA_BLOCK), 1)
    acc_sc[...] = jnp.zeros(acc_sc.shape, F32)
    m0 = jnp.full((N_GROUPS, MOBA_BLOCK), NEG, F32)
    l0 = jnp.zeros((N_GROUPS, MOBA_BLOCK), F32)
    m1, l1 = _attend_block(k_ref[qi], vt_ref[qi], _key_features(qi), q_pairs, key_i <= qry_i,
                           m0, l0, acc_sc, v_rows, None, s0_sc, (k_ref[0], _key_features(0)))

    def past(n, carry):
        return _attend_block(k_ref[n], vt_ref[n], _key_features(n), q_pairs, None, *carry, acc_sc, v_rows,
                             s0_sc[...], s0_sc, (k_ref[n + 1], _key_features(n + 1)))

    _, l_fin = lax.fori_loop(0, qi, past, (m1, l1))
    return l_fin


def _masked_queries(qt):
    row = lax.broadcasted_iota(jnp.int32, (LANES, MOBA_BLOCK), 0)
    out = []
    for g in range(N_GROUPS):
        p, half = divmod(g, 2)
        pair = qt[LANES * p:LANES * (p + 1), :]
        keep = (row >= 64 * half) & (row < 64 * (half + 1))
        out.append(jnp.where(keep, pair, jnp.zeros_like(pair)))
    return out


def _block_distance(qi, nbp):
    blk_i = lax.broadcasted_iota(jnp.int32, (nbp, MOBA_BLOCK), 0)
    return blk_i, ((qi - blk_i) * MOBA_BLOCK).astype(F32)


def _moba_prompt_kernel(qt_ref, k_ref, vt_ref, km_ref, o_ref, acc_sc, s0_sc):
    qi = pl.program_id(1)
    nb = k_ref.shape[0]
    nbp = -(-nb // SUBLANES) * SUBLANES
    blk = MOBA_BLOCK
    slopes = _slopes(N_HEADS_A)
    qms = _masked_queries(qt_ref[...])

    km = km_ref[...].astype(BF16)
    blk_i, blk_dist = _block_distance(qi, nbp)
    valid = blk_i[:nb] < qi
    biases = []
    for g in range(N_GROUPS):
        lanes = slice(LANES * (g // 2), LANES * (g // 2 + 1))
        gate = _dot(km[:, lanes], qms[g])
        gate = jnp.where(valid, gate, NEG)
        rank = jnp.zeros((nb, blk), jnp.int32)
        for m in range(nb):
            gm = gate[m:m + 1, :]
            beats = (gm > gate) | ((gm == gate) & (m < blk_i[:nb]))
            rank = rank + beats.astype(jnp.int32)
        drop = jnp.where(valid & (rank < MOBA_TOPK), 0.0, NEG)
        if nbp > nb:
            drop = jnp.concatenate([drop, jnp.zeros((nbp - nb, blk), F32)], axis=0)
        biases.append(jnp.where(blk_i == qi, 0.0, drop - slopes[g] * blk_dist))

    q_pairs = _augmented_query_pairs(qms, slopes, biases)
    l_fin = _attend_all_blocks(qi, k_ref, vt_ref, q_pairs, acc_sc, s0_sc,
                               lambda g: slice(64 * g, 64 * (g + 1)))
    parts = [acc_sc[64 * g:64 * (g + 1), :] * (1.0 / l_fin[g:g + 1, :]) for g in range(N_GROUPS)]
    o_ref[...] = jnp.concatenate(parts, axis=0).T.astype(BF16)


def _lambda(lam_ref, lam_init):
    a = jnp.sum(lam_ref[0:1, :] * lam_ref[1:2, :], axis=-1, keepdims=True)
    b = jnp.sum(lam_ref[2:3, :] * lam_ref[3:4, :], axis=-1, keepdims=True)
    return jnp.exp(a) - jnp.exp(b) + lam_init


def _diff_prompt_kernel(qt_ref, k_ref, vt_ref, lam_ref, subln_ref, o_ref, acc_sc, s0_sc, *, lam_init):
    qi = pl.program_id(1)
    nb = k_ref.shape[0]
    nbp = -(-nb // SUBLANES) * SUBLANES
    slopes = [s for s in _slopes(N_HEADS_B) for _ in range(2)]
    qms = _masked_queries(qt_ref[...])
    _, blk_dist = _block_distance(qi, nbp)
    q_pairs = _augmented_query_pairs(qms, slopes, [-s * blk_dist for s in slopes])
    v_rows = lambda g: slice(V_DIM_B * (g // 2), V_DIM_B * (g // 2 + 1))
    l_fin = _attend_all_blocks(qi, k_ref, vt_ref, q_pairs, acc_sc, s0_sc, v_rows)

    lam = _lambda(lam_ref, lam_init)
    parts = []
    for h in range(N_HEADS_B):
        o0 = acc_sc[V_DIM_B * (2 * h):V_DIM_B * (2 * h + 1), :] * (1.0 / l_fin[2 * h:2 * h + 1, :])
        o1 = acc_sc[V_DIM_B * (2 * h + 1):V_DIM_B * (2 * h + 2), :] * (1.0 / l_fin[2 * h + 1:2 * h + 2, :])
        o = o0 - lam * o1
        ms = jnp.mean(o * o, axis=0, keepdims=True)
        parts.append(o * lax.rsqrt(ms + RMS_EPS))
    o_t = jnp.concatenate(parts, axis=0).T
    o_ref[...] = (o_t * subln_ref[...] * (1.0 - lam_init)).astype(BF16)


def _prompt_attention(qt_a, kbf_a, vt_a, km_a, qt_b, kbf_b, vt_b, lam_vecs, subln_row, lam_init):
    b, nb = qt_a.shape[:2]
    blk = MOBA_BLOCK
    s = nb * blk
    q_spec = pl.BlockSpec((None, None, WIDTH, blk), lambda i, t: (i, t, 0, 0))
    k_spec = pl.BlockSpec((None, nb, blk, WIDTH), lambda i, t: (i, 0, 0, 0))
    vt_spec = pl.BlockSpec((None, nb, WIDTH, blk), lambda i, t: (i, 0, 0, 0))
    o_spec = pl.BlockSpec((None, blk, WIDTH), lambda i, t: (i, t, 0))
    o_shape = jax.ShapeDtypeStruct((b, s, WIDTH), BF16)
    o_a = pl.pallas_call(
        _moba_prompt_kernel,
        grid=(b, nb),
        in_specs=[q_spec, k_spec, vt_spec, pl.BlockSpec((None, nb, WIDTH), lambda i, t: (i, 0, 0))],
        out_specs=o_spec,
        out_shape=o_shape,
        scratch_shapes=[pltpu.VMEM((WIDTH, blk), F32), pltpu.VMEM((blk, 2 * blk), F32)],
        compiler_params=_params("arbitrary", "arbitrary"),
        name="moba_prompt",
    )(qt_a, kbf_a, vt_a, km_a)
    o_b = pl.pallas_call(
        functools.partial(_diff_prompt_kernel, lam_init=lam_init),
        grid=(b, nb),
        in_specs=[q_spec, k_spec, vt_spec, _const_spec((4, HEAD_DIM_B)), _const_spec((1, WIDTH))],
        out_specs=o_spec,
        out_shape=o_shape,
        scratch_shapes=[pltpu.VMEM((N_GROUPS * V_DIM_B, blk), F32), pltpu.VMEM((blk, 2 * blk), F32)],
        compiler_params=_params("arbitrary", "arbitrary"),
        name="diff_prompt",
    )(qt_b, kbf_b, vt_b, lam_vecs, subln_row)
    return o_a, o_b


def _mix_ffn_kernel(x_ref, oa_ref, ob_ref, ga_ref, gb_ref, mod_ref, wba_ref, wbd_ref, wout_ref,
                    nw_ref, wg_ref, wu_ref, wd_ref, o_ref, *, tf):
    y_a = _dot(oa_ref[...], wba_ref[...])
    y_b = _dot(ob_ref[...], wbd_ref[...])
    mixed = _dot((ga_ref[...] * y_a + gb_ref[...] * y_b).astype(BF16), wout_ref[...])
    x = x_ref[...] + mod_ref[5] * mixed
    o_ref[...] = _ffn_update(x, mod_ref[6], mod_ref[7], mod_ref[8], nw_ref[...],
                             wg_ref, wu_ref, wd_ref, tf)


def _mix_ffn(x, o_a, o_b, g_a, g_b, mod, w_ba, w_bd, w_out, nw, wg, wu, wd, tm):
    b, s, d = x.shape
    r = mod.shape[2]
    d_ff = wg.shape[1]
    tf = _pick_ff_tile(d_ff)
    row_spec = lambda w: pl.BlockSpec((None, tm, w), lambda i, t: (i, t, 0))
    return pl.pallas_call(
        functools.partial(_mix_ffn_kernel, tf=tf),
        grid=(b, s // tm),
        in_specs=[row_spec(d), row_spec(WIDTH), row_spec(WIDTH), row_spec(d), row_spec(d),
                  pl.BlockSpec((None, N_MOD, r, d), lambda i, t: (i, 0, 0, 0)),
                  _const_spec((WIDTH, d)), _const_spec((WIDTH, d)), _const_spec((d, d)),
                  _const_spec((1, d)),
                  _const_spec((d, d_ff)), _const_spec((d, d_ff)), _const_spec((d_ff, d))],
        out_specs=row_spec(d),
        out_shape=jax.ShapeDtypeStruct((b, s, d), F32),
        compiler_params=_params("arbitrary", "arbitrary"),
        name="mix_ffn",
    )(x, o_a, o_b, g_a, g_b, mod, w_ba, w_bd, w_out, nw, wg, wu, wd)


def _moba_gate_kernel(pt_ref, q_ref, *rest):
    del pt_ref
    k_refs, (gate_ref, s_ref) = rest[:GATE_PAGES_PER_STEP], rest[GATE_PAGES_PER_STEP:]
    q = q_ref[...]
    q_bf = _bf16_round(q)
    for bi in range(GATE_PAGES_PER_STEP // PAGES_PER_BLOCK):
        pages = [k_refs[bi * PAGES_PER_BLOCK + i][...] for i in range(PAGES_PER_BLOCK)]
        for i, k_page in enumerate(pages):
            s_ref[bi * PAGES_PER_BLOCK + i] = jnp.sum(k_page * q, axis=1)
        ksum = jnp.sum(functools.reduce(jnp.add, pages), axis=-1, keepdims=True)
        gate_ref[bi] = jnp.sum(_bf16_round(ksum * (1.0 / MOBA_BLOCK)) * q_bf, axis=1)


def _moba_select_kernel(gate_ref, sel_ref):
    gate = gate_ref[...]
    nblk = gate.shape[1]
    blk_i = lax.broadcasted_iota(jnp.int32, gate.shape, 1)
    rank = jnp.zeros(gate.shape, jnp.int32)
    for m in range(nblk):
        gm = gate[:, m:m + 1]
        beats = (gm > gate) | ((gm == gate) & (m < blk_i))
        rank = rank + beats.astype(jnp.int32)
    for j in range(MOBA_TOPK):
        sel_ref[:, j] = jnp.sum(jnp.where(rank == j, blk_i, 0), axis=1)


def _moba_attend_kernel(pt_ref, sel_ref, q_ref, kn_ref, vn_ref, s_ref, *rest, past_len):
    del pt_ref
    nt = N_HEADS_A * MOBA_TOPK * PAGES_PER_BLOCK
    v_refs, o_ref = rest[:nt], rest[nt]
    s = pl.program_id(0)
    lane = lax.broadcasted_iota(jnp.int32, (1, PAGE_SIZE), 1).astype(F32)
    slopes = _slopes(N_HEADS_A)
    for h in range(N_HEADS_A):
        q_h = q_ref[h]
        tiles = range(h * MOBA_TOPK * PAGES_PER_BLOCK, (h + 1) * MOBA_TOPK * PAGES_PER_BLOCK)
        rows = []
        for j in range(MOBA_TOPK):
            blk = sel_ref[s, j * N_HEADS_A + h]
            for i in range(PAGES_PER_BLOCK):
                raw = s_ref[PAGES_PER_BLOCK * blk + i, h:h + 1, :]
                dist0 = (past_len - blk * MOBA_BLOCK - i * PAGE_SIZE).astype(F32)
                rows.append(raw - slopes[h] * (dist0 - lane))
        s_self = jnp.sum(q_h * kn_ref[h], axis=0, keepdims=True)
        m = s_self
        for r in rows:
            m = jnp.maximum(m, jnp.max(r, axis=1, keepdims=True))
        w_self = jnp.exp(s_self - m)
        l = w_self
        acc = jnp.zeros((HEAD_DIM_A, PAGE_SIZE), F32)
        for t, r in zip(tiles, rows):
            p = jnp.exp(r - m)
            l = l + jnp.sum(p, axis=1, keepdims=True)
            acc = acc + p * v_refs[t][...]
        o = jnp.sum(acc, axis=1, keepdims=True) + w_self * vn_ref[h]
        o_ref[h] = o * (1.0 / l)


def _moba_decode(q, k_new, v_new, k_pool, v_pool, page_table, layer):
    db, n_pages = page_table.shape
    nblk = n_pages // PAGES_PER_BLOCK
    assert nblk >= MOBA_TOPK
    npg = GATE_PAGES_PER_STEP
    assert n_pages % npg == 0
    bps = npg // PAGES_PER_BLOCK
    as_tiles = lambda pool: jnp.transpose(pool, (0, 1, 3, 4, 2))
    col = lambda a: a[..., None]
    kt_pool, vt_pool = as_tiles(k_pool), as_tiles(v_pool)
    tile_shape = (None, None, N_HEADS_A, HEAD_DIM_A, PAGE_SIZE)

    gate, raw = pl.pallas_call(
        _moba_gate_kernel,
        grid_spec=pltpu.PrefetchScalarGridSpec(
            num_scalar_prefetch=1,
            grid=(db, n_pages // npg),
            in_specs=[pl.BlockSpec((None, N_HEADS_A, HEAD_DIM_A, 1), lambda s, j, pt: (s, 0, 0, 0))]
            + [pl.BlockSpec(tile_shape, functools.partial(
                lambda s, j, pt, i: (layer, pt[s, j * npg + i], 0, 0, 0), i=i))
               for i in range(npg)],
            out_specs=[pl.BlockSpec((None, bps, N_HEADS_A, 1), lambda s, j, pt: (s, j, 0, 0)),
                       pl.BlockSpec((None, npg, N_HEADS_A, PAGE_SIZE), lambda s, j, pt: (s, j, 0, 0))],
        ),
        out_shape=[jax.ShapeDtypeStruct((db, nblk, N_HEADS_A, 1), F32),
                   jax.ShapeDtypeStruct((db, n_pages, N_HEADS_A, PAGE_SIZE), F32)],
        compiler_params=_params("arbitrary", "arbitrary"),
        name="moba_gate",
    )(page_table, col(q), *([kt_pool] * npg))

    sel = pl.pallas_call(
        _moba_select_kernel,
        grid=(1,),
        in_specs=[_const_spec(gate.shape)],
        out_specs=_const_spec((db, MOBA_TOPK, N_HEADS_A, 1)),
        out_shape=jax.ShapeDtypeStruct((db, MOBA_TOPK, N_HEADS_A, 1), jnp.int32),
        compiler_params=_params("arbitrary"),
        name="moba_select",
    )(gate).reshape(db, MOBA_TOPK * N_HEADS_A)

    def tile_spec(h, j, i):
        def index(s, pt, sel_):
            blk = jnp.clip(sel_[s, j * N_HEADS_A + h], 0, nblk - 1)
            return (layer, pt[s, PAGES_PER_BLOCK * blk + i], h, 0, 0)
        return pl.BlockSpec((None, None, None, HEAD_DIM_A, PAGE_SIZE), index)

    tile_specs = [tile_spec(h, j, i) for h in range(N_HEADS_A) for j in range(MOBA_TOPK)
                  for i in range(PAGES_PER_BLOCK)]
    seq = pl.BlockSpec((None, N_HEADS_A, HEAD_DIM_A, 1), lambda s, pt, sel_: (s, 0, 0, 0))
    o = pl.pallas_call(
        functools.partial(_moba_attend_kernel, past_len=n_pages * PAGE_SIZE),
        grid_spec=pltpu.PrefetchScalarGridSpec(
            num_scalar_prefetch=2,
            grid=(db,),
            in_specs=[seq, seq, seq,
                      pl.BlockSpec((None, n_pages, N_HEADS_A, PAGE_SIZE), lambda s, pt, sel_: (s, 0, 0, 0))]
            + tile_specs,
            out_specs=seq,
        ),
        out_shape=jax.ShapeDtypeStruct((db, N_HEADS_A, HEAD_DIM_A, 1), F32),
        compiler_params=_params("arbitrary"),
        name="moba_attend",
    )(page_table, sel, col(q), col(k_new), col(v_new), raw, *([vt_pool] * len(tile_specs)))
    return o[..., 0]


def _dot_nt(a, b):
    return lax.dot_general(a, b, (((1,), (1,)), ((), ())), preferred_element_type=F32)


def _diff_decode_kernel(pt_ref, q_ref, kn_ref, vn_ref, slope_ref, tbias_ref, lam_ref, subln_ref,
                        *rest, past_len, lam_init):
    del pt_ref
    npg = PAGES_PER_STEP
    k_refs, v_refs = rest[:npg], rest[npg:2 * npg]
    o_ref, m_sc, l_sc, acc_sc = rest[2 * npg:]
    j = pl.program_id(1)
    q8 = q_ref[...]
    slope8 = slope_ref[...]
    tbias = tbias_ref[...]
    lane_max = lambda x: jnp.max(x, axis=-1, keepdims=True)

    @pl.when(j == 0)
    def _():
        s_self = jnp.sum(_bf16_round(q8) * _bf16_round(kn_ref[...]), axis=-1, keepdims=True)
        m_sc[...] = jnp.broadcast_to(s_self, m_sc.shape)
        l_sc[...] = jnp.ones(l_sc.shape, F32)
        acc_sc[...] = vn_ref[...]

    q8b = q8.astype(BF16)
    scores = []
    for idx in range(npg):
        dist0 = (past_len - (j * npg + idx) * PAGE_SIZE).astype(F32)
        scores.append(_dot_nt(q8b, k_refs[idx][...].astype(BF16)) + tbias - slope8 * dist0)
    m_old = m_sc[...]
    m_step = lane_max(functools.reduce(jnp.maximum, scores))
    m_new = jnp.maximum(m_old, m_step)
    alpha = jnp.exp(m_old - m_new)
    m1 = m_new[:, 0:1]
    p_sum = jnp.zeros(scores[0].shape, F32)
    pv = jnp.zeros(acc_sc.shape, F32)
    for idx in range(npg):
        p = jnp.exp(scores[idx] - m1)
        p_sum = p_sum + p
        pv = pv + _dot(p.astype(BF16), v_refs[idx][...].astype(BF16))
    l_new = alpha * l_sc[...] + jnp.sum(p_sum, axis=-1, keepdims=True)
    acc_new = alpha * acc_sc[...] + pv
    m_sc[...] = m_new
    l_sc[...] = l_new
    acc_sc[...] = acc_new

    @pl.when(j == pl.num_programs(1) - 1)
    def _():
        o_c = acc_new * (1.0 / l_new)
        o = o_c - _lambda(lam_ref, lam_init) * pltpu.roll(o_c, N_HEADS_B, axis=0)
        ms = jnp.mean(o * o, axis=-1, keepdims=True)
        o_ref[...] = o * lax.rsqrt(ms + RMS_EPS) * subln_ref[...] * (1.0 - lam_init)


def _diff_decode(q, k_new, v_new, k_pool, v_pool, page_table, layer, lam_vecs, subln, lam_init):
    db, n_pages = page_table.shape
    depth, n_phys = k_pool.shape[:2]
    rows = PAGE_SIZE * N_HEADS_B
    pages = lambda pool: pool.reshape(depth * n_phys, rows, V_DIM_B)
    twice = lambda a: jnp.concatenate([a, a], axis=1)
    branch = (np.arange(V_DIM_B) // HEAD_DIM_B)[None, :] == np.arange(2)[:, None]
    q8 = jnp.concatenate([q * branch[0].astype(np.float32), q * branch[1].astype(np.float32)], axis=1)
    slopes = np.tile(np.asarray(_slopes(N_HEADS_B), np.float32), 2)
    col = np.arange(rows)
    own = (col % N_HEADS_B)[None, :] == (np.arange(SUBLANES) % N_HEADS_B)[:, None]
    tbias_np = np.where(own, slopes[:, None] * (col // N_HEADS_B)[None, :], NEG).astype(np.float32)
    seq = pl.BlockSpec((None, SUBLANES, V_DIM_B), lambda s, j, pt: (s, 0, 0))
    cst = lambda shp: pl.BlockSpec(shp, lambda s, j, pt: (0, 0))
    page = lambda i: pl.BlockSpec(
        (None, rows, V_DIM_B), lambda s, j, pt: (layer * n_phys + pt[s, j * PAGES_PER_STEP + i], 0, 0))
    page_specs = [page(i) for i in range(PAGES_PER_STEP)]
    stat = pltpu.VMEM((SUBLANES, LANES), F32)
    grid_spec = pltpu.PrefetchScalarGridSpec(
        num_scalar_prefetch=1,
        grid=(db, n_pages // PAGES_PER_STEP),
        in_specs=[seq, seq, seq, cst((SUBLANES, 1)), cst((SUBLANES, rows)), cst((4, HEAD_DIM_B)), cst((1, V_DIM_B))]
        + page_specs * 2,
        out_specs=seq,
        scratch_shapes=[stat, stat, stat],
    )
    o = pl.pallas_call(
        functools.partial(_diff_decode_kernel, past_len=n_pages * PAGE_SIZE, lam_init=lam_init),
        grid_spec=grid_spec,
        out_shape=jax.ShapeDtypeStruct((db, SUBLANES, V_DIM_B), F32),
        compiler_params=_params("arbitrary", "arbitrary"),
        name="diff_decode",
    )(page_table, q8, twice(k_new), twice(v_new), jnp.asarray(slopes[:, None]), jnp.asarray(tbias_np),
      lam_vecs, subln, *([pages(k_pool)] * PAGES_PER_STEP), *([pages(v_pool)] * PAGES_PER_STEP))
    return o[:, :N_HEADS_B]


def _block_diag_ones():
    i = np.arange(MXU_DIM) // 64
    return jnp.asarray((i[:, None] == i[None, :]).astype(np.float32), BF16)


def kernel(x_prompt, x_sample, cache_k_moba, cache_v_moba, cache_k_diff, cache_v_diff, page_table, c_prompt, c_sample, w_ada, b_ada, norm_ffn1, ffn1_w_gate, ffn1_w_up, ffn1_w_down, norm_mix, w_in, qn_moba, kn_moba, qn_diff, kn_diff, lambda_q1, lambda_k1, lambda_q2, lambda_k2, subln_diff, w_branch_moba, w_branch_diff, w_out, norm_ffn2, ffn2_w_gate, ffn2_w_up, ffn2_w_down):
    depth = w_ada.shape[0]
    b, s, d = x_prompt.shape
    db, t_new, _ = x_sample.shape
    assert t_new == 1 and s % TOKEN_TILE == 0 and db % SUBLANES == 0
    n_pages = page_table.shape[1]
    assert n_pages % PAGES_PER_STEP == 0
    bd = _block_diag_ones()
    tile8 = lambda v: jnp.tile(v, WIDTH // v.shape[0]).reshape(1, WIDTH)

    y_p, y_s = x_prompt, x_sample.reshape(1, db, d)
    rows_p, rows_s = [], []
    for l in range(depth):
        lam_init = _lambda_init(l)
        bf = lambda w: w[l].astype(BF16)
        row = lambda v: v[l].reshape(1, -1)
        mod = _ada(jnp.concatenate([c_prompt, c_sample], axis=0), bf(w_ada), b_ada[l])
        mod_p = mod[:b].reshape(b, N_MOD, 1, d)
        mod_s = mod[b:].reshape(db, N_MOD, d).transpose(1, 0, 2).reshape(1, N_MOD, db, d)
        ffn1 = (row(norm_ffn1), bf(ffn1_w_gate), bf(ffn1_w_up), bf(ffn1_w_down))
        ffn2 = (row(norm_ffn2), bf(ffn2_w_gate), bf(ffn2_w_up), bf(ffn2_w_down))
        w_in_bf = bf(w_in)
        gains = jnp.concatenate([tile8(qn_moba[l]), tile8(kn_moba[l]), tile8(qn_diff[l]), tile8(kn_diff[l])], 0)
        lam_vecs = jnp.stack([lambda_q1[l], lambda_k1[l], lambda_q2[l], lambda_k2[l]])
        mix_w = (bf(w_branch_moba), bf(w_branch_diff), bf(w_out))

        x1 = _ffn(y_p, mod_p, *ffn1, k0=0, tm=TOKEN_TILE)
        (kt_a, vt32_a, k_b, v_b, kbf_a, kbf_b, qt_a, qt_b, vt_a, vt_b, km_a, g_a, g_b) = _proj(
            x1, mod_p, row(norm_mix), w_in_bf, gains, bd, TOKEN_TILE, True)
        o_a, o_b = _prompt_attention(qt_a, kbf_a, vt_a, km_a.reshape(b, -1, WIDTH), qt_b, kbf_b, vt_b,
                                     lam_vecs, tile8(subln_diff[l]), lam_init)
        y_p = _mix_ffn(x1, o_a, o_b, g_a, g_b, mod_p, *mix_w, *ffn2, tm=TOKEN_TILE)
        token_major = lambda a: a.reshape(b, N_HEADS_A, HEAD_DIM_A, s).transpose(0, 3, 1, 2)
        rows_p.append((token_major(kt_a), token_major(vt32_a), k_b, v_b))

        x1s = _ffn(y_s, mod_s, *ffn1, k0=0, tm=db)
        (q_as, k_as, v_as, q_bs, k_bs, v_bs, g_as, g_bs) = _proj(
            x1s, mod_s, row(norm_mix), w_in_bf, gains, bd, db, False)
        heads_a = lambda a: a.reshape(db, N_HEADS_A, HEAD_DIM_A)
        heads_b = lambda a: a.reshape(db, N_HEADS_B, V_DIM_B)
        o_as = _moba_decode(heads_a(q_as), heads_a(k_as), heads_a(v_as), cache_k_moba, cache_v_moba,
                            page_table, l)
        o_bs = _diff_decode(heads_b(q_bs), heads_b(k_bs), heads_b(v_bs), cache_k_diff, cache_v_diff,
                            page_table, l, lam_vecs, subln_diff[l].reshape(1, V_DIM_B), lam_init)
        as_rows = lambda a: a.reshape(1, db, WIDTH).astype(BF16)
        y_s = _mix_ffn(x1s, as_rows(o_as), as_rows(o_bs), g_as, g_bs, mod_s, *mix_w, *ffn2, tm=db)
        rows_s.append((k_as, v_as, k_bs, v_bs))

    def stack(rows, i, lead, heads, hd):
        return jnp.stack([r[i].reshape(lead + (heads, hd)) for r in rows])

    lp, ls = (b, s), (db, 1)
    return (y_p, y_s.reshape(db, 1, d),
            stack(rows_p, 0, lp, N_HEADS_A, HEAD_DIM_A), stack(rows_p, 1, lp, N_HEADS_A, HEAD_DIM_A),
            stack(rows_p, 2, lp, N_HEADS_B, V_DIM_B), stack(rows_p, 3, lp, N_HEADS_B, V_DIM_B),
            stack(rows_s, 0, ls, N_HEADS_A, HEAD_DIM_A), stack(rows_s, 1, ls, N_HEADS_A, HEAD_DIM_A),
            stack(rows_s, 2, ls, N_HEADS_B, V_DIM_B), stack(rows_s, 3, ls, N_HEADS_B, V_DIM_B))
```

```python
import functools
import math
from typing import NamedTuple

import jax
import jax.numpy as jnp
import numpy as np
from jax import lax
from jax.experimental import pallas as pl
from jax.experimental.pallas import tpu as pltpu

F32 = jnp.float32
BF16 = jnp.bfloat16

N_HEADS_A = 8
HEAD_DIM_A = 64
MOBA_BLOCK = 256
MOBA_TOPK = 3
N_HEADS_B = 4
HEAD_DIM_B = 64
V_DIM_B = 2 * HEAD_DIM_B
WIDTH = 512
N_GROUPS = WIDTH // 64
PAGE_SIZE = 128
PAGES_PER_BLOCK = MOBA_BLOCK // PAGE_SIZE
N_MOD = 9
RMS_EPS = 1e-6
QK_SCALE = 0.125
NEG = -1e30

LANES = 128
SUBLANES = 8
MXU_DIM = 256
VMEM_LIMIT_BYTES = 56 * 1024 * 1024

TOKEN_TILE = 512
PAGES_PER_STEP = 16
GATE_PAGES_PER_STEP = 32


def _slopes(n_heads):
    return [2.0 ** (-8.0 * (i + 1) / n_heads) for i in range(n_heads)]


def _lambda_init(layer):
    return 0.8 - 0.6 * math.exp(-0.3 * layer)


def _dot(a, b):
    return jnp.dot(a, b, preferred_element_type=F32)


def _bf16_round(x):
    return x.astype(BF16).astype(F32)


def _rms(x, w):
    ms = jnp.mean(x * x, axis=-1, keepdims=True)
    return x * lax.rsqrt(ms + RMS_EPS) * w


def _pick_ff_tile(d_ff):
    best = LANES
    for t in range(LANES, min(d_ff, 1408) + 1, LANES):
        if d_ff % t == 0:
            best = t
    return best


def _params(*sem):
    return pltpu.CompilerParams(dimension_semantics=sem, vmem_limit_bytes=VMEM_LIMIT_BYTES)


def _const_spec(shape):
    nd = len(shape)
    return pl.BlockSpec(shape, lambda *_: (0,) * nd)


def _ada_kernel(c_ref, w_ref, b_ref, o_ref):
    c = c_ref[...]
    s = c * jax.nn.sigmoid(c)
    o_ref[...] = _dot(s.astype(BF16), w_ref[...].astype(BF16)) + b_ref[...]


def _ada(c, w, b):
    m, d = c.shape
    n = w.shape[1]
    tn = 1024 if n % 1024 == 0 else n
    return pl.pallas_call(
        _ada_kernel,
        grid=(n // tn,),
        in_specs=[pl.BlockSpec((m, d), lambda j: (0, 0)),
                  pl.BlockSpec((d, tn), lambda j: (0, j)),
                  pl.BlockSpec((1, tn), lambda j: (0, j))],
        out_specs=pl.BlockSpec((m, tn), lambda j: (0, j)),
        out_shape=jax.ShapeDtypeStruct((m, n), F32),
        compiler_params=_params("arbitrary"),
        name="ada_mod",
    )(c, w, b.reshape(1, n))


def _ffn_update(x, shift, scale, gate, nw, wg_ref, wu_ref, wd_ref, tf):
    h = (_rms(x, nw) * (1.0 + scale) + shift).astype(BF16)
    d_ff = wg_ref.shape[1]
    acc = jnp.zeros(x.shape, F32)
    for j in range(d_ff // tf):
        g = _dot(h, wg_ref[:, j * tf:(j + 1) * tf])
        u = _dot(h, wu_ref[:, j * tf:(j + 1) * tf])
        a = (g * jax.nn.sigmoid(g) * u).astype(BF16)
        acc = acc + _dot(a, wd_ref[j * tf:(j + 1) * tf, :])
    return x + 0.5 * gate * acc


class _GateStream(NamedTuple):
    page_table: jax.Array
    q_col: jax.Array
    kt_pool: jax.Array
    layer: int
    first_page: int


def _gate_pages(q_ref, k_refs, gate_ref, s_ref):
    q = q_ref[...]
    q_bf = _bf16_round(q)
    for bi in range(len(k_refs) // PAGES_PER_BLOCK):
        pages = [k_refs[bi * PAGES_PER_BLOCK + i][...] for i in range(PAGES_PER_BLOCK)]
        for i, k_page in enumerate(pages):
            s_ref[bi * PAGES_PER_BLOCK + i] = jnp.sum(k_page * q, axis=1)
        ksum = jnp.sum(functools.reduce(jnp.add, pages), axis=-1, keepdims=True)
        gate_ref[bi] = jnp.sum(_bf16_round(ksum * (1.0 / MOBA_BLOCK)) * q_bf, axis=1)


def _gate_stream_io(gs, steps_per_batch):
    db = gs.page_table.shape[0]
    npg = GATE_PAGES_PER_STEP
    seq = lambda i, t: i * steps_per_batch + t
    page = lambda k: pl.BlockSpec(
        (None, None, N_HEADS_A, HEAD_DIM_A, PAGE_SIZE),
        lambda i, t, pt: (gs.layer, pt[seq(i, t), gs.first_page + k], 0, 0, 0))
    per_seq = lambda shp: pl.BlockSpec((None,) + shp, lambda i, t, pt: (seq(i, t), 0, 0, 0))
    in_specs = [per_seq((N_HEADS_A, HEAD_DIM_A, 1))] + [page(k) for k in range(npg)]
    out_specs = [per_seq((npg // PAGES_PER_BLOCK, N_HEADS_A, 1)), per_seq((npg, N_HEADS_A, PAGE_SIZE))]
    out_shape = [jax.ShapeDtypeStruct((db, npg // PAGES_PER_BLOCK, N_HEADS_A, 1), F32),
                 jax.ShapeDtypeStruct((db, npg, N_HEADS_A, PAGE_SIZE), F32)]
    operands = [gs.q_col] + [gs.kt_pool] * npg
    return in_specs, out_specs, out_shape, operands


def _dense_call(body, name, grid, in_specs, out_specs, out_shape, operands, n_in, gate):
    if gate is None:
        def kernel_fn(*refs):
            body(refs[:n_in], refs[n_in:])
        return pl.pallas_call(kernel_fn, grid=grid, in_specs=in_specs, out_specs=out_specs, out_shape=out_shape,
                              compiler_params=_params("arbitrary", "arbitrary"), name=name)(*operands)
    assert gate.page_table.shape[0] == grid[0] * grid[1]
    g_in, g_out, g_shape, g_ops = _gate_stream_io(gate, grid[1])
    n_out = len(out_specs)

    def kernel_fn(pt_ref, *refs):
        del pt_ref
        ins, outs = refs[:n_in + len(g_in)], refs[n_in + len(g_in):]
        body(ins[:n_in], outs[:n_out])
        _gate_pages(ins[n_in], ins[n_in + 1:], *outs[n_out:])

    return pl.pallas_call(
        kernel_fn,
        grid_spec=pltpu.PrefetchScalarGridSpec(
            num_scalar_prefetch=1, grid=grid, in_specs=list(in_specs) + g_in, out_specs=list(out_specs) + g_out),
        out_shape=list(out_shape) + g_shape,
        compiler_params=_params("arbitrary", "arbitrary"),
        name=name,
    )(gate.page_table, *operands, *g_ops)


def _ffn(x, mod, nw, wg, wu, wd, k0, tm, gate=None):
    b, s, d = x.shape
    r = mod.shape[2]
    d_ff = wg.shape[1]
    tf = _pick_ff_tile(d_ff)

    def body(ins, outs):
        x_ref, mod_ref, nw_ref, wg_ref, wu_ref, wd_ref = ins
        outs[0][...] = _ffn_update(x_ref[...], mod_ref[k0], mod_ref[k0 + 1], mod_ref[k0 + 2],
                                   nw_ref[...], wg_ref, wu_ref, wd_ref, tf)

    return _dense_call(
        body, "ffn", (b, s // tm),
        in_specs=[pl.BlockSpec((None, tm, d), lambda i, t, *_: (i, t, 0)),
                  pl.BlockSpec((None, N_MOD, r, d), lambda i, t, *_: (i, 0, 0, 0)),
                  _const_spec((1, d)),
                  _const_spec((d, d_ff)), _const_spec((d, d_ff)), _const_spec((d_ff, d))],
        out_specs=[pl.BlockSpec((None, tm, d), lambda i, t, *_: (i, t, 0))],
        out_shape=[jax.ShapeDtypeStruct((b, s, d), F32)],
        operands=(x, mod, nw, wg, wu, wd), n_in=6, gate=gate)


def _head_norm(seg, gain_row, bd):
    sq = (seg * seg).astype(BF16)
    parts = [_dot(sq[:, c * MXU_DIM:(c + 1) * MXU_DIM], bd) for c in range(WIDTH // MXU_DIM)]
    ms = jnp.concatenate(parts, axis=1) * (1.0 / 64.0)
    return seg * lax.rsqrt(ms + RMS_EPS) * gain_row


def _proj_kernel(x_ref, mod_ref, nw_ref, win_ref, gains_ref, bd_ref, *outs, transposed):
    x = x_ref[...]
    d = x.shape[1]
    h = (_rms(x, nw_ref[...]) * (1.0 + mod_ref[4]) + mod_ref[3]).astype(BF16)
    bd = bd_ref[...]

    def seg(j):
        return _dot(h, win_ref[:, j * WIDTH:(j + 1) * WIDTH])

    q_a = _head_norm(seg(0), gains_ref[0:1, :], bd) * QK_SCALE
    k_a = _head_norm(seg(1), gains_ref[1:2, :], bd)
    v_a = seg(2)
    q_b = _head_norm(seg(3), gains_ref[2:3, :], bd) * QK_SCALE
    k_b = _head_norm(seg(4), gains_ref[3:4, :], bd)
    v_b = seg(5)
    g0 = 6 * WIDTH
    g_a = jax.nn.sigmoid(_dot(h, win_ref[:, g0:g0 + d]))
    g_b = jax.nn.sigmoid(_dot(h, win_ref[:, g0 + d:g0 + 2 * d]))

    if not transposed:
        (qa_ref, ka_ref, va_ref, qb_ref, kb_ref, vb_ref, ga_ref, gb_ref) = outs
        qa_ref[...] = q_a
        ka_ref[...] = k_a
        va_ref[...] = v_a
        qb_ref[...] = q_b
        kb_ref[...] = k_b
        vb_ref[...] = v_b
    else:
        (kta_ref, vta32_ref, kb_ref, vb_ref, kabf_ref, kbbf_ref, qta_ref, qtb_ref,
         vta_ref, vtb_ref, km_ref, ga_ref, gb_ref) = outs
        kta_ref[...] = k_a.T
        vt_a = v_a.T
        vta32_ref[...] = vt_a
        qt_a, qt_b, vt_b = q_a.T, q_b.T, v_b.T
        for r in range(x.shape[0] // MOBA_BLOCK):
            rows = slice(r * MOBA_BLOCK, (r + 1) * MOBA_BLOCK)
            kabf_ref[r] = k_a[rows].astype(BF16)
            kbbf_ref[r] = k_b[rows].astype(BF16)
            qta_ref[r] = qt_a[:, rows].astype(BF16)
            qtb_ref[r] = qt_b[:, rows].astype(BF16)
            vta_ref[r] = vt_a[:, rows].astype(BF16)
            vtb_ref[r] = vt_b[:, rows].astype(BF16)
            km_ref[r] = jnp.sum(k_a[rows], axis=0, keepdims=True) * (1.0 / MOBA_BLOCK)
        for hd in range(N_HEADS_B):
            lanes = slice(V_DIM_B * hd, V_DIM_B * (hd + 1))
            kb_ref[pl.ds(hd, x.shape[0], stride=N_HEADS_B), :] = k_b[:, lanes]
            vb_ref[pl.ds(hd, x.shape[0], stride=N_HEADS_B), :] = v_b[:, lanes]
    ga_ref[...] = g_a.astype(BF16)
    gb_ref[...] = g_b.astype(BF16)


def _proj(x, mod, nw, w_in, gains, bd, tm, transposed, gate=None):
    b, s, d = x.shape
    r = mod.shape[2]
    d_in = w_in.shape[1]
    row_spec = lambda w: pl.BlockSpec((None, tm, w), lambda i, t, *_: (i, t, 0))
    rows = lambda w, dt=F32: jax.ShapeDtypeStruct((b, s, w), dt)
    if transposed:
        nb, bpt = s // MOBA_BLOCK, tm // MOBA_BLOCK
        blk = lambda shp: pl.BlockSpec((None, bpt) + shp, lambda i, t, *_: (i, t, 0, 0))
        col_spec = pl.BlockSpec((None, WIDTH, tm), lambda i, t, *_: (i, 0, t))
        head_rows = pl.BlockSpec((None, tm * N_HEADS_B, V_DIM_B), lambda i, t, *_: (i, t, 0))
        out_specs = [col_spec] * 2 + [head_rows] * 2 + [blk((MOBA_BLOCK, WIDTH))] * 2 \
            + [blk((WIDTH, MOBA_BLOCK))] * 4 + [blk((1, WIDTH))] + [row_spec(d)] * 2
        out_shape = [jax.ShapeDtypeStruct((b, WIDTH, s), F32)] * 2 \
            + [jax.ShapeDtypeStruct((b, s * N_HEADS_B, V_DIM_B), F32)] * 2 \
            + [jax.ShapeDtypeStruct((b, nb, MOBA_BLOCK, WIDTH), BF16)] * 2 \
            + [jax.ShapeDtypeStruct((b, nb, WIDTH, MOBA_BLOCK), BF16)] * 4 \
            + [jax.ShapeDtypeStruct((b, nb, 1, WIDTH), F32)] + [rows(d, BF16)] * 2
    else:
        out_specs = [row_spec(WIDTH)] * 6 + [row_spec(d)] * 2
        out_shape = [rows(WIDTH)] * 6 + [rows(d, BF16)] * 2
    return _dense_call(
        lambda ins, outs: _proj_kernel(*ins, *outs, transposed=transposed), "mixer_proj", (b, s // tm),
        in_specs=[pl.BlockSpec((None, tm, d), lambda i, t, *_: (i, t, 0)),
                  pl.BlockSpec((None, N_MOD, r, d), lambda i, t, *_: (i, 0, 0, 0)),
                  _const_spec((1, d)), _const_spec((d, d_in)),
                  _const_spec((4, WIDTH)), _const_spec((MXU_DIM, MXU_DIM))],
        out_specs=out_specs, out_shape=out_shape, operands=(x, mod, nw, w_in, gains, bd), n_in=6, gate=gate)


FEATURE_ROWS = SUBLANES


def _key_features(n):
    lane = lax.broadcasted_iota(jnp.int32, (MOBA_BLOCK, LANES), 1)
    key = lax.broadcasted_iota(jnp.int32, (MOBA_BLOCK, LANES), 0).astype(F32)
    f = jnp.where(lane == 0, 1.0, jnp.where(lane == 1, key, jnp.where(lane == FEATURE_ROWS + n, 1.0, 0.0)))
    return f.astype(BF16)


def _query_features(slope, block_bias):
    qry = lax.broadcasted_iota(jnp.int32, (FEATURE_ROWS, MOBA_BLOCK), 1).astype(F32)
    row = lax.broadcasted_iota(jnp.int32, (FEATURE_ROWS, MOBA_BLOCK), 0)
    head = jnp.where(row == 0, -slope * qry, jnp.where(row == 1, slope, 0.0))
    pad = jnp.zeros((LANES - FEATURE_ROWS - block_bias.shape[0], MOBA_BLOCK), F32)
    return jnp.concatenate([head, block_bias, pad], axis=0).astype(BF16)


def _augmented_query_pairs(qms, slopes, block_biases):
    q_aug = [jnp.concatenate([qm, _query_features(s, bb)], axis=0)
             for qm, s, bb in zip(qms, slopes, block_biases)]
    return [jnp.concatenate(q_aug[2 * p:2 * p + 2], axis=1) for p in range(N_GROUPS // 2)]


def _pair_scores(kb, kfeat, q_pairs, p):
    k_aug = jnp.concatenate([kb[:, LANES * p:LANES * (p + 1)], kfeat], axis=1)
    return _dot(k_aug, q_pairs[p])


def _attend_block(kb, vtb, kfeat, q_pairs, causal, m_all, l_all, acc_sc, v_rows, st_first, s0_sc, k_next):
    m_out, l_out = [], []
    n_pairs = len(q_pairs)
    st_next = _pair_scores(kb, kfeat, q_pairs, 0) if st_first is None else st_first
    for p in range(n_pairs):
        st_pair = st_next
        if p + 1 < n_pairs:
            st_next = _pair_scores(kb, kfeat, q_pairs, p + 1)
        else:
            s0_sc[...] = _pair_scores(*k_next, q_pairs, 0)
        for half in range(2):
            g = 2 * p + half
            st = st_pair[:, MOBA_BLOCK * half:MOBA_BLOCK * (half + 1)]
            if causal is not None:
                st = jnp.where(causal, st, NEG)
            m_old = m_all[g:g + 1, :]
            m_new = jnp.maximum(m_old, jnp.max(st, axis=0, keepdims=True))
            alpha = jnp.exp(m_old - m_new)
            pt = jnp.exp(st - m_new)
            l_out.append(alpha * l_all[g:g + 1, :] + jnp.sum(pt, axis=0, keepdims=True))
            rows = v_rows(g)
            nr = rows.stop - rows.start
            acc_rows = slice(g * nr, (g + 1) * nr)
            acc_sc[acc_rows, :] = alpha * acc_sc[acc_rows, :] + _dot(vtb[rows, :], pt.astype(BF16))
            m_out.append(m_new)
    return jnp.concatenate(m_out, axis=0), jnp.concatenate(l_out, axis=0)


def _attend_all_blocks(qi, k_ref, vt_ref, q_pairs, acc_sc, s0_sc, v_rows):
    key_i = lax.broadcasted_iota(jnp.int32, (MOBA_BLOCK, MOBA_BLOCK), 0)
    qry_i = lax.broadcasted_iota(jnp.int32, (MOBA_BLOCK, MOBA_BLOCK), 1)
    acc_sc[...] = jnp.zeros(acc_sc.shape, F32)
    m0 = jnp.full((N_GROUPS, MOBA_BLOCK), NEG, F32)
    l0 = jnp.zeros((N_GROUPS, MOBA_BLOCK), F32)
    m1, l1 = _attend_block(k_ref[qi], vt_ref[qi], _key_features(qi), q_pairs, key_i <= qry_i,
                           m0, l0, acc_sc, v_rows, None, s0_sc, (k_ref[0], _key_features(0)))

    def past(n, carry):
        return _attend_block(k_ref[n], vt_ref[n], _key_features(n), q_pairs, None, *carry, acc_sc, v_rows,
                             s0_sc[...], s0_sc, (k_ref[n + 1], _key_features(n + 1)))

    _, l_fin = lax.fori_loop(0, qi, past, (m1, l1))
    return l_fin


def _masked_queries(qt):
    row = lax.broadcasted_iota(jnp.int32, (LANES, MOBA_BLOCK), 0)
    out = []
    for g in range(N_GROUPS):
        p, half = divmod(g, 2)
        pair = qt[LANES * p:LANES * (p + 1), :]
        keep = (row >= 64 * half) & (row < 64 * (half + 1))
        out.append(jnp.where(keep, pair, jnp.zeros_like(pair)))
    return out


def _block_distance(qi, nbp):
    blk_i = lax.broadcasted_iota(jnp.int32, (nbp, MOBA_BLOCK), 0)
    return blk_i, ((qi - blk_i) * MOBA_BLOCK).astype(F32)


def _moba_prompt_kernel(qt_ref, k_ref, vt_ref, km_ref, o_ref, acc_sc, s0_sc):
    qi = pl.program_id(1)
    nb = k_ref.shape[0]
    nbp = -(-nb // SUBLANES) * SUBLANES
    blk = MOBA_BLOCK
    slopes = _slopes(N_HEADS_A)
    qms = _masked_queries(qt_ref[...])

    km = km_ref[...].astype(BF16)
    blk_i, blk_dist = _block_distance(qi, nbp)
    valid = blk_i[:nb] < qi
    biases = []
    for g in range(N_GROUPS):
        lanes = slice(LANES * (g // 2), LANES * (g // 2 + 1))
        gate = _dot(km[:, lanes], qms[g])
        gate = jnp.where(valid, gate, NEG)
        rank = jnp.zeros((nb, blk), jnp.int32)
        for m in range(nb):
            gm = gate[m:m + 1, :]
            beats = (gm > gate) | ((gm == gate) & (m < blk_i[:nb]))
            rank = rank + beats.astype(jnp.int32)
        drop = jnp.where(valid & (rank < MOBA_TOPK), 0.0, NEG)
        if nbp > nb:
            drop = jnp.concatenate([drop, jnp.zeros((nbp - nb, blk), F32)], axis=0)
        biases.append(jnp.where(blk_i == qi, 0.0, drop - slopes[g] * blk_dist))

    q_pairs = _augmented_query_pairs(qms, slopes, biases)
    l_fin = _attend_all_blocks(qi, k_ref, vt_ref, q_pairs, acc_sc, s0_sc,
                               lambda g: slice(64 * g, 64 * (g + 1)))
    parts = [acc_sc[64 * g:64 * (g + 1), :] * (1.0 / l_fin[g:g + 1, :]) for g in range(N_GROUPS)]
    o_ref[...] = jnp.concatenate(parts, axis=0).T.astype(BF16)


def _lambda(lam_ref, lam_init):
    a = jnp.sum(lam_ref[0:1, :] * lam_ref[1:2, :], axis=-1, keepdims=True)
    b = jnp.sum(lam_ref[2:3, :] * lam_ref[3:4, :], axis=-1, keepdims=True)
    return jnp.exp(a) - jnp.exp(b) + lam_init


def _diff_prompt_kernel(qt_ref, k_ref, vt_ref, lam_ref, subln_ref, o_ref, acc_sc, s0_sc, *, lam_init):
    qi = pl.program_id(1)
    nb = k_ref.shape[0]
    nbp = -(-nb // SUBLANES) * SUBLANES
    slopes = [s for s in _slopes(N_HEADS_B) for _ in range(2)]
    qms = _masked_queries(qt_ref[...])
    _, blk_dist = _block_distance(qi, nbp)
    q_pairs = _augmented_query_pairs(qms, slopes, [-s * blk_dist for s in slopes])
    v_rows = lambda g: slice(V_DIM_B * (g // 2), V_DIM_B * (g // 2 + 1))
    l_fin = _attend_all_blocks(qi, k_ref, vt_ref, q_pairs, acc_sc, s0_sc, v_rows)

    lam = _lambda(lam_ref, lam_init)
    parts = []
    for h in range(N_HEADS_B):
        o0 = acc_sc[V_DIM_B * (2 * h):V_DIM_B * (2 * h + 1), :] * (1.0 / l_fin[2 * h:2 * h + 1, :])
        o1 = acc_sc[V_DIM_B * (2 * h + 1):V_DIM_B * (2 * h + 2), :] * (1.0 / l_fin[2 * h + 1:2 * h + 2, :])
        o = o0 - lam * o1
        ms = jnp.mean(o * o, axis=0, keepdims=True)
        parts.append(o * lax.rsqrt(ms + RMS_EPS))
    o_t = jnp.concatenate(parts, axis=0).T
    o_ref[...] = (o_t * subln_ref[...] * (1.0 - lam_init)).astype(BF16)


def _prompt_attention(qt_a, kbf_a, vt_a, km_a, qt_b, kbf_b, vt_b, lam_vecs, subln_row, lam_init):
    b, nb = qt_a.shape[:2]
    blk = MOBA_BLOCK
    s = nb * blk
    q_spec = pl.BlockSpec((None, None, WIDTH, blk), lambda i, t: (i, t, 0, 0))
    k_spec = pl.BlockSpec((None, nb, blk, WIDTH), lambda i, t: (i, 0, 0, 0))
    vt_spec = pl.BlockSpec((None, nb, WIDTH, blk), lambda i, t: (i, 0, 0, 0))
    o_spec = pl.BlockSpec((None, blk, WIDTH), lambda i, t: (i, t, 0))
    o_shape = jax.ShapeDtypeStruct((b, s, WIDTH), BF16)
    o_a = pl.pallas_call(
        _moba_prompt_kernel,
        grid=(b, nb),
        in_specs=[q_spec, k_spec, vt_spec, pl.BlockSpec((None, nb, WIDTH), lambda i, t: (i, 0, 0))],
        out_specs=o_spec,
        out_shape=o_shape,
        scratch_shapes=[pltpu.VMEM((WIDTH, blk), F32), pltpu.VMEM((blk, 2 * blk), F32)],
        compiler_params=_params("arbitrary", "arbitrary"),
        name="moba_prompt",
    )(qt_a, kbf_a, vt_a, km_a)
    o_b = pl.pallas_call(
        functools.partial(_diff_prompt_kernel, lam_init=lam_init),
        grid=(b, nb),
        in_specs=[q_spec, k_spec, vt_spec, _const_spec((4, HEAD_DIM_B)), _const_spec((1, WIDTH))],
        out_specs=o_spec,
        out_shape=o_shape,
        scratch_shapes=[pltpu.VMEM((N_GROUPS * V_DIM_B, blk), F32), pltpu.VMEM((blk, 2 * blk), F32)],
        compiler_params=_params("arbitrary", "arbitrary"),
        name="diff_prompt",
    )(qt_b, kbf_b, vt_b, lam_vecs, subln_row)
    return o_a, o_b


def _mix_ffn_kernel(x_ref, oa_ref, ob_ref, ga_ref, gb_ref, mod_ref, wba_ref, wbd_ref, wout_ref,
                    nw_ref, wg_ref, wu_ref, wd_ref, o_ref, *, tf):
    y_a = _dot(oa_ref[...], wba_ref[...])
    y_b = _dot(ob_ref[...], wbd_ref[...])
    mixed = _dot((ga_ref[...].astype(F32) * y_a + gb_ref[...].astype(F32) * y_b).astype(BF16), wout_ref[...])
    x = x_ref[...] + mod_ref[5] * mixed
    o_ref[...] = _ffn_update(x, mod_ref[6], mod_ref[7], mod_ref[8], nw_ref[...],
                             wg_ref, wu_ref, wd_ref, tf)


def _mix_ffn(x, o_a, o_b, g_a, g_b, mod, w_ba, w_bd, w_out, nw, wg, wu, wd, tm):
    b, s, d = x.shape
    r = mod.shape[2]
    d_ff = wg.shape[1]
    tf = _pick_ff_tile(d_ff)
    row_spec = lambda w: pl.BlockSpec((None, tm, w), lambda i, t: (i, t, 0))
    return pl.pallas_call(
        functools.partial(_mix_ffn_kernel, tf=tf),
        grid=(b, s // tm),
        in_specs=[row_spec(d), row_spec(WIDTH), row_spec(WIDTH), row_spec(d), row_spec(d),
                  pl.BlockSpec((None, N_MOD, r, d), lambda i, t: (i, 0, 0, 0)),
                  _const_spec((WIDTH, d)), _const_spec((WIDTH, d)), _const_spec((d, d)),
                  _const_spec((1, d)),
                  _const_spec((d, d_ff)), _const_spec((d, d_ff)), _const_spec((d_ff, d))],
        out_specs=row_spec(d),
        out_shape=jax.ShapeDtypeStruct((b, s, d), F32),
        compiler_params=_params("arbitrary", "arbitrary"),
        name="mix_ffn",
    )(x, o_a, o_b, g_a, g_b, mod, w_ba, w_bd, w_out, nw, wg, wu, wd)


def _moba_select_kernel(gate_ref, sel_ref):
    gate = gate_ref[...]
    nblk = gate.shape[1]
    blk_i = lax.broadcasted_iota(jnp.int32, gate.shape, 1)
    rank = jnp.zeros(gate.shape, jnp.int32)
    for m in range(nblk):
        gm = gate[:, m:m + 1]
        beats = (gm > gate) | ((gm == gate) & (m < blk_i))
        rank = rank + beats.astype(jnp.int32)
    lane = lax.broadcasted_iota(jnp.int32, sel_ref.shape, 1)
    sel = jnp.zeros(sel_ref.shape, jnp.int32)
    for j in range(MOBA_TOPK):
        sel = jnp.where(lane == j, jnp.sum(jnp.where(rank == j, blk_i, 0), axis=1, keepdims=True), sel)
    sel_ref[...] = sel


def _moba_attend_kernel(pt_ref, sel_ref, q_ref, kn_ref, vn_ref, *rest, past_len, n_chunks):
    del pt_ref
    nt = N_HEADS_A * MOBA_TOPK * PAGES_PER_BLOCK
    s_refs, v_refs, o_ref = rest[:n_chunks], rest[n_chunks:n_chunks + nt], rest[n_chunks + nt]
    npg = GATE_PAGES_PER_STEP
    s = pl.program_id(0)
    lane = lax.broadcasted_iota(jnp.int32, (1, PAGE_SIZE), 1).astype(F32)
    slopes = _slopes(N_HEADS_A)
    for h in range(N_HEADS_A):
        q_h = q_ref[h]
        tiles = range(h * MOBA_TOPK * PAGES_PER_BLOCK, (h + 1) * MOBA_TOPK * PAGES_PER_BLOCK)
        rows = []
        for j in range(MOBA_TOPK):
            blk = sel_ref[s, j * N_HEADS_A + h]
            for i in range(PAGES_PER_BLOCK):
                page = PAGES_PER_BLOCK * blk + i
                raw = s_refs[0][jnp.clip(page, 0, npg - 1), h:h + 1, :]
                for c in range(1, n_chunks):
                    other = s_refs[c][jnp.clip(page - c * npg, 0, npg - 1), h:h + 1, :]
                    raw = jnp.where(page >= c * npg, other, raw)
                dist0 = (past_len - blk * MOBA_BLOCK - i * PAGE_SIZE).astype(F32)
                rows.append(raw - slopes[h] * (dist0 - lane))
        s_self = jnp.sum(q_h * kn_ref[h], axis=0, keepdims=True)
        m = s_self
        for r in rows:
            m = jnp.maximum(m, jnp.max(r, axis=1, keepdims=True))
        w_self = jnp.exp(s_self - m)
        l = w_self
        acc = jnp.zeros((HEAD_DIM_A, PAGE_SIZE), F32)
        for t, r in zip(tiles, rows):
            p = jnp.exp(r - m)
            l = l + jnp.sum(p, axis=1, keepdims=True)
            acc = acc + p * v_refs[t][...]
        o = jnp.sum(acc, axis=1, keepdims=True) + w_self * vn_ref[h]
        o_ref[h] = o * (1.0 / l)


def _moba_pool_tiles(pool):
    return jnp.transpose(pool, (0, 1, 3, 4, 2))


def _moba_decode(q, k_new, v_new, gates, raws, vt_pool, page_table, layer):
    db, n_pages = page_table.shape
    nblk = n_pages // PAGES_PER_BLOCK
    assert nblk >= MOBA_TOPK and len(raws) * GATE_PAGES_PER_STEP == n_pages
    col = lambda a: a[..., None]

    gate = jnp.concatenate(gates, axis=1)[..., 0]
    gate = gate.transpose(0, 2, 1).reshape(db * N_HEADS_A, nblk)
    sel = pl.pallas_call(
        _moba_select_kernel,
        grid=(1,),
        in_specs=[_const_spec(gate.shape)],
        out_specs=_const_spec((db * N_HEADS_A, LANES)),
        out_shape=jax.ShapeDtypeStruct((db * N_HEADS_A, LANES), jnp.int32),
        compiler_params=_params("arbitrary"),
        name="moba_select",
    )(gate)
    sel = sel[:, :MOBA_TOPK].reshape(db, N_HEADS_A, MOBA_TOPK).transpose(0, 2, 1).reshape(db, -1)

    def tile_spec(h, j, i):
        def index(s, pt, sel_):
            blk = jnp.clip(sel_[s, j * N_HEADS_A + h], 0, nblk - 1)
            return (layer, pt[s, PAGES_PER_BLOCK * blk + i], h, 0, 0)
        return pl.BlockSpec((None, None, None, HEAD_DIM_A, PAGE_SIZE), index)

    tile_specs = [tile_spec(h, j, i) for h in range(N_HEADS_A) for j in range(MOBA_TOPK)
                  for i in range(PAGES_PER_BLOCK)]
    seq = pl.BlockSpec((None, N_HEADS_A, HEAD_DIM_A, 1), lambda s, pt, sel_: (s, 0, 0, 0))
    raw_spec = pl.BlockSpec((None, GATE_PAGES_PER_STEP, N_HEADS_A, PAGE_SIZE), lambda s, pt, sel_: (s, 0, 0, 0))
    o = pl.pallas_call(
        functools.partial(_moba_attend_kernel, past_len=n_pages * PAGE_SIZE, n_chunks=len(raws)),
        grid_spec=pltpu.PrefetchScalarGridSpec(
            num_scalar_prefetch=2,
            grid=(db,),
            in_specs=[seq, seq, seq] + [raw_spec] * len(raws) + tile_specs,
            out_specs=seq,
        ),
        out_shape=jax.ShapeDtypeStruct((db, N_HEADS_A, HEAD_DIM_A, 1), F32),
        compiler_params=_params("arbitrary"),
        name="moba_attend",
    )(page_table, sel, col(q), col(k_new), col(v_new), *raws, *([vt_pool] * len(tile_specs)))
    return o[..., 0]


def _dot_nt(a, b):
    return lax.dot_general(a, b, (((1,), (1,)), ((), ())), preferred_element_type=F32)


def _diff_decode_kernel(pt_ref, q_ref, kn_ref, vn_ref, slope_ref, tbias_ref, lam_ref, subln_ref,
                        *rest, past_len, lam_init):
    del pt_ref
    npg = PAGES_PER_STEP
    k_refs, v_refs = rest[:npg], rest[npg:2 * npg]
    o_ref, m_sc, l_sc, acc_sc = rest[2 * npg:]
    j = pl.program_id(1)
    q8 = q_ref[...]
    slope8 = slope_ref[...]
    tbias = tbias_ref[...]
    lane_max = lambda x: jnp.max(x, axis=-1, keepdims=True)

    @pl.when(j == 0)
    def _():
        s_self = jnp.sum(_bf16_round(q8) * _bf16_round(kn_ref[...]), axis=-1, keepdims=True)
        m_sc[...] = jnp.broadcast_to(s_self, m_sc.shape)
        l_sc[...] = jnp.ones(l_sc.shape, F32)
        acc_sc[...] = vn_ref[...]

    q8b = q8.astype(BF16)
    scores = []
    for idx in range(npg):
        dist0 = (past_len - (j * npg + idx) * PAGE_SIZE).astype(F32)
        scores.append(_dot_nt(q8b, k_refs[idx][...].astype(BF16)) + tbias - slope8 * dist0)
    m_old = m_sc[...]
    m_step = lane_max(functools.reduce(jnp.maximum, scores))
    m_new = jnp.maximum(m_old, m_step)
    alpha = jnp.exp(m_old - m_new)
    m1 = m_new[:, 0:1]
    p_sum = jnp.zeros(scores[0].shape, F32)
    pv = jnp.zeros(acc_sc.shape, F32)
    for idx in range(npg):
        p = jnp.exp(scores[idx] - m1)
        p_sum = p_sum + p
        pv = pv + _dot(p.astype(BF16), v_refs[idx][...].astype(BF16))
    l_new = alpha * l_sc[...] + jnp.sum(p_sum, axis=-1, keepdims=True)
    acc_new = alpha * acc_sc[...] + pv
    m_sc[...] = m_new
    l_sc[...] = l_new
    acc_sc[...] = acc_new

    @pl.when(j == pl.num_programs(1) - 1)
    def _():
        o_c = acc_new * (1.0 / l_new)
        o = o_c - _lambda(lam_ref, lam_init) * pltpu.roll(o_c, N_HEADS_B, axis=0)
        ms = jnp.mean(o * o, axis=-1, keepdims=True)
        o_ref[...] = o * lax.rsqrt(ms + RMS_EPS) * subln_ref[...] * (1.0 - lam_init)


def _diff_decode(q, k_new, v_new, k_pool, v_pool, page_table, layer, lam_vecs, subln, lam_init):
    db, n_pages = page_table.shape
    depth, n_phys = k_pool.shape[:2]
    rows = PAGE_SIZE * N_HEADS_B
    pages = lambda pool: pool.reshape(depth * n_phys, rows, V_DIM_B)
    twice = lambda a: jnp.concatenate([a, a], axis=1)
    branch = (np.arange(V_DIM_B) // HEAD_DIM_B)[None, :] == np.arange(2)[:, None]
    q8 = jnp.concatenate([q * branch[0].astype(np.float32), q * branch[1].astype(np.float32)], axis=1)
    slopes = np.tile(np.asarray(_slopes(N_HEADS_B), np.float32), 2)
    col = np.arange(rows)
    own = (col % N_HEADS_B)[None, :] == (np.arange(SUBLANES) % N_HEADS_B)[:, None]
    tbias_np = np.where(own, slopes[:, None] * (col // N_HEADS_B)[None, :], NEG).astype(np.float32)
    seq = pl.BlockSpec((None, SUBLANES, V_DIM_B), lambda s, j, pt: (s, 0, 0))
    cst = lambda shp: pl.BlockSpec(shp, lambda s, j, pt: (0, 0))
    page = lambda i: pl.BlockSpec(
        (None, rows, V_DIM_B), lambda s, j, pt: (layer * n_phys + pt[s, j * PAGES_PER_STEP + i], 0, 0))
    page_specs = [page(i) for i in range(PAGES_PER_STEP)]
    stat = pltpu.VMEM((SUBLANES, LANES), F32)
    grid_spec = pltpu.PrefetchScalarGridSpec(
        num_scalar_prefetch=1,
        grid=(db, n_pages // PAGES_PER_STEP),
        in_specs=[seq, seq, seq, cst((SUBLANES, 1)), cst((SUBLANES, rows)), cst((4, HEAD_DIM_B)), cst((1, V_DIM_B))]
        + page_specs * 2,
        out_specs=seq,
        scratch_shapes=[stat, stat, stat],
    )
    o = pl.pallas_call(
        functools.partial(_diff_decode_kernel, past_len=n_pages * PAGE_SIZE, lam_init=lam_init),
        grid_spec=grid_spec,
        out_shape=jax.ShapeDtypeStruct((db, SUBLANES, V_DIM_B), F32),
        compiler_params=_params("arbitrary", "arbitrary"),
        name="diff_decode",
    )(page_table, q8, twice(k_new), twice(v_new), jnp.asarray(slopes[:, None]), jnp.asarray(tbias_np),
      lam_vecs, subln, *([pages(k_pool)] * PAGES_PER_STEP), *([pages(v_pool)] * PAGES_PER_STEP))
    return o[:, :N_HEADS_B]


def _block_diag_ones():
    i = np.arange(MXU_DIM) // 64
    return jnp.asarray((i[:, None] == i[None, :]).astype(np.float32), BF16)


def kernel(x_prompt, x_sample, cache_k_moba, cache_v_moba, cache_k_diff, cache_v_diff, page_table, c_prompt, c_sample, w_ada, b_ada, norm_ffn1, ffn1_w_gate, ffn1_w_up, ffn1_w_down, norm_mix, w_in, qn_moba, kn_moba, qn_diff, kn_diff, lambda_q1, lambda_k1, lambda_q2, lambda_k2, subln_diff, w_branch_moba, w_branch_diff, w_out, norm_ffn2, ffn2_w_gate, ffn2_w_up, ffn2_w_down):
    depth = w_ada.shape[0]
    b, s, d = x_prompt.shape
    db, t_new, _ = x_sample.shape
    assert t_new == 1 and s % TOKEN_TILE == 0 and db % SUBLANES == 0
    n_pages = page_table.shape[1]
    assert n_pages % PAGES_PER_STEP == 0
    bd = _block_diag_ones()
    tile8 = lambda v: jnp.tile(v, WIDTH // v.shape[0]).reshape(1, WIDTH)

    y_p, y_s = x_prompt, x_sample.reshape(1, db, d)
    rows_p, rows_s = [], []
    for l in range(depth):
        lam_init = _lambda_init(l)
        bf = lambda w: w[l].astype(BF16)
        row = lambda v: v[l].reshape(1, -1)
        mod = _ada(jnp.concatenate([c_prompt, c_sample], axis=0), w_ada[l], b_ada[l])
        mod_p = mod[:b].reshape(b, N_MOD, 1, d)
        mod_s = mod[b:].reshape(db, N_MOD, d).transpose(1, 0, 2).reshape(1, N_MOD, db, d)
        ffn1 = (row(norm_ffn1), bf(ffn1_w_gate), bf(ffn1_w_up), bf(ffn1_w_down))
        ffn2 = (row(norm_ffn2), bf(ffn2_w_gate), bf(ffn2_w_up), bf(ffn2_w_down))
        w_in_bf = bf(w_in)
        gains = jnp.concatenate([tile8(qn_moba[l]), tile8(kn_moba[l]), tile8(qn_diff[l]), tile8(kn_diff[l])], 0)
        lam_vecs = jnp.stack([lambda_q1[l], lambda_k1[l], lambda_q2[l], lambda_k2[l]])
        mix_w = (bf(w_branch_moba), bf(w_branch_diff), bf(w_out))

        (x1s,) = _ffn(y_s, mod_s, *ffn1, k0=0, tm=db)
        (q_as, k_as, v_as, q_bs, k_bs, v_bs, g_as, g_bs) = _proj(
            x1s, mod_s, row(norm_mix), w_in_bf, gains, bd, db, False)
        heads_a = lambda a: a.reshape(db, N_HEADS_A, HEAD_DIM_A)
        heads_b = lambda a: a.reshape(db, N_HEADS_B, V_DIM_B)

        kt_pool = _moba_pool_tiles(cache_k_moba)
        gate_stream = lambda first: _GateStream(page_table, heads_a(q_as)[..., None], kt_pool, l, first)
        x1, gate0, raw0 = _ffn(y_p, mod_p, *ffn1, k0=0, tm=TOKEN_TILE, gate=gate_stream(0))
        (kt_a, vt32_a, k_b, v_b, kbf_a, kbf_b, qt_a, qt_b, vt_a, vt_b, km_a, g_a, g_b, gate1, raw1) = _proj(
            x1, mod_p, row(norm_mix), w_in_bf, gains, bd, TOKEN_TILE, True, gate=gate_stream(GATE_PAGES_PER_STEP))
        o_a, o_b = _prompt_attention(qt_a, kbf_a, vt_a, km_a.reshape(b, -1, WIDTH), qt_b, kbf_b, vt_b,
                                     lam_vecs, tile8(subln_diff[l]), lam_init)
        y_p = _mix_ffn(x1, o_a, o_b, g_a, g_b, mod_p, *mix_w, *ffn2, tm=TOKEN_TILE)
        token_major = lambda a: a.reshape(b, N_HEADS_A, HEAD_DIM_A, s).transpose(0, 3, 1, 2)
        rows_p.append((token_major(kt_a), token_major(vt32_a), k_b, v_b))

        o_as = _moba_decode(heads_a(q_as), heads_a(k_as), heads_a(v_as), [gate0, gate1], [raw0, raw1],
                            _moba_pool_tiles(cache_v_moba), page_table, l)
        o_bs = _diff_decode(heads_b(q_bs), heads_b(k_bs), heads_b(v_bs), cache_k_diff, cache_v_diff,
                            page_table, l, lam_vecs, subln_diff[l].reshape(1, V_DIM_B), lam_init)
        as_rows = lambda a: a.reshape(1, db, WIDTH).astype(BF16)
        y_s = _mix_ffn(x1s, as_rows(o_as), as_rows(o_bs), g_as, g_bs, mod_s, *mix_w, *ffn2, tm=db)
        rows_s.append((k_as, v_as, k_bs, v_bs))

    def stack(rows, i, lead, heads, hd):
        return jnp.stack([r[i].reshape(lead + (heads, hd)) for r in rows])

    lp, ls = (b, s), (db, 1)
    return (y_p, y_s.reshape(db, 1, d),
            stack(rows_p, 0, lp, N_HEADS_A, HEAD_DIM_A), stack(rows_p, 1, lp, N_HEADS_A, HEAD_DIM_A),
            stack(rows_p, 2, lp, N_HEADS_B, V_DIM_B), stack(rows_p, 3, lp, N_HEADS_B, V_DIM_B),
            stack(rows_s, 0, ls, N_HEADS_A, HEAD_DIM_A), stack(rows_s, 1, ls, N_HEADS_A, HEAD_DIM_A),
            stack(rows_s, 2, ls, N_HEADS_B, V_DIM_B), stack(rows_s, 3, ls, N_HEADS_B, V_DIM_B))
```

```python
import functools
import math
from typing import NamedTuple

import jax
import jax.numpy as jnp
import numpy as np
from jax import lax
from jax.experimental import pallas as pl
from jax.experimental.pallas import tpu as pltpu

F32 = jnp.float32
BF16 = jnp.bfloat16

N_HEADS_A = 8
HEAD_DIM_A = 64
MOBA_BLOCK = 256
MOBA_TOPK = 3
N_HEADS_B = 4
HEAD_DIM_B = 64
V_DIM_B = 2 * HEAD_DIM_B
WIDTH = 512
N_GROUPS = WIDTH // 64
PAGE_SIZE = 128
PAGES_PER_BLOCK = MOBA_BLOCK // PAGE_SIZE
N_MOD = 9
RMS_EPS = 1e-6
QK_SCALE = 0.125
NEG = -1e30

LANES = 128
SUBLANES = 8
MXU_DIM = 256
VMEM_LIMIT_BYTES = 58 * 1024 * 1024

TOKEN_TILE = 512
PAGES_PER_STEP = 32
GATE_PAGES_PER_STEP = 32


def _slopes(n_heads):
    return [2.0 ** (-8.0 * (i + 1) / n_heads) for i in range(n_heads)]


def _lambda_init(layer):
    return 0.8 - 0.6 * math.exp(-0.3 * layer)


def _dot(a, b):
    return jnp.dot(a, b, preferred_element_type=F32)


def _bf16_round(x):
    return x.astype(BF16).astype(F32)


def _rms(x, w):
    ms = jnp.mean(x * x, axis=-1, keepdims=True)
    return x * lax.rsqrt(ms + RMS_EPS) * w


def _pick_ff_tile(d_ff):
    best = LANES
    for t in range(LANES, min(d_ff, 1408) + 1, LANES):
        if d_ff % t == 0:
            best = t
    return best


def _params(*sem):
    return pltpu.CompilerParams(dimension_semantics=sem, vmem_limit_bytes=VMEM_LIMIT_BYTES)


def _const_spec(shape):
    nd = len(shape)
    return pl.BlockSpec(shape, lambda *_: (0,) * nd)


def _ada_kernel(c_ref, w_ref, b_ref, o_ref):
    c = c_ref[...]
    s = c * jax.nn.sigmoid(c)
    o_ref[...] = _dot(s.astype(BF16), w_ref[...].astype(BF16)) + b_ref[...]


def _ada(c, w, b):
    m, d = c.shape
    n = w.shape[1]
    tn = 1024 if n % 1024 == 0 else n
    return pl.pallas_call(
        _ada_kernel,
        grid=(n // tn,),
        in_specs=[pl.BlockSpec((m, d), lambda j: (0, 0)),
                  pl.BlockSpec((d, tn), lambda j: (0, j)),
                  pl.BlockSpec((1, tn), lambda j: (0, j))],
        out_specs=pl.BlockSpec((m, tn), lambda j: (0, j)),
        out_shape=jax.ShapeDtypeStruct((m, n), F32),
        compiler_params=_params("arbitrary"),
        name="ada_mod",
    )(c, w, b.reshape(1, n))


def _ffn_update(x, shift, scale, gate, nw, wg_ref, wu_ref, wd_ref, tf):
    h = (_rms(x, nw) * (1.0 + scale) + shift).astype(BF16)
    d_ff = wg_ref.shape[1]
    acc = jnp.zeros(x.shape, F32)
    for j in range(d_ff // tf):
        g = _dot(h, wg_ref[:, j * tf:(j + 1) * tf])
        u = _dot(h, wu_ref[:, j * tf:(j + 1) * tf])
        a = (g * jax.nn.sigmoid(g) * u).astype(BF16)
        acc = acc + _dot(a, wd_ref[j * tf:(j + 1) * tf, :])
    return x + 0.5 * gate * acc


class _GateStream(NamedTuple):
    page_table: jax.Array
    q_col: jax.Array
    kt_pool: jax.Array
    layer: int
    first_page: int


def _gate_pages(q_ref, k_refs, gate_ref, s_ref):
    q = q_ref[...]
    q_bf = _bf16_round(q)
    for bi in range(len(k_refs) // PAGES_PER_BLOCK):
        pages = [k_refs[bi * PAGES_PER_BLOCK + i][...] for i in range(PAGES_PER_BLOCK)]
        for i, k_page in enumerate(pages):
            s_ref[bi * PAGES_PER_BLOCK + i] = jnp.sum(k_page * q, axis=1)
        ksum = jnp.sum(functools.reduce(jnp.add, pages), axis=-1, keepdims=True)
        gate_ref[bi] = jnp.sum(_bf16_round(ksum * (1.0 / MOBA_BLOCK)) * q_bf, axis=1)


def _gate_stream_io(gs, steps_per_batch):
    db = gs.page_table.shape[0]
    npg = GATE_PAGES_PER_STEP
    seq = lambda i, t: i * steps_per_batch + t
    page = lambda k: pl.BlockSpec(
        (None, None, N_HEADS_A, HEAD_DIM_A, PAGE_SIZE),
        lambda i, t, pt: (gs.layer, pt[seq(i, t), gs.first_page + k], 0, 0, 0))
    per_seq = lambda shp: pl.BlockSpec((None,) + shp, lambda i, t, pt: (seq(i, t), 0, 0, 0))
    in_specs = [per_seq((N_HEADS_A, HEAD_DIM_A, 1))] + [page(k) for k in range(npg)]
    out_specs = [per_seq((npg // PAGES_PER_BLOCK, N_HEADS_A, 1)), per_seq((npg, N_HEADS_A, PAGE_SIZE))]
    out_shape = [jax.ShapeDtypeStruct((db, npg // PAGES_PER_BLOCK, N_HEADS_A, 1), F32),
                 jax.ShapeDtypeStruct((db, npg, N_HEADS_A, PAGE_SIZE), F32)]
    operands = [gs.q_col] + [gs.kt_pool] * npg
    return in_specs, out_specs, out_shape, operands


def _dense_call(body, name, grid, in_specs, out_specs, out_shape, operands, n_in, gate):
    if gate is None:
        def kernel_fn(*refs):
            body(refs[:n_in], refs[n_in:])
        return pl.pallas_call(kernel_fn, grid=grid, in_specs=in_specs, out_specs=out_specs, out_shape=out_shape,
                              compiler_params=_params("arbitrary", "arbitrary"), name=name)(*operands)
    assert gate.page_table.shape[0] == grid[0] * grid[1]
    g_in, g_out, g_shape, g_ops = _gate_stream_io(gate, grid[1])
    n_out = len(out_specs)

    def kernel_fn(pt_ref, *refs):
        del pt_ref
        ins, outs = refs[:n_in + len(g_in)], refs[n_in + len(g_in):]
        body(ins[:n_in], outs[:n_out])
        _gate_pages(ins[n_in], ins[n_in + 1:], *outs[n_out:])

    return pl.pallas_call(
        kernel_fn,
        grid_spec=pltpu.PrefetchScalarGridSpec(
            num_scalar_prefetch=1, grid=grid, in_specs=list(in_specs) + g_in, out_specs=list(out_specs) + g_out),
        out_shape=list(out_shape) + g_shape,
        compiler_params=_params("arbitrary", "arbitrary"),
        name=name,
    )(gate.page_table, *operands, *g_ops)


def _ffn(x, mod, nw, wg, wu, wd, k0, tm, gate=None):
    b, s, d = x.shape
    r = mod.shape[2]
    d_ff = wg.shape[1]
    tf = _pick_ff_tile(d_ff)

    def body(ins, outs):
        x_ref, mod_ref, nw_ref, wg_ref, wu_ref, wd_ref = ins
        outs[0][...] = _ffn_update(x_ref[...], mod_ref[k0], mod_ref[k0 + 1], mod_ref[k0 + 2],
                                   nw_ref[...], wg_ref, wu_ref, wd_ref, tf)

    return _dense_call(
        body, "ffn", (b, s // tm),
        in_specs=[pl.BlockSpec((None, tm, d), lambda i, t, *_: (i, t, 0)),
                  pl.BlockSpec((None, N_MOD, r, d), lambda i, t, *_: (i, 0, 0, 0)),
                  _const_spec((1, d)),
                  _const_spec((d, d_ff)), _const_spec((d, d_ff)), _const_spec((d_ff, d))],
        out_specs=[pl.BlockSpec((None, tm, d), lambda i, t, *_: (i, t, 0))],
        out_shape=[jax.ShapeDtypeStruct((b, s, d), F32)],
        operands=(x, mod, nw, wg, wu, wd), n_in=6, gate=gate)


def _head_norm(seg, gain_row, bd):
    sq = (seg * seg).astype(BF16)
    parts = [_dot(sq[:, c * MXU_DIM:(c + 1) * MXU_DIM], bd) for c in range(WIDTH // MXU_DIM)]
    ms = jnp.concatenate(parts, axis=1) * (1.0 / 64.0)
    return seg * lax.rsqrt(ms + RMS_EPS) * gain_row


def _proj_kernel(x_ref, mod_ref, nw_ref, win_ref, gains_ref, bd_ref, *outs, transposed):
    x = x_ref[...]
    d = x.shape[1]
    h = (_rms(x, nw_ref[...]) * (1.0 + mod_ref[4]) + mod_ref[3]).astype(BF16)
    bd = bd_ref[...]

    def seg(j):
        return _dot(h, win_ref[:, j * WIDTH:(j + 1) * WIDTH])

    q_a = _head_norm(seg(0), gains_ref[0:1, :], bd) * QK_SCALE
    k_a = _head_norm(seg(1), gains_ref[1:2, :], bd)
    v_a = seg(2)
    q_b = _head_norm(seg(3), gains_ref[2:3, :], bd) * QK_SCALE
    k_b = _head_norm(seg(4), gains_ref[3:4, :], bd)
    v_b = seg(5)
    g0 = 6 * WIDTH
    g_a = jax.nn.sigmoid(_dot(h, win_ref[:, g0:g0 + d]))
    g_b = jax.nn.sigmoid(_dot(h, win_ref[:, g0 + d:g0 + 2 * d]))

    if not transposed:
        (qa_ref, ka_ref, va_ref, qb_ref, kb_ref, vb_ref, ga_ref, gb_ref) = outs
        qa_ref[...] = q_a
        ka_ref[...] = k_a
        va_ref[...] = v_a
        qb_ref[...] = q_b
        kb_ref[...] = k_b
        vb_ref[...] = v_b
    else:
        (kta_ref, vta32_ref, kb_ref, vb_ref, kaa_ref, kab_ref, qta_ref, qtb_ref,
         vta_ref, vtb_ref, km_ref, ga_ref, gb_ref) = outs
        kta_ref[...] = k_a.T
        vt_a = v_a.T
        vta32_ref[...] = vt_a
        qt_a, qt_b, vt_b = q_a.T, q_b.T, v_b.T
        for r in range(x.shape[0] // MOBA_BLOCK):
            rows = slice(r * MOBA_BLOCK, (r + 1) * MOBA_BLOCK)
            n = pl.program_id(1) * (x.shape[0] // MOBA_BLOCK) + r
            for g, (ka_g, kb_g) in enumerate(zip(_augmented_keys(k_a[rows], n), _augmented_keys(k_b[rows], n))):
                kaa_ref[r, g] = ka_g
                kab_ref[r, g] = kb_g
            qta_ref[r] = qt_a[:, rows].astype(BF16)
            qtb_ref[r] = qt_b[:, rows].astype(BF16)
            vta_ref[r] = vt_a[:, rows].astype(BF16)
            vtb_ref[r] = vt_b[:, rows].astype(BF16)
            km_ref[r] = jnp.sum(k_a[rows], axis=0, keepdims=True) * (1.0 / MOBA_BLOCK)
        for hd in range(N_HEADS_B):
            lanes = slice(V_DIM_B * hd, V_DIM_B * (hd + 1))
            kb_ref[pl.ds(hd, x.shape[0], stride=N_HEADS_B), :] = k_b[:, lanes]
            vb_ref[pl.ds(hd, x.shape[0], stride=N_HEADS_B), :] = v_b[:, lanes]
    ga_ref[...] = g_a.astype(BF16)
    gb_ref[...] = g_b.astype(BF16)


def _proj(x, mod, nw, w_in, gains, bd, tm, transposed, gate=None):
    b, s, d = x.shape
    r = mod.shape[2]
    d_in = w_in.shape[1]
    row_spec = lambda w: pl.BlockSpec((None, tm, w), lambda i, t, *_: (i, t, 0))
    rows = lambda w, dt=F32: jax.ShapeDtypeStruct((b, s, w), dt)
    if transposed:
        nb, bpt = s // MOBA_BLOCK, tm // MOBA_BLOCK
        blk = lambda shp: pl.BlockSpec((None, bpt) + shp, lambda i, t, *_: (i, t, 0, 0))
        col_spec = pl.BlockSpec((None, WIDTH, tm), lambda i, t, *_: (i, 0, t))
        head_rows = pl.BlockSpec((None, tm * N_HEADS_B, V_DIM_B), lambda i, t, *_: (i, t, 0))
        k_aug = pl.BlockSpec((None, bpt, N_GROUPS, MOBA_BLOCK, LANES), lambda i, t, *_: (i, t, 0, 0, 0))
        out_specs = [col_spec] * 2 + [head_rows] * 2 + [k_aug] * 2 \
            + [blk((WIDTH, MOBA_BLOCK))] * 4 + [blk((1, WIDTH))] + [row_spec(d)] * 2
        out_shape = [jax.ShapeDtypeStruct((b, WIDTH, s), F32)] * 2 \
            + [jax.ShapeDtypeStruct((b, s * N_HEADS_B, V_DIM_B), F32)] * 2 \
            + [jax.ShapeDtypeStruct((b, nb, N_GROUPS, MOBA_BLOCK, LANES), BF16)] * 2 \
            + [jax.ShapeDtypeStruct((b, nb, WIDTH, MOBA_BLOCK), BF16)] * 4 \
            + [jax.ShapeDtypeStruct((b, nb, 1, WIDTH), F32)] + [rows(d, BF16)] * 2
    else:
        out_specs = [row_spec(WIDTH)] * 6 + [row_spec(d)] * 2
        out_shape = [rows(WIDTH)] * 6 + [rows(d, BF16)] * 2
    return _dense_call(
        lambda ins, outs: _proj_kernel(*ins, *outs, transposed=transposed), "mixer_proj", (b, s // tm),
        in_specs=[pl.BlockSpec((None, tm, d), lambda i, t, *_: (i, t, 0)),
                  pl.BlockSpec((None, N_MOD, r, d), lambda i, t, *_: (i, 0, 0, 0)),
                  _const_spec((1, d)), _const_spec((d, d_in)),
                  _const_spec((4, WIDTH)), _const_spec((MXU_DIM, MXU_DIM))],
        out_specs=out_specs, out_shape=out_shape, operands=(x, mod, nw, w_in, gains, bd), n_in=6, gate=gate)


FEATURE_ROWS = SUBLANES
MAX_KEY_BLOCKS = HEAD_DIM_A - FEATURE_ROWS


def _augmented_keys(k, n):
    lane = lax.broadcasted_iota(jnp.int32, (MOBA_BLOCK, LANES), 1)
    key = lax.broadcasted_iota(jnp.int32, (MOBA_BLOCK, LANES), 0).astype(F32)
    feat = jnp.where(lane == 64, 1.0, jnp.where(lane == 65, key,
                                                 jnp.where(lane == 64 + FEATURE_ROWS + n, 1.0, 0.0)))
    out = []
    for g in range(N_GROUPS):
        p, half = divmod(g, 2)
        pair = k[:, LANES * p:LANES * (p + 1)]
        k_g = pair if half == 0 else pltpu.roll(pair, 64, axis=1)
        out.append(jnp.where(lane < 64, k_g, feat).astype(BF16))
    return out


def _augmented_queries(qt, slopes, block_biases):
    qry = lax.broadcasted_iota(jnp.int32, (FEATURE_ROWS, MOBA_BLOCK), 1).astype(F32)
    row = lax.broadcasted_iota(jnp.int32, (FEATURE_ROWS, MOBA_BLOCK), 0)
    out = []
    for g, (slope, bias) in enumerate(zip(slopes, block_biases)):
        head = jnp.where(row == 0, -slope * qry, jnp.where(row == 1, slope, 0.0))
        pad = jnp.zeros((64 - FEATURE_ROWS - bias.shape[0], MOBA_BLOCK), F32)
        feat = jnp.concatenate([head, bias, pad], axis=0).astype(BF16)
        out.append(jnp.concatenate([qt[64 * g:64 * (g + 1), :], feat], axis=0))
    return out


def _pair_scores(k_aug, q_aug, p):
    return [_dot(k_aug[2 * p + half], q_aug[2 * p + half]) for half in range(2)]


def _attend_block(k_aug, vtb, q_aug, causal, m_all, l_all, acc_sc, v_rows, st_first, s0_sc, k_next):
    m_out, l_out = [], []
    n_pairs = N_GROUPS // 2
    st_next = _pair_scores(k_aug, q_aug, 0) if st_first is None else st_first
    for p in range(n_pairs):
        st_pair = st_next
        if p + 1 < n_pairs:
            st_next = _pair_scores(k_aug, q_aug, p + 1)
        else:
            nxt = _pair_scores(k_next, q_aug, 0)
            s0_sc[0] = nxt[0]
            s0_sc[1] = nxt[1]
        for half in range(2):
            g = 2 * p + half
            st = st_pair[half]
            if causal is not None:
                st = jnp.where(causal, st, NEG)
            m_old = m_all[g:g + 1, :]
            m_new = jnp.maximum(m_old, jnp.max(st, axis=0, keepdims=True))
            alpha = jnp.exp(m_old - m_new)
            pt = jnp.exp(st - m_new)
            l_out.append(alpha * l_all[g:g + 1, :] + jnp.sum(pt, axis=0, keepdims=True))
            rows = v_rows(g)
            nr = rows.stop - rows.start
            acc_rows = slice(g * nr, (g + 1) * nr)
            acc_sc[acc_rows, :] = alpha * acc_sc[acc_rows, :] + _dot(vtb[rows, :], pt.astype(BF16))
            m_out.append(m_new)
    return jnp.concatenate(m_out, axis=0), jnp.concatenate(l_out, axis=0)


def _attend_all_blocks(qi, k_ref, vt_ref, q_aug, acc_sc, s0_sc, v_rows):
    key_i = lax.broadcasted_iota(jnp.int32, (MOBA_BLOCK, MOBA_BLOCK), 0)
    qry_i = lax.broadcasted_iota(jnp.int32, (MOBA_BLOCK, MOBA_BLOCK), 1)
    acc_sc[...] = jnp.zeros(acc_sc.shape, F32)
    m0 = jnp.full((N_GROUPS, MOBA_BLOCK), NEG, F32)
    l0 = jnp.zeros((N_GROUPS, MOBA_BLOCK), F32)
    m1, l1 = _attend_block(k_ref.at[qi], vt_ref[qi], q_aug, key_i <= qry_i, m0, l0, acc_sc, v_rows,
                           None, s0_sc, k_ref.at[0])

    def past(n, carry):
        return _attend_block(k_ref.at[n], vt_ref[n], q_aug, None, *carry, acc_sc, v_rows,
                             [s0_sc[0], s0_sc[1]], s0_sc, k_ref.at[n + 1])

    _, l_fin = lax.fori_loop(0, qi, past, (m1, l1))
    return l_fin


def _block_distance(qi, nbp):
    blk_i = lax.broadcasted_iota(jnp.int32, (nbp, MOBA_BLOCK), 0)
    return blk_i, ((qi - blk_i) * MOBA_BLOCK).astype(F32)


def _moba_prompt_kernel(qt_ref, k_ref, vt_ref, km_ref, o_ref, acc_sc, s0_sc):
    qi = pl.program_id(1)
    nb = k_ref.shape[0]
    nbp = -(-nb // SUBLANES) * SUBLANES
    blk = MOBA_BLOCK
    slopes = _slopes(N_HEADS_A)
    qt = qt_ref[...]

    km = km_ref[...].astype(BF16)
    blk_i, blk_dist = _block_distance(qi, nbp)
    valid = blk_i[:nb] < qi
    biases = []
    for g in range(N_GROUPS):
        gate = _dot(km[:, 64 * g:64 * (g + 1)], qt[64 * g:64 * (g + 1), :])
        gate = jnp.where(valid, gate, NEG)
        rank = jnp.zeros((nb, blk), jnp.int32)
        for m in range(nb):
            gm = gate[m:m + 1, :]
            beats = (gm > gate) | ((gm == gate) & (m < blk_i[:nb]))
            rank = rank + beats.astype(jnp.int32)
        drop = jnp.where(valid & (rank < MOBA_TOPK), 0.0, NEG)
        if nbp > nb:
            drop = jnp.concatenate([drop, jnp.zeros((nbp - nb, blk), F32)], axis=0)
        biases.append(jnp.where(blk_i == qi, 0.0, drop - slopes[g] * blk_dist))

    q_aug = _augmented_queries(qt, slopes, biases)
    l_fin = _attend_all_blocks(qi, k_ref, vt_ref, q_aug, acc_sc, s0_sc,
                               lambda g: slice(64 * g, 64 * (g + 1)))
    parts = [acc_sc[64 * g:64 * (g + 1), :] * (1.0 / l_fin[g:g + 1, :]) for g in range(N_GROUPS)]
    o_ref[...] = jnp.concatenate(parts, axis=0).T.astype(BF16)


def _lambda(lam_ref, lam_init):
    a = jnp.sum(lam_ref[0:1, :] * lam_ref[1:2, :], axis=-1, keepdims=True)
    b = jnp.sum(lam_ref[2:3, :] * lam_ref[3:4, :], axis=-1, keepdims=True)
    return jnp.exp(a) - jnp.exp(b) + lam_init


def _diff_prompt_kernel(qt_ref, k_ref, vt_ref, lam_ref, subln_ref, o_ref, acc_sc, s0_sc, *, lam_init):
    qi = pl.program_id(1)
    nb = k_ref.shape[0]
    nbp = -(-nb // SUBLANES) * SUBLANES
    slopes = [s for s in _slopes(N_HEADS_B) for _ in range(2)]
    _, blk_dist = _block_distance(qi, nbp)
    q_aug = _augmented_queries(qt_ref[...], slopes, [-s * blk_dist for s in slopes])
    v_rows = lambda g: slice(V_DIM_B * (g // 2), V_DIM_B * (g // 2 + 1))
    l_fin = _attend_all_blocks(qi, k_ref, vt_ref, q_aug, acc_sc, s0_sc, v_rows)

    lam = _lambda(lam_ref, lam_init)
    parts = []
    for h in range(N_HEADS_B):
        o0 = acc_sc[V_DIM_B * (2 * h):V_DIM_B * (2 * h + 1), :] * (1.0 / l_fin[2 * h:2 * h + 1, :])
        o1 = acc_sc[V_DIM_B * (2 * h + 1):V_DIM_B * (2 * h + 2), :] * (1.0 / l_fin[2 * h + 1:2 * h + 2, :])
        o = o0 - lam * o1
        ms = jnp.mean(o * o, axis=0, keepdims=True)
        parts.append(o * lax.rsqrt(ms + RMS_EPS))
    o_t = jnp.concatenate(parts, axis=0).T
    o_ref[...] = (o_t * subln_ref[...] * (1.0 - lam_init)).astype(BF16)


def _prompt_attention(qt_a, k_aug_a, vt_a, km_a, qt_b, k_aug_b, vt_b, lam_vecs, subln_row, lam_init):
    b, nb = qt_a.shape[:2]
    assert nb <= MAX_KEY_BLOCKS
    blk = MOBA_BLOCK
    s = nb * blk
    q_spec = pl.BlockSpec((None, None, WIDTH, blk), lambda i, t: (i, t, 0, 0))
    k_spec = pl.BlockSpec((None, nb, N_GROUPS, blk, LANES), lambda i, t: (i, 0, 0, 0, 0))
    vt_spec = pl.BlockSpec((None, nb, WIDTH, blk), lambda i, t: (i, 0, 0, 0))
    o_spec = pl.BlockSpec((None, blk, WIDTH), lambda i, t: (i, t, 0))
    o_shape = jax.ShapeDtypeStruct((b, s, WIDTH), BF16)
    s0 = pltpu.VMEM((2, blk, blk), F32)
    o_a = pl.pallas_call(
        _moba_prompt_kernel,
        grid=(b, nb),
        in_specs=[q_spec, k_spec, vt_spec, pl.BlockSpec((None, nb, WIDTH), lambda i, t: (i, 0, 0))],
        out_specs=o_spec,
        out_shape=o_shape,
        scratch_shapes=[pltpu.VMEM((WIDTH, blk), F32), s0],
        compiler_params=_params("arbitrary", "arbitrary"),
        name="moba_prompt",
    )(qt_a, k_aug_a, vt_a, km_a)
    o_b = pl.pallas_call(
        functools.partial(_diff_prompt_kernel, lam_init=lam_init),
        grid=(b, nb),
        in_specs=[q_spec, k_spec, vt_spec, _const_spec((4, HEAD_DIM_B)), _const_spec((1, WIDTH))],
        out_specs=o_spec,
        out_shape=o_shape,
        scratch_shapes=[pltpu.VMEM((N_GROUPS * V_DIM_B, blk), F32), s0],
        compiler_params=_params("arbitrary", "arbitrary"),
        name="diff_prompt",
    )(qt_b, k_aug_b, vt_b, lam_vecs, subln_row)
    return o_a, o_b


def _mix_ffn_kernel(x_ref, oa_ref, ob_ref, ga_ref, gb_ref, mod_ref, wba_ref, wbd_ref, wout_ref,
                    nw_ref, wg_ref, wu_ref, wd_ref, o_ref, *, tf):
    y_a = _dot(oa_ref[...], wba_ref[...])
    y_b = _dot(ob_ref[...], wbd_ref[...])
    mixed = _dot((ga_ref[...].astype(F32) * y_a + gb_ref[...].astype(F32) * y_b).astype(BF16), wout_ref[...])
    x = x_ref[...] + mod_ref[5] * mixed
    o_ref[...] = _ffn_update(x, mod_ref[6], mod_ref[7], mod_ref[8], nw_ref[...],
                             wg_ref, wu_ref, wd_ref, tf)


def _mix_ffn(x, o_a, o_b, g_a, g_b, mod, w_ba, w_bd, w_out, nw, wg, wu, wd, tm):
    b, s, d = x.shape
    r = mod.shape[2]
    d_ff = wg.shape[1]
    tf = _pick_ff_tile(d_ff)
    row_spec = lambda w: pl.BlockSpec((None, tm, w), lambda i, t: (i, t, 0))
    return pl.pallas_call(
        functools.partial(_mix_ffn_kernel, tf=tf),
        grid=(b, s // tm),
        in_specs=[row_spec(d), row_spec(WIDTH), row_spec(WIDTH), row_spec(d), row_spec(d),
                  pl.BlockSpec((None, N_MOD, r, d), lambda i, t: (i, 0, 0, 0)),
                  _const_spec((WIDTH, d)), _const_spec((WIDTH, d)), _const_spec((d, d)),
                  _const_spec((1, d)),
                  _const_spec((d, d_ff)), _const_spec((d, d_ff)), _const_spec((d_ff, d))],
        out_specs=row_spec(d),
        out_shape=jax.ShapeDtypeStruct((b, s, d), F32),
        compiler_params=_params("arbitrary", "arbitrary"),
        name="mix_ffn",
    )(x, o_a, o_b, g_a, g_b, mod, w_ba, w_bd, w_out, nw, wg, wu, wd)


def _moba_select_kernel(gate_ref, sel_ref):
    gate = gate_ref[...]
    nblk = gate.shape[1]
    blk_i = lax.broadcasted_iota(jnp.int32, gate.shape, 1)
    rank = jnp.zeros(gate.shape, jnp.int32)
    for m in range(nblk):
        gm = gate[:, m:m + 1]
        beats = (gm > gate) | ((gm == gate) & (m < blk_i))
        rank = rank + beats.astype(jnp.int32)
    lane = lax.broadcasted_iota(jnp.int32, sel_ref.shape, 1)
    sel = jnp.zeros(sel_ref.shape, jnp.int32)
    for j in range(MOBA_TOPK):
        sel = jnp.where(lane == j, jnp.sum(jnp.where(rank == j, blk_i, 0), axis=1, keepdims=True), sel)
    sel_ref[...] = sel


def _moba_attend_kernel(pt_ref, sel_ref, q_ref, kn_ref, vn_ref, *rest, past_len, n_chunks):
    del pt_ref
    nt = N_HEADS_A * MOBA_TOPK * PAGES_PER_BLOCK
    s_refs, v_refs, o_ref = rest[:n_chunks], rest[n_chunks:n_chunks + nt], rest[n_chunks + nt]
    npg = GATE_PAGES_PER_STEP
    s = pl.program_id(0)
    lane = lax.broadcasted_iota(jnp.int32, (1, PAGE_SIZE), 1).astype(F32)
    slopes = _slopes(N_HEADS_A)
    for h in range(N_HEADS_A):
        q_h = q_ref[h]
        tiles = range(h * MOBA_TOPK * PAGES_PER_BLOCK, (h + 1) * MOBA_TOPK * PAGES_PER_BLOCK)
        rows = []
        for j in range(MOBA_TOPK):
            blk = sel_ref[s, j * N_HEADS_A + h]
            for i in range(PAGES_PER_BLOCK):
                page = PAGES_PER_BLOCK * blk + i
                raw = s_refs[0][jnp.clip(page, 0, npg - 1), h:h + 1, :]
                for c in range(1, n_chunks):
                    other = s_refs[c][jnp.clip(page - c * npg, 0, npg - 1), h:h + 1, :]
                    raw = jnp.where(page >= c * npg, other, raw)
                dist0 = (past_len - blk * MOBA_BLOCK - i * PAGE_SIZE).astype(F32)
                rows.append(raw - slopes[h] * (dist0 - lane))
        s_self = jnp.sum(q_h * kn_ref[h], axis=0, keepdims=True)
        m = s_self
        for r in rows:
            m = jnp.maximum(m, jnp.max(r, axis=1, keepdims=True))
        w_self = jnp.exp(s_self - m)
        l = w_self
        acc = jnp.zeros((HEAD_DIM_A, PAGE_SIZE), F32)
        for t, r in zip(tiles, rows):
            p = jnp.exp(r - m)
            l = l + jnp.sum(p, axis=1, keepdims=True)
            acc = acc + p * v_refs[t][...]
        o = jnp.sum(acc, axis=1, keepdims=True) + w_self * vn_ref[h]
        o_ref[h] = o * (1.0 / l)


def _moba_pool_tiles(pool):
    return jnp.transpose(pool, (0, 1, 3, 4, 2))


def _moba_decode(q, k_new, v_new, gates, raws, vt_pool, page_table, layer):
    db, n_pages = page_table.shape
    nblk = n_pages // PAGES_PER_BLOCK
    assert nblk >= MOBA_TOPK and len(raws) * GATE_PAGES_PER_STEP == n_pages
    col = lambda a: a[..., None]

    gate = jnp.concatenate(gates, axis=1)[..., 0]
    gate = gate.transpose(0, 2, 1).reshape(db * N_HEADS_A, nblk)
    sel = pl.pallas_call(
        _moba_select_kernel,
        grid=(1,),
        in_specs=[_const_spec(gate.shape)],
        out_specs=_const_spec((db * N_HEADS_A, LANES)),
        out_shape=jax.ShapeDtypeStruct((db * N_HEADS_A, LANES), jnp.int32),
        compiler_params=_params("arbitrary"),
        name="moba_select",
    )(gate)
    sel = sel[:, :MOBA_TOPK].reshape(db, N_HEADS_A, MOBA_TOPK).transpose(0, 2, 1).reshape(db, -1)

    def tile_spec(h, j, i):
        def index(s, pt, sel_):
            blk = jnp.clip(sel_[s, j * N_HEADS_A + h], 0, nblk - 1)
            return (layer, pt[s, PAGES_PER_BLOCK * blk + i], h, 0, 0)
        return pl.BlockSpec((None, None, None, HEAD_DIM_A, PAGE_SIZE), index)

    tile_specs = [tile_spec(h, j, i) for h in range(N_HEADS_A) for j in range(MOBA_TOPK)
                  for i in range(PAGES_PER_BLOCK)]
    seq = pl.BlockSpec((None, N_HEADS_A, HEAD_DIM_A, 1), lambda s, pt, sel_: (s, 0, 0, 0))
    raw_spec = pl.BlockSpec((None, GATE_PAGES_PER_STEP, N_HEADS_A, PAGE_SIZE), lambda s, pt, sel_: (s, 0, 0, 0))
    o = pl.pallas_call(
        functools.partial(_moba_attend_kernel, past_len=n_pages * PAGE_SIZE, n_chunks=len(raws)),
        grid_spec=pltpu.PrefetchScalarGridSpec(
            num_scalar_prefetch=2,
            grid=(db,),
            in_specs=[seq, seq, seq] + [raw_spec] * len(raws) + tile_specs,
            out_specs=seq,
        ),
        out_shape=jax.ShapeDtypeStruct((db, N_HEADS_A, HEAD_DIM_A, 1), F32),
        compiler_params=_params("arbitrary"),
        name="moba_attend",
    )(page_table, sel, col(q), col(k_new), col(v_new), *raws, *([vt_pool] * len(tile_specs)))
    return o[..., 0]


def _dot_nt(a, b):
    return lax.dot_general(a, b, (((1,), (1,)), ((), ())), preferred_element_type=F32)


def _diff_decode_kernel(pt_ref, q_ref, kn_ref, vn_ref, slope_ref, tbias_ref, lam_ref, subln_ref,
                        *rest, past_len, lam_init):
    del pt_ref
    npg = PAGES_PER_STEP
    k_refs, v_refs = rest[:npg], rest[npg:2 * npg]
    o_ref, m_sc, l_sc, acc_sc = rest[2 * npg:]
    j = pl.program_id(1)
    q8 = q_ref[...]
    slope8 = slope_ref[...]
    tbias = tbias_ref[...]
    lane_max = lambda x: jnp.max(x, axis=-1, keepdims=True)

    @pl.when(j == 0)
    def _():
        s_self = jnp.sum(_bf16_round(q8) * _bf16_round(kn_ref[...]), axis=-1, keepdims=True)
        m_sc[...] = jnp.broadcast_to(s_self, m_sc.shape)
        l_sc[...] = jnp.ones(l_sc.shape, F32)
        acc_sc[...] = vn_ref[...]

    q8b = q8.astype(BF16)
    scores = []
    for idx in range(npg):
        dist0 = (past_len - (j * npg + idx) * PAGE_SIZE).astype(F32)
        scores.append(_dot_nt(q8b, k_refs[idx][...].astype(BF16)) + tbias - slope8 * dist0)
    m_old = m_sc[...]
    m_step = lane_max(functools.reduce(jnp.maximum, scores))
    m_new = jnp.maximum(m_old, m_step)
    alpha = jnp.exp(m_old - m_new)
    m1 = m_new[:, 0:1]
    p_sum = jnp.zeros(scores[0].shape, F32)
    pv = jnp.zeros(acc_sc.shape, F32)
    for idx in range(npg):
        p = jnp.exp(scores[idx] - m1)
        p_sum = p_sum + p
        pv = pv + _dot(p.astype(BF16), v_refs[idx][...].astype(BF16))
    l_new = alpha * l_sc[...] + jnp.sum(p_sum, axis=-1, keepdims=True)
    acc_new = alpha * acc_sc[...] + pv
    m_sc[...] = m_new
    l_sc[...] = l_new
    acc_sc[...] = acc_new

    @pl.when(j == pl.num_programs(1) - 1)
    def _():
        o_c = acc_new * (1.0 / l_new)
        o = o_c - _lambda(lam_ref, lam_init) * pltpu.roll(o_c, N_HEADS_B, axis=0)
        ms = jnp.mean(o * o, axis=-1, keepdims=True)
        o_ref[...] = o * lax.rsqrt(ms + RMS_EPS) * subln_ref[...] * (1.0 - lam_init)


def _diff_decode(q, k_new, v_new, k_pool, v_pool, page_table, layer, lam_vecs, subln, lam_init):
    db, n_pages = page_table.shape
    depth, n_phys = k_pool.shape[:2]
    rows = PAGE_SIZE * N_HEADS_B
    pages = lambda pool: pool.reshape(depth * n_phys, rows, V_DIM_B)
    twice = lambda a: jnp.concatenate([a, a], axis=1)
    branch = (np.arange(V_DIM_B) // HEAD_DIM_B)[None, :] == np.arange(2)[:, None]
    q8 = jnp.concatenate([q * branch[0].astype(np.float32), q * branch[1].astype(np.float32)], axis=1)
    slopes = np.tile(np.asarray(_slopes(N_HEADS_B), np.float32), 2)
    col = np.arange(rows)
    own = (col % N_HEADS_B)[None, :] == (np.arange(SUBLANES) % N_HEADS_B)[:, None]
    tbias_np = np.where(own, slopes[:, None] * (col // N_HEADS_B)[None, :], NEG).astype(np.float32)
    seq = pl.BlockSpec((None, SUBLANES, V_DIM_B), lambda s, j, pt: (s, 0, 0))
    cst = lambda shp: pl.BlockSpec(shp, lambda s, j, pt: (0, 0))
    page = lambda i: pl.BlockSpec(
        (None, rows, V_DIM_B), lambda s, j, pt: (layer * n_phys + pt[s, j * PAGES_PER_STEP + i], 0, 0))
    page_specs = [page(i) for i in range(PAGES_PER_STEP)]
    stat = pltpu.VMEM((SUBLANES, LANES), F32)
    grid_spec = pltpu.PrefetchScalarGridSpec(
        num_scalar_prefetch=1,
        grid=(db, n_pages // PAGES_PER_STEP),
        in_specs=[seq, seq, seq, cst((SUBLANES, 1)), cst((SUBLANES, rows)), cst((4, HEAD_DIM_B)), cst((1, V_DIM_B))]
        + page_specs * 2,
        out_specs=seq,
        scratch_shapes=[stat, stat, stat],
    )
    o = pl.pallas_call(
        functools.partial(_diff_decode_kernel, past_len=n_pages * PAGE_SIZE, lam_init=lam_init),
        grid_spec=grid_spec,
        out_shape=jax.ShapeDtypeStruct((db, SUBLANES, V_DIM_B), F32),
        compiler_params=_params("arbitrary", "arbitrary"),
        name="diff_decode",
    )(page_table, q8, twice(k_new), twice(v_new), jnp.asarray(slopes[:, None]), jnp.asarray(tbias_np),
      lam_vecs, subln, *([pages(k_pool)] * PAGES_PER_STEP), *([pages(v_pool)] * PAGES_PER_STEP))
    return o[:, :N_HEADS_B]


def _block_diag_ones():
    i = np.arange(MXU_DIM) // 64
    return jnp.asarray((i[:, None] == i[None, :]).astype(np.float32), BF16)


def kernel(x_prompt, x_sample, cache_k_moba, cache_v_moba, cache_k_diff, cache_v_diff, page_table, c_prompt, c_sample, w_ada, b_ada, norm_ffn1, ffn1_w_gate, ffn1_w_up, ffn1_w_down, norm_mix, w_in, qn_moba, kn_moba, qn_diff, kn_diff, lambda_q1, lambda_k1, lambda_q2, lambda_k2, subln_diff, w_branch_moba, w_branch_diff, w_out, norm_ffn2, ffn2_w_gate, ffn2_w_up, ffn2_w_down):
    depth = w_ada.shape[0]
    b, s, d = x_prompt.shape
    db, t_new, _ = x_sample.shape
    assert t_new == 1 and s % TOKEN_TILE == 0 and db % SUBLANES == 0
    n_pages = page_table.shape[1]
    assert n_pages % PAGES_PER_STEP == 0
    bd = _block_diag_ones()
    tile8 = lambda v: jnp.tile(v, WIDTH // v.shape[0]).reshape(1, WIDTH)

    y_p, y_s = x_prompt, x_sample.reshape(1, db, d)
    rows_p, rows_s = [], []
    for l in range(depth):
        lam_init = _lambda_init(l)
        bf = lambda w: w[l].astype(BF16)
        row = lambda v: v[l].reshape(1, -1)
        mod = _ada(jnp.concatenate([c_prompt, c_sample], axis=0), w_ada[l], b_ada[l])
        mod_p = mod[:b].reshape(b, N_MOD, 1, d)
        mod_s = mod[b:].reshape(db, N_MOD, d).transpose(1, 0, 2).reshape(1, N_MOD, db, d)
        ffn1 = (row(norm_ffn1), bf(ffn1_w_gate), bf(ffn1_w_up), bf(ffn1_w_down))
        ffn2 = (row(norm_ffn2), bf(ffn2_w_gate), bf(ffn2_w_up), bf(ffn2_w_down))
        w_in_bf = bf(w_in)
        gains = jnp.concatenate([tile8(qn_moba[l]), tile8(kn_moba[l]), tile8(qn_diff[l]), tile8(kn_diff[l])], 0)
        lam_vecs = jnp.stack([lambda_q1[l], lambda_k1[l], lambda_q2[l], lambda_k2[l]])
        mix_w = (bf(w_branch_moba), bf(w_branch_diff), bf(w_out))

        (x1s,) = _ffn(y_s, mod_s, *ffn1, k0=0, tm=db)
        (q_as, k_as, v_as, q_bs, k_bs, v_bs, g_as, g_bs) = _proj(
            x1s, mod_s, row(norm_mix), w_in_bf, gains, bd, db, False)
        heads_a = lambda a: a.reshape(db, N_HEADS_A, HEAD_DIM_A)
        heads_b = lambda a: a.reshape(db, N_HEADS_B, V_DIM_B)

        kt_pool = _moba_pool_tiles(cache_k_moba)
        gate_stream = lambda first: _GateStream(page_table, heads_a(q_as)[..., None], kt_pool, l, first)
        x1, gate0, raw0 = _ffn(y_p, mod_p, *ffn1, k0=0, tm=TOKEN_TILE, gate=gate_stream(0))
        (kt_a, vt32_a, k_b, v_b, ka_a, ka_b, qt_a, qt_b, vt_a, vt_b, km_a, g_a, g_b, gate1, raw1) = _proj(
            x1, mod_p, row(norm_mix), w_in_bf, gains, bd, TOKEN_TILE, True, gate=gate_stream(GATE_PAGES_PER_STEP))
        o_a, o_b = _prompt_attention(qt_a, ka_a, vt_a, km_a.reshape(b, -1, WIDTH), qt_b, ka_b, vt_b,
                                     lam_vecs, tile8(subln_diff[l]), lam_init)
        y_p = _mix_ffn(x1, o_a, o_b, g_a, g_b, mod_p, *mix_w, *ffn2, tm=TOKEN_TILE)
        token_major = lambda a: a.reshape(b, N_HEADS_A, HEAD_DIM_A, s).transpose(0, 3, 1, 2)
        rows_p.append((token_major(kt_a), token_major(vt32_a), k_b, v_b))

        o_as = _moba_decode(heads_a(q_as), heads_a(k_as), heads_a(v_as), [gate0, gate1], [raw0, raw1],
                            _moba_pool_tiles(cache_v_moba), page_table, l)
        o_bs = _diff_decode(heads_b(q_bs), heads_b(k_bs), heads_b(v_bs), cache_k_diff, cache_v_diff,
                            page_table, l, lam_vecs, subln_diff[l].reshape(1, V_DIM_B), lam_init)
        as_rows = lambda a: a.reshape(1, db, WIDTH).astype(BF16)
        y_s = _mix_ffn(x1s, as_rows(o_as), as_rows(o_bs), g_as, g_bs, mod_s, *mix_w, *ffn2, tm=db)
        rows_s.append((k_as, v_as, k_bs, v_bs))

    def stack(rows, i, lead, heads, hd):
        return jnp.stack([r[i].reshape(lead + (heads, hd)) for r in rows])

    lp, ls = (b, s), (db, 1)
    return (y_p, y_s.reshape(db, 1, d),
            stack(rows_p, 0, lp, N_HEADS_A, HEAD_DIM_A), stack(rows_p, 1, lp, N_HEADS_A, HEAD_DIM_A),
            stack(rows_p, 2, lp, N_HEADS_B, V_DIM_B), stack(rows_p, 3, lp, N_HEADS_B, V_DIM_B),
            stack(rows_s, 0, ls, N_HEADS_A, HEAD_DIM_A), stack(rows_s, 1, ls, N_HEADS_A, HEAD_DIM_A),
            stack(rows_s, 2, ls, N_HEADS_B, V_DIM_B), stack(rows_s, 3, ls, N_HEADS_B, V_DIM_B))
```

```python
import functools
import math
from typing import Callable, NamedTuple

import jax
import jax.numpy as jnp
import numpy as np
from jax import lax
from jax.experimental import pallas as pl
from jax.experimental.pallas import tpu as pltpu

F32 = jnp.float32
BF16 = jnp.bfloat16

N_HEADS_A = 8
HEAD_DIM_A = 64
MOBA_BLOCK = 256
MOBA_TOPK = 3
N_HEADS_B = 4
HEAD_DIM_B = 64
V_DIM_B = 2 * HEAD_DIM_B
WIDTH = 512
N_GROUPS = WIDTH // 64
PAGE_SIZE = 128
PAGES_PER_BLOCK = MOBA_BLOCK // PAGE_SIZE
N_MOD = 9
RMS_EPS = 1e-6
QK_SCALE = 0.125
NEG = -1e30

LANES = 128
SUBLANES = 8
MXU_DIM = 256
VMEM_LIMIT_BYTES = 58 * 1024 * 1024

TOKEN_TILE = 512
PAGES_PER_STEP = 32
GATE_PAGES_PER_STEP = 32


def _slopes(n_heads):
    return [2.0 ** (-8.0 * (i + 1) / n_heads) for i in range(n_heads)]


def _lambda_init(layer):
    return 0.8 - 0.6 * math.exp(-0.3 * layer)


def _dot(a, b):
    return jnp.dot(a, b, preferred_element_type=F32)


def _bf16_round(x):
    return x.astype(BF16).astype(F32)


def _rms(x, w):
    ms = jnp.mean(x * x, axis=-1, keepdims=True)
    return x * lax.rsqrt(ms + RMS_EPS) * w


def _pick_ff_tile(d_ff):
    best = LANES
    for t in range(LANES, min(d_ff, 1408) + 1, LANES):
        if d_ff % t == 0:
            best = t
    return best


def _params(*sem):
    return pltpu.CompilerParams(dimension_semantics=sem, vmem_limit_bytes=VMEM_LIMIT_BYTES)


def _const_spec(shape):
    nd = len(shape)
    return pl.BlockSpec(shape, lambda *_: (0,) * nd)


def _ada_kernel(c_ref, w_ref, b_ref, o_ref):
    c = c_ref[...]
    s = c * jax.nn.sigmoid(c)
    o_ref[...] = _dot(s.astype(BF16), w_ref[...].astype(BF16)) + b_ref[...]


def _ada(c, w, b):
    m, d = c.shape
    n = w.shape[1]
    tn = 1024 if n % 1024 == 0 else n
    return pl.pallas_call(
        _ada_kernel,
        grid=(n // tn,),
        in_specs=[pl.BlockSpec((m, d), lambda j: (0, 0)),
                  pl.BlockSpec((d, tn), lambda j: (0, j)),
                  pl.BlockSpec((1, tn), lambda j: (0, j))],
        out_specs=pl.BlockSpec((m, tn), lambda j: (0, j)),
        out_shape=jax.ShapeDtypeStruct((m, n), F32),
        compiler_params=_params("arbitrary"),
        name="ada_mod",
    )(c, w, b.reshape(1, n))


def _ffn_update(x, shift, scale, gate, nw, wg_ref, wu_ref, wd_ref, tf):
    h = (_rms(x, nw) * (1.0 + scale) + shift).astype(BF16)
    d_ff = wg_ref.shape[1]
    acc = jnp.zeros(x.shape, F32)
    for j in range(d_ff // tf):
        g = _dot(h, wg_ref[:, j * tf:(j + 1) * tf])
        u = _dot(h, wu_ref[:, j * tf:(j + 1) * tf])
        a = (g * jax.nn.sigmoid(g) * u).astype(BF16)
        acc = acc + _dot(a, wd_ref[j * tf:(j + 1) * tf, :])
    return x + 0.5 * gate * acc


class _Rider(NamedTuple):
    prefetch: tuple
    in_specs: list
    out_specs: list
    out_shape: list
    operands: tuple
    body: Callable


def _gate_pages(q_ref, k_refs, gate_ref, s_ref):
    q = q_ref[...]
    q_bf = _bf16_round(q)
    for bi in range(len(k_refs) // PAGES_PER_BLOCK):
        pages = [k_refs[bi * PAGES_PER_BLOCK + i][...] for i in range(PAGES_PER_BLOCK)]
        for i, k_page in enumerate(pages):
            s_ref[bi * PAGES_PER_BLOCK + i] = jnp.sum(k_page * q, axis=1)
        ksum = jnp.sum(functools.reduce(jnp.add, pages), axis=-1, keepdims=True)
        gate_ref[bi] = jnp.sum(_bf16_round(ksum * (1.0 / MOBA_BLOCK)) * q_bf, axis=1)


def _gate_rider(page_table, cols, kt_pool, layer, first_page, steps_per_batch):
    db = page_table.shape[0]
    npg = GATE_PAGES_PER_STEP
    seq = lambda i, t: i * steps_per_batch + t
    page = lambda k: pl.BlockSpec(
        (None, None, N_HEADS_A, HEAD_DIM_A, PAGE_SIZE),
        lambda i, t, pt: (layer, pt[seq(i, t), first_page + k], 0, 0, 0))
    per_seq = lambda shp: pl.BlockSpec((None,) + shp, lambda i, t, pt: (seq(i, t), 0, 0, 0))
    q_spec = pl.BlockSpec((None, None, N_HEADS_A, HEAD_DIM_A, 1), lambda i, t, pt: (seq(i, t), 0, 0, 0, 0))
    return _Rider(
        prefetch=(page_table,),
        in_specs=[q_spec] + [page(k) for k in range(npg)],
        out_specs=[per_seq((npg // PAGES_PER_BLOCK, N_HEADS_A, 1)), per_seq((npg, N_HEADS_A, PAGE_SIZE))],
        out_shape=[jax.ShapeDtypeStruct((db, npg // PAGES_PER_BLOCK, N_HEADS_A, 1), F32),
                   jax.ShapeDtypeStruct((db, npg, N_HEADS_A, PAGE_SIZE), F32)],
        operands=(cols,) + (kt_pool,) * npg,
        body=lambda pf, ins, outs, s: _gate_pages(ins[0], ins[1:], *outs))


def _dense_call(body, name, grid, in_specs, out_specs, out_shape, operands, rider=None):
    n_in, n_out = len(in_specs), len(out_specs)
    if rider is None:
        def kernel_fn(*refs):
            body(refs[:n_in], refs[n_in:])
        return pl.pallas_call(kernel_fn, grid=grid, in_specs=in_specs, out_specs=out_specs, out_shape=out_shape,
                              compiler_params=_params("arbitrary", "arbitrary"), name=name)(*operands)
    n_pf = len(rider.prefetch)
    assert rider.out_shape[0].shape[0] == grid[0] * grid[1]

    def kernel_fn(*refs):
        pf, refs = refs[:n_pf], refs[n_pf:]
        n_all_in = n_in + len(rider.in_specs)
        ins, outs = refs[:n_all_in], refs[n_all_in:]
        body(ins[:n_in], outs[:n_out])
        rider.body(pf, ins[n_in:], outs[n_out:], pl.program_id(0) * grid[1] + pl.program_id(1))

    return pl.pallas_call(
        kernel_fn,
        grid_spec=pltpu.PrefetchScalarGridSpec(
            num_scalar_prefetch=n_pf, grid=grid, in_specs=list(in_specs) + rider.in_specs,
            out_specs=list(out_specs) + rider.out_specs),
        out_shape=list(out_shape) + rider.out_shape,
        compiler_params=_params("arbitrary", "arbitrary"),
        name=name,
    )(*rider.prefetch, *operands, *rider.operands)


def _ffn(x, mod, nw, wg, wu, wd, k0, tm, rider=None):
    b, s, d = x.shape
    r = mod.shape[2]
    d_ff = wg.shape[1]
    tf = _pick_ff_tile(d_ff)

    def body(ins, outs):
        x_ref, mod_ref, nw_ref, wg_ref, wu_ref, wd_ref = ins
        outs[0][...] = _ffn_update(x_ref[...], mod_ref[k0], mod_ref[k0 + 1], mod_ref[k0 + 2],
                                   nw_ref[...], wg_ref, wu_ref, wd_ref, tf)

    return _dense_call(
        body, "ffn", (b, s // tm),
        in_specs=[pl.BlockSpec((None, tm, d), lambda i, t, *_: (i, t, 0)),
                  pl.BlockSpec((None, N_MOD, r, d), lambda i, t, *_: (i, 0, 0, 0)),
                  _const_spec((1, d)),
                  _const_spec((d, d_ff)), _const_spec((d, d_ff)), _const_spec((d_ff, d))],
        out_specs=[pl.BlockSpec((None, tm, d), lambda i, t, *_: (i, t, 0))],
        out_shape=[jax.ShapeDtypeStruct((b, s, d), F32)],
        operands=(x, mod, nw, wg, wu, wd), rider=rider)


def _head_norm(seg, gain_row, bd):
    sq = (seg * seg).astype(BF16)
    parts = [_dot(sq[:, c * MXU_DIM:(c + 1) * MXU_DIM], bd) for c in range(WIDTH // MXU_DIM)]
    ms = jnp.concatenate(parts, axis=1) * (1.0 / 64.0)
    return seg * lax.rsqrt(ms + RMS_EPS) * gain_row


def _proj_kernel(x_ref, mod_ref, nw_ref, win_ref, gains_ref, bd_ref, *outs, transposed):
    x = x_ref[...]
    d = x.shape[1]
    h = (_rms(x, nw_ref[...]) * (1.0 + mod_ref[4]) + mod_ref[3]).astype(BF16)
    bd = bd_ref[...]

    def seg(j):
        return _dot(h, win_ref[:, j * WIDTH:(j + 1) * WIDTH])

    q_a = _head_norm(seg(0), gains_ref[0:1, :], bd) * QK_SCALE
    k_a = _head_norm(seg(1), gains_ref[1:2, :], bd)
    v_a = seg(2)
    q_b = _head_norm(seg(3), gains_ref[2:3, :], bd) * QK_SCALE
    k_b = _head_norm(seg(4), gains_ref[3:4, :], bd)
    v_b = seg(5)
    g0 = 6 * WIDTH
    g_a = jax.nn.sigmoid(_dot(h, win_ref[:, g0:g0 + d]))
    g_b = jax.nn.sigmoid(_dot(h, win_ref[:, g0 + d:g0 + 2 * d]))

    if not transposed:
        (qa_ref, ka_ref, va_ref, qb_ref, kb_ref, vb_ref, ga_ref, gb_ref) = outs
        qa_ref[...] = q_a
        ka_ref[...] = k_a
        va_ref[...] = v_a
        qb_ref[...] = q_b
        kb_ref[...] = k_b
        vb_ref[...] = v_b
    else:
        (kta_ref, vta32_ref, kb_ref, vb_ref, kaa_ref, kab_ref, qta_ref, qtb_ref,
         vta_ref, vtb_ref, km_ref, ga_ref, gb_ref) = outs
        kta_ref[...] = k_a.T
        vt_a = v_a.T
        vta32_ref[...] = vt_a
        qt_a, qt_b, vt_b = q_a.T, q_b.T, v_b.T
        for r in range(x.shape[0] // MOBA_BLOCK):
            rows = slice(r * MOBA_BLOCK, (r + 1) * MOBA_BLOCK)
            n = pl.program_id(1) * (x.shape[0] // MOBA_BLOCK) + r
            for g, (ka_g, kb_g) in enumerate(zip(_augmented_keys(k_a[rows], n), _augmented_keys(k_b[rows], n))):
                kaa_ref[r, g] = ka_g
                kab_ref[r, g] = kb_g
            qta_ref[r] = qt_a[:, rows].astype(BF16)
            qtb_ref[r] = qt_b[:, rows].astype(BF16)
            vta_ref[r] = vt_a[:, rows].astype(BF16)
            vtb_ref[r] = vt_b[:, rows].astype(BF16)
            km_ref[r] = jnp.sum(k_a[rows], axis=0, keepdims=True) * (1.0 / MOBA_BLOCK)
        for hd in range(N_HEADS_B):
            lanes = slice(V_DIM_B * hd, V_DIM_B * (hd + 1))
            kb_ref[pl.ds(hd, x.shape[0], stride=N_HEADS_B), :] = k_b[:, lanes]
            vb_ref[pl.ds(hd, x.shape[0], stride=N_HEADS_B), :] = v_b[:, lanes]
    ga_ref[...] = g_a.astype(BF16)
    gb_ref[...] = g_b.astype(BF16)


def _proj(x, mod, nw, w_in, gains, bd, tm, transposed, rider=None):
    b, s, d = x.shape
    r = mod.shape[2]
    d_in = w_in.shape[1]
    row_spec = lambda w: pl.BlockSpec((None, tm, w), lambda i, t, *_: (i, t, 0))
    rows = lambda w, dt=F32: jax.ShapeDtypeStruct((b, s, w), dt)
    if transposed:
        nb, bpt = s // MOBA_BLOCK, tm // MOBA_BLOCK
        blk = lambda shp: pl.BlockSpec((None, bpt) + shp, lambda i, t, *_: (i, t, 0, 0))
        col_spec = pl.BlockSpec((None, WIDTH, tm), lambda i, t, *_: (i, 0, t))
        head_rows = pl.BlockSpec((None, tm * N_HEADS_B, V_DIM_B), lambda i, t, *_: (i, t, 0))
        k_aug = pl.BlockSpec((None, bpt, N_GROUPS, MOBA_BLOCK, LANES), lambda i, t, *_: (i, t, 0, 0, 0))
        out_specs = [col_spec] * 2 + [head_rows] * 2 + [k_aug] * 2 \
            + [blk((WIDTH, MOBA_BLOCK))] * 4 + [blk((1, WIDTH))] + [row_spec(d)] * 2
        out_shape = [jax.ShapeDtypeStruct((b, WIDTH, s), F32)] * 2 \
            + [jax.ShapeDtypeStruct((b, s * N_HEADS_B, V_DIM_B), F32)] * 2 \
            + [jax.ShapeDtypeStruct((b, nb, N_GROUPS, MOBA_BLOCK, LANES), BF16)] * 2 \
            + [jax.ShapeDtypeStruct((b, nb, WIDTH, MOBA_BLOCK), BF16)] * 4 \
            + [jax.ShapeDtypeStruct((b, nb, 1, WIDTH), F32)] + [rows(d, BF16)] * 2
    else:
        out_specs = [row_spec(WIDTH)] * 6 + [row_spec(d)] * 2
        out_shape = [rows(WIDTH)] * 6 + [rows(d, BF16)] * 2
    return _dense_call(
        lambda ins, outs: _proj_kernel(*ins, *outs, transposed=transposed), "mixer_proj", (b, s // tm),
        in_specs=[pl.BlockSpec((None, tm, d), lambda i, t, *_: (i, t, 0)),
                  pl.BlockSpec((None, N_MOD, r, d), lambda i, t, *_: (i, 0, 0, 0)),
                  _const_spec((1, d)), _const_spec((d, d_in)),
                  _const_spec((4, WIDTH)), _const_spec((MXU_DIM, MXU_DIM))],
        out_specs=out_specs, out_shape=out_shape, operands=(x, mod, nw, w_in, gains, bd), rider=rider)


FEATURE_ROWS = SUBLANES
MAX_KEY_BLOCKS = HEAD_DIM_A - FEATURE_ROWS


def _augmented_keys(k, n):
    lane = lax.broadcasted_iota(jnp.int32, (MOBA_BLOCK, LANES), 1)
    key = lax.broadcasted_iota(jnp.int32, (MOBA_BLOCK, LANES), 0).astype(F32)
    feat = jnp.where(lane == 64, 1.0, jnp.where(lane == 65, key,
                                                 jnp.where(lane == 64 + FEATURE_ROWS + n, 1.0, 0.0)))
    out = []
    for g in range(N_GROUPS):
        p, half = divmod(g, 2)
        pair = k[:, LANES * p:LANES * (p + 1)]
        k_g = pair if half == 0 else pltpu.roll(pair, 64, axis=1)
        out.append(jnp.where(lane < 64, k_g, feat).astype(BF16))
    return out


def _augmented_queries(qt, slopes, block_biases):
    qry = lax.broadcasted_iota(jnp.int32, (FEATURE_ROWS, MOBA_BLOCK), 1).astype(F32)
    row = lax.broadcasted_iota(jnp.int32, (FEATURE_ROWS, MOBA_BLOCK), 0)
    out = []
    for g, (slope, bias) in enumerate(zip(slopes, block_biases)):
        head = jnp.where(row == 0, -slope * qry, jnp.where(row == 1, slope, 0.0))
        pad = jnp.zeros((64 - FEATURE_ROWS - bias.shape[0], MOBA_BLOCK), F32)
        feat = jnp.concatenate([head, bias, pad], axis=0).astype(BF16)
        out.append(jnp.concatenate([qt[64 * g:64 * (g + 1), :], feat], axis=0))
    return out


def _pair_scores(k_aug, q_aug, p):
    return [_dot(k_aug[2 * p + half], q_aug[2 * p + half]) for half in range(2)]


def _attend_block(k_aug, vtb, q_aug, causal, m_all, l_all, acc_sc, v_rows, st_first, s0_sc, k_next):
    m_out, l_out = [], []
    n_pairs = N_GROUPS // 2
    st_next = _pair_scores(k_aug, q_aug, 0) if st_first is None else st_first
    for p in range(n_pairs):
        st_pair = st_next
        if p + 1 < n_pairs:
            st_next = _pair_scores(k_aug, q_aug, p + 1)
        else:
            nxt = _pair_scores(k_next, q_aug, 0)
            s0_sc[0] = nxt[0]
            s0_sc[1] = nxt[1]
        for half in range(2):
            g = 2 * p + half
            st = st_pair[half]
            if causal is not None:
                st = jnp.where(causal, st, NEG)
            m_old = m_all[g:g + 1, :]
            m_new = jnp.maximum(m_old, jnp.max(st, axis=0, keepdims=True))
            alpha = jnp.exp(m_old - m_new)
            pt = jnp.exp(st - m_new)
            l_out.append(alpha * l_all[g:g + 1, :] + jnp.sum(pt, axis=0, keepdims=True))
            rows = v_rows(g)
            nr = rows.stop - rows.start
            acc_rows = slice(g * nr, (g + 1) * nr)
            acc_sc[acc_rows, :] = alpha * acc_sc[acc_rows, :] + _dot(vtb[rows, :], pt.astype(BF16))
            m_out.append(m_new)
    return jnp.concatenate(m_out, axis=0), jnp.concatenate(l_out, axis=0)


def _attend_all_blocks(qi, k_ref, vt_ref, q_aug, acc_sc, s0_sc, v_rows):
    key_i = lax.broadcasted_iota(jnp.int32, (MOBA_BLOCK, MOBA_BLOCK), 0)
    qry_i = lax.broadcasted_iota(jnp.int32, (MOBA_BLOCK, MOBA_BLOCK), 1)
    acc_sc[...] = jnp.zeros(acc_sc.shape, F32)
    m0 = jnp.full((N_GROUPS, MOBA_BLOCK), NEG, F32)
    l0 = jnp.zeros((N_GROUPS, MOBA_BLOCK), F32)
    m1, l1 = _attend_block(k_ref.at[qi], vt_ref[qi], q_aug, key_i <= qry_i, m0, l0, acc_sc, v_rows,
                           None, s0_sc, k_ref.at[0])

    def past(n, carry):
        return _attend_block(k_ref.at[n], vt_ref[n], q_aug, None, *carry, acc_sc, v_rows,
                             [s0_sc[0], s0_sc[1]], s0_sc, k_ref.at[n + 1])

    _, l_fin = lax.fori_loop(0, qi, past, (m1, l1))
    return l_fin


def _block_distance(qi, nbp):
    blk_i = lax.broadcasted_iota(jnp.int32, (nbp, MOBA_BLOCK), 0)
    return blk_i, ((qi - blk_i) * MOBA_BLOCK).astype(F32)


def _moba_prompt_kernel(qt_ref, k_ref, vt_ref, km_ref, o_ref, acc_sc, s0_sc):
    qi = pl.program_id(1)
    nb = k_ref.shape[0]
    nbp = -(-nb // SUBLANES) * SUBLANES
    blk = MOBA_BLOCK
    slopes = _slopes(N_HEADS_A)
    qt = qt_ref[...]

    km = km_ref[...].astype(BF16)
    blk_i, blk_dist = _block_distance(qi, nbp)
    valid = blk_i[:nb] < qi
    biases = []
    for g in range(N_GROUPS):
        gate = _dot(km[:, 64 * g:64 * (g + 1)], qt[64 * g:64 * (g + 1), :])
        gate = jnp.where(valid, gate, NEG)
        rank = jnp.zeros((nb, blk), jnp.int32)
        for m in range(nb):
            gm = gate[m:m + 1, :]
            beats = (gm > gate) | ((gm == gate) & (m < blk_i[:nb]))
            rank = rank + beats.astype(jnp.int32)
        drop = jnp.where(valid & (rank < MOBA_TOPK), 0.0, NEG)
        if nbp > nb:
            drop = jnp.concatenate([drop, jnp.zeros((nbp - nb, blk), F32)], axis=0)
        biases.append(jnp.where(blk_i == qi, 0.0, drop - slopes[g] * blk_dist))

    q_aug = _augmented_queries(qt, slopes, biases)
    l_fin = _attend_all_blocks(qi, k_ref, vt_ref, q_aug, acc_sc, s0_sc,
                               lambda g: slice(64 * g, 64 * (g + 1)))
    parts = [acc_sc[64 * g:64 * (g + 1), :] * (1.0 / l_fin[g:g + 1, :]) for g in range(N_GROUPS)]
    o_ref[...] = jnp.concatenate(parts, axis=0).T.astype(BF16)


def _lambda(lam_ref, lam_init):
    a = jnp.sum(lam_ref[0:1, :] * lam_ref[1:2, :], axis=-1, keepdims=True)
    b = jnp.sum(lam_ref[2:3, :] * lam_ref[3:4, :], axis=-1, keepdims=True)
    return jnp.exp(a) - jnp.exp(b) + lam_init


def _diff_prompt_kernel(qt_ref, k_ref, vt_ref, lam_ref, subln_ref, o_ref, acc_sc, s0_sc, *, lam_init):
    qi = pl.program_id(1)
    nb = k_ref.shape[0]
    nbp = -(-nb // SUBLANES) * SUBLANES
    slopes = [s for s in _slopes(N_HEADS_B) for _ in range(2)]
    _, blk_dist = _block_distance(qi, nbp)
    q_aug = _augmented_queries(qt_ref[...], slopes, [-s * blk_dist for s in slopes])
    v_rows = lambda g: slice(V_DIM_B * (g // 2), V_DIM_B * (g // 2 + 1))
    l_fin = _attend_all_blocks(qi, k_ref, vt_ref, q_aug, acc_sc, s0_sc, v_rows)

    lam = _lambda(lam_ref, lam_init)
    parts = []
    for h in range(N_HEADS_B):
        o0 = acc_sc[V_DIM_B * (2 * h):V_DIM_B * (2 * h + 1), :] * (1.0 / l_fin[2 * h:2 * h + 1, :])
        o1 = acc_sc[V_DIM_B * (2 * h + 1):V_DIM_B * (2 * h + 2), :] * (1.0 / l_fin[2 * h + 1:2 * h + 2, :])
        o = o0 - lam * o1
        ms = jnp.mean(o * o, axis=0, keepdims=True)
        parts.append(o * lax.rsqrt(ms + RMS_EPS))
    o_t = jnp.concatenate(parts, axis=0).T
    o_ref[...] = (o_t * subln_ref[...] * (1.0 - lam_init)).astype(BF16)


def _prompt_attention(qt_a, k_aug_a, vt_a, km_a, qt_b, k_aug_b, vt_b, lam_vecs, subln_row, lam_init):
    b, nb = qt_a.shape[:2]
    assert nb <= MAX_KEY_BLOCKS
    blk = MOBA_BLOCK
    s = nb * blk
    q_spec = pl.BlockSpec((None, None, WIDTH, blk), lambda i, t: (i, t, 0, 0))
    k_spec = pl.BlockSpec((None, nb, N_GROUPS, blk, LANES), lambda i, t: (i, 0, 0, 0, 0))
    vt_spec = pl.BlockSpec((None, nb, WIDTH, blk), lambda i, t: (i, 0, 0, 0))
    o_spec = pl.BlockSpec((None, blk, WIDTH), lambda i, t: (i, t, 0))
    o_shape = jax.ShapeDtypeStruct((b, s, WIDTH), BF16)
    s0 = pltpu.VMEM((2, blk, blk), F32)
    o_a = pl.pallas_call(
        _moba_prompt_kernel,
        grid=(b, nb),
        in_specs=[q_spec, k_spec, vt_spec, pl.BlockSpec((None, nb, WIDTH), lambda i, t: (i, 0, 0))],
        out_specs=o_spec,
        out_shape=o_shape,
        scratch_shapes=[pltpu.VMEM((WIDTH, blk), F32), s0],
        compiler_params=_params("arbitrary", "arbitrary"),
        name="moba_prompt",
    )(qt_a, k_aug_a, vt_a, km_a)
    o_b = pl.pallas_call(
        functools.partial(_diff_prompt_kernel, lam_init=lam_init),
        grid=(b, nb),
        in_specs=[q_spec, k_spec, vt_spec, _const_spec((4, HEAD_DIM_B)), _const_spec((1, WIDTH))],
        out_specs=o_spec,
        out_shape=o_shape,
        scratch_shapes=[pltpu.VMEM((N_GROUPS * V_DIM_B, blk), F32), s0],
        compiler_params=_params("arbitrary", "arbitrary"),
        name="diff_prompt",
    )(qt_b, k_aug_b, vt_b, lam_vecs, subln_row)
    return o_a, o_b


def _mix_ffn_kernel(x_ref, oa_ref, ob_ref, ga_ref, gb_ref, mod_ref, wba_ref, wbd_ref, wout_ref,
                    nw_ref, wg_ref, wu_ref, wd_ref, o_ref, *, tf):
    y_a = _dot(oa_ref[...], wba_ref[...])
    y_b = _dot(ob_ref[...], wbd_ref[...])
    mixed = _dot((ga_ref[...].astype(F32) * y_a + gb_ref[...].astype(F32) * y_b).astype(BF16), wout_ref[...])
    x = x_ref[...] + mod_ref[5] * mixed
    o_ref[...] = _ffn_update(x, mod_ref[6], mod_ref[7], mod_ref[8], nw_ref[...],
                             wg_ref, wu_ref, wd_ref, tf)


def _mix_ffn(x, o_a, o_b, g_a, g_b, mod, w_ba, w_bd, w_out, nw, wg, wu, wd, tm, rider=None):
    b, s, d = x.shape
    r = mod.shape[2]
    d_ff = wg.shape[1]
    tf = _pick_ff_tile(d_ff)
    row_spec = lambda w: pl.BlockSpec((None, tm, w), lambda i, t, *_: (i, t, 0))
    return _dense_call(
        lambda ins, outs: _mix_ffn_kernel(*ins, *outs, tf=tf), "mix_ffn", (b, s // tm),
        in_specs=[row_spec(d), row_spec(WIDTH), row_spec(WIDTH), row_spec(d), row_spec(d),
                  pl.BlockSpec((None, N_MOD, r, d), lambda i, t, *_: (i, 0, 0, 0)),
                  _const_spec((WIDTH, d)), _const_spec((WIDTH, d)), _const_spec((d, d)),
                  _const_spec((1, d)),
                  _const_spec((d, d_ff)), _const_spec((d, d_ff)), _const_spec((d_ff, d))],
        out_specs=[row_spec(d)], out_shape=[jax.ShapeDtypeStruct((b, s, d), F32)],
        operands=(x, o_a, o_b, g_a, g_b, mod, w_ba, w_bd, w_out, nw, wg, wu, wd), rider=rider)


def _moba_select_kernel(gate_ref, sel_ref):
    gate = gate_ref[...]
    nblk = gate.shape[1]
    blk_i = lax.broadcasted_iota(jnp.int32, gate.shape, 1)
    rank = jnp.zeros(gate.shape, jnp.int32)
    for m in range(nblk):
        gm = gate[:, m:m + 1]
        beats = (gm > gate) | ((gm == gate) & (m < blk_i))
        rank = rank + beats.astype(jnp.int32)
    lane = lax.broadcasted_iota(jnp.int32, sel_ref.shape, 1)
    sel = jnp.zeros(sel_ref.shape, jnp.int32)
    for j in range(MOBA_TOPK):
        sel = jnp.where(lane == j, jnp.sum(jnp.where(rank == j, blk_i, 0), axis=1, keepdims=True), sel)
    sel_ref[...] = sel


def _moba_attend(sel_ref, cols_ref, s_refs, v_refs, o_ref, s, past_len):
    npg = GATE_PAGES_PER_STEP
    lane = lax.broadcasted_iota(jnp.int32, (1, PAGE_SIZE), 1).astype(F32)
    slopes = _slopes(N_HEADS_A)
    for h in range(N_HEADS_A):
        q_h = cols_ref[0, h]
        tiles = range(h * MOBA_TOPK * PAGES_PER_BLOCK, (h + 1) * MOBA_TOPK * PAGES_PER_BLOCK)
        rows = []
        for j in range(MOBA_TOPK):
            blk = sel_ref[s, j * N_HEADS_A + h]
            for i in range(PAGES_PER_BLOCK):
                page = PAGES_PER_BLOCK * blk + i
                raw = s_refs[0][jnp.clip(page, 0, npg - 1), h:h + 1, :]
                for c in range(1, len(s_refs)):
                    other = s_refs[c][jnp.clip(page - c * npg, 0, npg - 1), h:h + 1, :]
                    raw = jnp.where(page >= c * npg, other, raw)
                dist0 = (past_len - blk * MOBA_BLOCK - i * PAGE_SIZE).astype(F32)
                rows.append(raw - slopes[h] * (dist0 - lane))
        s_self = jnp.sum(q_h * cols_ref[1, h], axis=0, keepdims=True)
        m = s_self
        for r in rows:
            m = jnp.maximum(m, jnp.max(r, axis=1, keepdims=True))
        w_self = jnp.exp(s_self - m)
        l = w_self
        acc = jnp.zeros((HEAD_DIM_A, PAGE_SIZE), F32)
        for t, r in zip(tiles, rows):
            p = jnp.exp(r - m)
            l = l + jnp.sum(p, axis=1, keepdims=True)
            acc = acc + p * v_refs[t][...]
        o = jnp.sum(acc, axis=1, keepdims=True) + w_self * cols_ref[2, h]
        o_ref[h] = o * (1.0 / l)


def _moba_pool_tiles(pool):
    return jnp.transpose(pool, (0, 1, 3, 4, 2))


def _moba_select(gates):
    gate = jnp.concatenate(gates, axis=1)[..., 0]
    db, nblk = gate.shape[:2]
    assert nblk >= MOBA_TOPK
    gate = gate.transpose(0, 2, 1).reshape(db * N_HEADS_A, nblk)
    sel = pl.pallas_call(
        _moba_select_kernel,
        grid=(1,),
        in_specs=[_const_spec(gate.shape)],
        out_specs=_const_spec((db * N_HEADS_A, LANES)),
        out_shape=jax.ShapeDtypeStruct((db * N_HEADS_A, LANES), jnp.int32),
        compiler_params=_params("arbitrary"),
        name="moba_select",
    )(gate)
    return sel[:, :MOBA_TOPK].reshape(db, N_HEADS_A, MOBA_TOPK).transpose(0, 2, 1).reshape(db, -1)


def _attend_rider(page_table, sel, cols, raws, vt_pool, layer, steps_per_batch):
    db, n_pages = page_table.shape
    nblk = n_pages // PAGES_PER_BLOCK
    assert len(raws) * GATE_PAGES_PER_STEP == n_pages
    seq = lambda i, t: i * steps_per_batch + t

    def tile_spec(h, j, i_page):
        def index(i, t, pt, sel_):
            blk = jnp.clip(sel_[seq(i, t), j * N_HEADS_A + h], 0, nblk - 1)
            return (layer, pt[seq(i, t), PAGES_PER_BLOCK * blk + i_page], h, 0, 0)
        return pl.BlockSpec((None, None, None, HEAD_DIM_A, PAGE_SIZE), index)

    tile_specs = [tile_spec(h, j, i_page) for h in range(N_HEADS_A) for j in range(MOBA_TOPK)
                  for i_page in range(PAGES_PER_BLOCK)]
    cols_spec = pl.BlockSpec((None, 3, N_HEADS_A, HEAD_DIM_A, 1), lambda i, t, pt, sel_: (seq(i, t), 0, 0, 0, 0))
    raw_spec = pl.BlockSpec((None, GATE_PAGES_PER_STEP, N_HEADS_A, PAGE_SIZE),
                            lambda i, t, pt, sel_: (seq(i, t), 0, 0, 0))
    nc = len(raws)
    return _Rider(
        prefetch=(page_table, sel),
        in_specs=[cols_spec] + [raw_spec] * nc + tile_specs,
        out_specs=[pl.BlockSpec((None, N_HEADS_A, HEAD_DIM_A, 1), lambda i, t, pt, sel_: (seq(i, t), 0, 0, 0))],
        out_shape=[jax.ShapeDtypeStruct((db, N_HEADS_A, HEAD_DIM_A, 1), F32)],
        operands=(cols,) + tuple(raws) + (vt_pool,) * len(tile_specs),
        body=lambda pf, ins, outs, s: _moba_attend(pf[1], ins[0], ins[1:1 + nc], ins[1 + nc:], outs[0], s,
                                                   n_pages * PAGE_SIZE))


def _dot_nt(a, b):
    return lax.dot_general(a, b, (((1,), (1,)), ((), ())), preferred_element_type=F32)


def _diff_decode_kernel(pt_ref, q_ref, kn_ref, vn_ref, slope_ref, tbias_ref, lam_ref, subln_ref,
                        *rest, past_len, lam_init):
    del pt_ref
    npg = PAGES_PER_STEP
    k_refs, v_refs = rest[:npg], rest[npg:2 * npg]
    o_ref, m_sc, l_sc, acc_sc = rest[2 * npg:]
    j = pl.program_id(1)
    q8 = q_ref[...]
    slope8 = slope_ref[...]
    tbias = tbias_ref[...]
    lane_max = lambda x: jnp.max(x, axis=-1, keepdims=True)

    @pl.when(j == 0)
    def _():
        s_self = jnp.sum(_bf16_round(q8) * _bf16_round(kn_ref[...]), axis=-1, keepdims=True)
        m_sc[...] = jnp.broadcast_to(s_self, m_sc.shape)
        l_sc[...] = jnp.ones(l_sc.shape, F32)
        acc_sc[...] = vn_ref[...]

    q8b = q8.astype(BF16)
    scores = []
    for idx in range(npg):
        dist0 = (past_len - (j * npg + idx) * PAGE_SIZE).astype(F32)
        scores.append(_dot_nt(q8b, k_refs[idx][...].astype(BF16)) + tbias - slope8 * dist0)
    m_old = m_sc[...]
    m_step = lane_max(functools.reduce(jnp.maximum, scores))
    m_new = jnp.maximum(m_old, m_step)
    alpha = jnp.exp(m_old - m_new)
    m1 = m_new[:, 0:1]
    p_sum = jnp.zeros(scores[0].shape, F32)
    pv = jnp.zeros(acc_sc.shape, F32)
    for idx in range(npg):
        p = jnp.exp(scores[idx] - m1)
        p_sum = p_sum + p
        pv = pv + _dot(p.astype(BF16), v_refs[idx][...].astype(BF16))
    l_new = alpha * l_sc[...] + jnp.sum(p_sum, axis=-1, keepdims=True)
    acc_new = alpha * acc_sc[...] + pv
    m_sc[...] = m_new
    l_sc[...] = l_new
    acc_sc[...] = acc_new

    @pl.when(j == pl.num_programs(1) - 1)
    def _():
        o_c = acc_new * (1.0 / l_new)
        o = o_c - _lambda(lam_ref, lam_init) * pltpu.roll(o_c, N_HEADS_B, axis=0)
        ms = jnp.mean(o * o, axis=-1, keepdims=True)
        o_ref[...] = o * lax.rsqrt(ms + RMS_EPS) * subln_ref[...] * (1.0 - lam_init)


def _diff_decode(q, k_new, v_new, k_pool, v_pool, page_table, layer, lam_vecs, subln, lam_init):
    db, n_pages = page_table.shape
    depth, n_phys = k_pool.shape[:2]
    rows = PAGE_SIZE * N_HEADS_B
    pages = lambda pool: pool.reshape(depth * n_phys, rows, V_DIM_B)
    twice = lambda a: jnp.concatenate([a, a], axis=1)
    branch = (np.arange(V_DIM_B) // HEAD_DIM_B)[None, :] == np.arange(2)[:, None]
    q8 = jnp.concatenate([q * branch[0].astype(np.float32), q * branch[1].astype(np.float32)], axis=1)
    slopes = np.tile(np.asarray(_slopes(N_HEADS_B), np.float32), 2)
    col = np.arange(rows)
    own = (col % N_HEADS_B)[None, :] == (np.arange(SUBLANES) % N_HEADS_B)[:, None]
    tbias_np = np.where(own, slopes[:, None] * (col // N_HEADS_B)[None, :], NEG).astype(np.float32)
    seq = pl.BlockSpec((None, SUBLANES, V_DIM_B), lambda s, j, pt: (s, 0, 0))
    cst = lambda shp: pl.BlockSpec(shp, lambda s, j, pt: (0, 0))
    page = lambda i: pl.BlockSpec(
        (None, rows, V_DIM_B), lambda s, j, pt: (layer * n_phys + pt[s, j * PAGES_PER_STEP + i], 0, 0))
    page_specs = [page(i) for i in range(PAGES_PER_STEP)]
    stat = pltpu.VMEM((SUBLANES, LANES), F32)
    grid_spec = pltpu.PrefetchScalarGridSpec(
        num_scalar_prefetch=1,
        grid=(db, n_pages // PAGES_PER_STEP),
        in_specs=[seq, seq, seq, cst((SUBLANES, 1)), cst((SUBLANES, rows)), cst((4, HEAD_DIM_B)), cst((1, V_DIM_B))]
        + page_specs * 2,
        out_specs=seq,
        scratch_shapes=[stat, stat, stat],
    )
    o = pl.pallas_call(
        functools.partial(_diff_decode_kernel, past_len=n_pages * PAGE_SIZE, lam_init=lam_init),
        grid_spec=grid_spec,
        out_shape=jax.ShapeDtypeStruct((db, SUBLANES, V_DIM_B), F32),
        compiler_params=_params("arbitrary", "arbitrary"),
        name="diff_decode",
    )(page_table, q8, twice(k_new), twice(v_new), jnp.asarray(slopes[:, None]), jnp.asarray(tbias_np),
      lam_vecs, subln, *([pages(k_pool)] * PAGES_PER_STEP), *([pages(v_pool)] * PAGES_PER_STEP))
    return o[:, :N_HEADS_B]


def _block_diag_ones():
    i = np.arange(MXU_DIM) // 64
    return jnp.asarray((i[:, None] == i[None, :]).astype(np.float32), BF16)


def kernel(x_prompt, x_sample, cache_k_moba, cache_v_moba, cache_k_diff, cache_v_diff, page_table, c_prompt, c_sample, w_ada, b_ada, norm_ffn1, ffn1_w_gate, ffn1_w_up, ffn1_w_down, norm_mix, w_in, qn_moba, kn_moba, qn_diff, kn_diff, lambda_q1, lambda_k1, lambda_q2, lambda_k2, subln_diff, w_branch_moba, w_branch_diff, w_out, norm_ffn2, ffn2_w_gate, ffn2_w_up, ffn2_w_down):
    depth = w_ada.shape[0]
    b, s, d = x_prompt.shape
    db, t_new, _ = x_sample.shape
    assert t_new == 1 and s % TOKEN_TILE == 0 and db % SUBLANES == 0
    n_pages = page_table.shape[1]
    assert n_pages % PAGES_PER_STEP == 0
    bd = _block_diag_ones()
    tile8 = lambda v: jnp.tile(v, WIDTH // v.shape[0]).reshape(1, WIDTH)

    y_p, y_s = x_prompt, x_sample.reshape(1, db, d)
    rows_p, rows_s = [], []
    for l in range(depth):
        lam_init = _lambda_init(l)
        bf = lambda w: w[l].astype(BF16)
        row = lambda v: v[l].reshape(1, -1)
        mod = _ada(jnp.concatenate([c_prompt, c_sample], axis=0), w_ada[l], b_ada[l])
        mod_p = mod[:b].reshape(b, N_MOD, 1, d)
        mod_s = mod[b:].reshape(db, N_MOD, d).transpose(1, 0, 2).reshape(1, N_MOD, db, d)
        ffn1 = (row(norm_ffn1), bf(ffn1_w_gate), bf(ffn1_w_up), bf(ffn1_w_down))
        ffn2 = (row(norm_ffn2), bf(ffn2_w_gate), bf(ffn2_w_up), bf(ffn2_w_down))
        w_in_bf = bf(w_in)
        gains = jnp.concatenate([tile8(qn_moba[l]), tile8(kn_moba[l]), tile8(qn_diff[l]), tile8(kn_diff[l])], 0)
        lam_vecs = jnp.stack([lambda_q1[l], lambda_k1[l], lambda_q2[l], lambda_k2[l]])
        mix_w = (bf(w_branch_moba), bf(w_branch_diff), bf(w_out))

        (x1s,) = _ffn(y_s, mod_s, *ffn1, k0=0, tm=db)
        (q_as, k_as, v_as, q_bs, k_bs, v_bs, g_as, g_bs) = _proj(
            x1s, mod_s, row(norm_mix), w_in_bf, gains, bd, db, False)
        heads_a = lambda a: a.reshape(db, N_HEADS_A, HEAD_DIM_A)
        heads_b = lambda a: a.reshape(db, N_HEADS_B, V_DIM_B)
        cols = jnp.stack([heads_a(q_as), heads_a(k_as), heads_a(v_as)], axis=1)[..., None]

        steps = s // TOKEN_TILE
        kt_pool, vt_pool = _moba_pool_tiles(cache_k_moba), _moba_pool_tiles(cache_v_moba)
        gate_rider = lambda first: _gate_rider(page_table, cols, kt_pool, l, first, steps)
        x1, gate0, raw0 = _ffn(y_p, mod_p, *ffn1, k0=0, tm=TOKEN_TILE, rider=gate_rider(0))
        (kt_a, vt32_a, k_b, v_b, ka_a, ka_b, qt_a, qt_b, vt_a, vt_b, km_a, g_a, g_b, gate1, raw1) = _proj(
            x1, mod_p, row(norm_mix), w_in_bf, gains, bd, TOKEN_TILE, True, rider=gate_rider(GATE_PAGES_PER_STEP))
        o_a, o_b = _prompt_attention(qt_a, ka_a, vt_a, km_a.reshape(b, -1, WIDTH), qt_b, ka_b, vt_b,
                                     lam_vecs, tile8(subln_diff[l]), lam_init)
        attend = _attend_rider(page_table, _moba_select([gate0, gate1]), cols, [raw0, raw1], vt_pool, l, steps)
        y_p, o_as = _mix_ffn(x1, o_a, o_b, g_a, g_b, mod_p, *mix_w, *ffn2, tm=TOKEN_TILE, rider=attend)
        token_major = lambda a: a.reshape(b, N_HEADS_A, HEAD_DIM_A, s).transpose(0, 3, 1, 2)
        rows_p.append((token_major(kt_a), token_major(vt32_a), k_b, v_b))

        o_bs = _diff_decode(heads_b(q_bs), heads_b(k_bs), heads_b(v_bs), cache_k_diff, cache_v_diff,
                            page_table, l, lam_vecs, subln_diff[l].reshape(1, V_DIM_B), lam_init)
        as_rows = lambda a: a.reshape(1, db, WIDTH).astype(BF16)
        (y_s,) = _mix_ffn(x1s, as_rows(o_as), as_rows(o_bs), g_as, g_bs, mod_s, *mix_w, *ffn2, tm=db)
        rows_s.append((k_as, v_as, k_bs, v_bs))

    def stack(rows, i, lead, heads, hd):
        return jnp.stack([r[i].reshape(lead + (heads, hd)) for r in rows])

    lp, ls = (b, s), (db, 1)
    return (y_p, y_s.reshape(db, 1, d),
            stack(rows_p, 0, lp, N_HEADS_A, HEAD_DIM_A), stack(rows_p, 1, lp, N_HEADS_A, HEAD_DIM_A),
            stack(rows_p, 2, lp, N_HEADS_B, V_DIM_B), stack(rows_p, 3, lp, N_HEADS_B, V_DIM_B),
            stack(rows_s, 0, ls, N_HEADS_A, HEAD_DIM_A), stack(rows_s, 1, ls, N_HEADS_A, HEAD_DIM_A),
            stack(rows_s, 2, ls, N_HEADS_B, V_DIM_B), stack(rows_s, 3, ls, N_HEADS_B, V_DIM_B))
```

```python
import functools
import math
from typing import Callable, NamedTuple

import jax
import jax.numpy as jnp
import numpy as np
from jax import lax
from jax.experimental import pallas as pl
from jax.experimental.pallas import tpu as pltpu

F32 = jnp.float32
BF16 = jnp.bfloat16

N_HEADS_A = 8
HEAD_DIM_A = 64
MOBA_BLOCK = 256
MOBA_TOPK = 3
N_HEADS_B = 4
HEAD_DIM_B = 64
V_DIM_B = 2 * HEAD_DIM_B
WIDTH = 512
N_GROUPS = WIDTH // 64
PAGE_SIZE = 128
PAGES_PER_BLOCK = MOBA_BLOCK // PAGE_SIZE
N_MOD = 9
RMS_EPS = 1e-6
QK_SCALE = 0.125
NEG = -1e30

LANES = 128
SUBLANES = 8
MXU_DIM = 256
VMEM_LIMIT_BYTES = 58 * 1024 * 1024

TOKEN_TILE = 512
MAX_FF_TILE = 1408
PAGES_PER_STEP = 32
GATE_PAGES_PER_STEP = 32


def _slopes(n_heads):
    return [2.0 ** (-8.0 * (i + 1) / n_heads) for i in range(n_heads)]


def _lambda_init(layer):
    return 0.8 - 0.6 * math.exp(-0.3 * layer)


def _dot(a, b):
    return jnp.dot(a, b, preferred_element_type=F32)


def _bf16_round(x):
    return x.astype(BF16).astype(F32)


def _rms(x, w):
    ms = jnp.mean(x * x, axis=-1, keepdims=True)
    return x * lax.rsqrt(ms + RMS_EPS) * w


def _pick_ff_tile(d_ff, rows):
    cap = MXU_DIM if rows >= MXU_DIM else MAX_FF_TILE
    best = LANES
    for t in range(LANES, min(d_ff, cap) + 1, LANES):
        if d_ff % t == 0:
            best = t
    return best


def _params(*sem):
    return pltpu.CompilerParams(dimension_semantics=sem, vmem_limit_bytes=VMEM_LIMIT_BYTES)


def _const_spec(shape):
    nd = len(shape)
    return pl.BlockSpec(shape, lambda *_: (0,) * nd)


def _ada_kernel(c_ref, w_ref, b_ref, o_ref):
    c = c_ref[...]
    s = c * jax.nn.sigmoid(c)
    o_ref[...] = _dot(s.astype(BF16), w_ref[...].astype(BF16)) + b_ref[...]


def _ada(c, w, b):
    m, d = c.shape
    n = w.shape[1]
    tn = 1024 if n % 1024 == 0 else n
    return pl.pallas_call(
        _ada_kernel,
        grid=(n // tn,),
        in_specs=[pl.BlockSpec((m, d), lambda j: (0, 0)),
                  pl.BlockSpec((d, tn), lambda j: (0, j)),
                  pl.BlockSpec((1, tn), lambda j: (0, j))],
        out_specs=pl.BlockSpec((m, tn), lambda j: (0, j)),
        out_shape=jax.ShapeDtypeStruct((m, n), F32),
        compiler_params=_params("arbitrary"),
        name="ada_mod",
    )(c, w, b.reshape(1, n))


def _ffn_update(x, shift, scale, gate, nw, wg_ref, wu_ref, wd_ref, tf):
    h = (_rms(x, nw) * (1.0 + scale) + shift).astype(BF16)
    d_ff = wg_ref.shape[1]
    acc = jnp.zeros(x.shape, F32)
    for j in range(d_ff // tf):
        g = _dot(h, wg_ref[:, j * tf:(j + 1) * tf])
        u = _dot(h, wu_ref[:, j * tf:(j + 1) * tf])
        a = (g * jax.nn.sigmoid(g) * u).astype(BF16)
        acc = acc + _dot(a, wd_ref[j * tf:(j + 1) * tf, :])
    return x + 0.5 * gate * acc


class _Rider(NamedTuple):
    prefetch: tuple
    in_specs: list
    out_specs: list
    out_shape: list
    operands: tuple
    body: Callable


def _gate_pages(q_ref, k_refs, gate_ref, s_ref):
    q = q_ref[...]
    q_bf = _bf16_round(q)
    for bi in range(len(k_refs) // PAGES_PER_BLOCK):
        pages = [k_refs[bi * PAGES_PER_BLOCK + i][...] for i in range(PAGES_PER_BLOCK)]
        for i, k_page in enumerate(pages):
            s_ref[bi * PAGES_PER_BLOCK + i] = jnp.sum(k_page * q, axis=1)
        ksum = jnp.sum(functools.reduce(jnp.add, pages), axis=-1, keepdims=True)
        gate_ref[bi] = jnp.sum(_bf16_round(ksum * (1.0 / MOBA_BLOCK)) * q_bf, axis=1)


def _gate_rider(page_table, cols, kt_pool, layer, first_page, steps_per_batch):
    db = page_table.shape[0]
    npg = GATE_PAGES_PER_STEP
    seq = lambda i, t: i * steps_per_batch + t
    page = lambda k: pl.BlockSpec(
        (None, None, N_HEADS_A, HEAD_DIM_A, PAGE_SIZE),
        lambda i, t, pt: (layer, pt[seq(i, t), first_page + k], 0, 0, 0))
    per_seq = lambda shp: pl.BlockSpec((None,) + shp, lambda i, t, pt: (seq(i, t), 0, 0, 0))
    q_spec = pl.BlockSpec((None, None, N_HEADS_A, HEAD_DIM_A, 1), lambda i, t, pt: (seq(i, t), 0, 0, 0, 0))
    return _Rider(
        prefetch=(page_table,),
        in_specs=[q_spec] + [page(k) for k in range(npg)],
        out_specs=[per_seq((npg // PAGES_PER_BLOCK, N_HEADS_A, 1)), per_seq((npg, N_HEADS_A, PAGE_SIZE))],
        out_shape=[jax.ShapeDtypeStruct((db, npg // PAGES_PER_BLOCK, N_HEADS_A, 1), F32),
                   jax.ShapeDtypeStruct((db, npg, N_HEADS_A, PAGE_SIZE), F32)],
        operands=(cols,) + (kt_pool,) * npg,
        body=lambda pf, ins, outs, s: _gate_pages(ins[0], ins[1:], *outs))


def _dense_call(body, name, grid, in_specs, out_specs, out_shape, operands, rider=None):
    n_in, n_out = len(in_specs), len(out_specs)
    if rider is None:
        def kernel_fn(*refs):
            body(refs[:n_in], refs[n_in:])
        return pl.pallas_call(kernel_fn, grid=grid, in_specs=in_specs, out_specs=out_specs, out_shape=out_shape,
                              compiler_params=_params("arbitrary", "arbitrary"), name=name)(*operands)
    n_pf = len(rider.prefetch)
    assert rider.out_shape[0].shape[0] == grid[0] * grid[1]

    def kernel_fn(*refs):
        pf, refs = refs[:n_pf], refs[n_pf:]
        n_all_in = n_in + len(rider.in_specs)
        ins, outs = refs[:n_all_in], refs[n_all_in:]
        body(ins[:n_in], outs[:n_out])
        rider.body(pf, ins[n_in:], outs[n_out:], pl.program_id(0) * grid[1] + pl.program_id(1))

    return pl.pallas_call(
        kernel_fn,
        grid_spec=pltpu.PrefetchScalarGridSpec(
            num_scalar_prefetch=n_pf, grid=grid, in_specs=list(in_specs) + rider.in_specs,
            out_specs=list(out_specs) + rider.out_specs),
        out_shape=list(out_shape) + rider.out_shape,
        compiler_params=_params("arbitrary", "arbitrary"),
        name=name,
    )(*rider.prefetch, *operands, *rider.operands)


def _ffn(x, mod, nw, wg, wu, wd, k0, tm, rider=None):
    b, s, d = x.shape
    r = mod.shape[2]
    d_ff = wg.shape[1]
    tf = _pick_ff_tile(d_ff, tm)

    def body(ins, outs):
        x_ref, mod_ref, nw_ref, wg_ref, wu_ref, wd_ref = ins
        outs[0][...] = _ffn_update(x_ref[...], mod_ref[k0], mod_ref[k0 + 1], mod_ref[k0 + 2],
                                   nw_ref[...], wg_ref, wu_ref, wd_ref, tf)

    return _dense_call(
        body, "ffn", (b, s // tm),
        in_specs=[pl.BlockSpec((None, tm, d), lambda i, t, *_: (i, t, 0)),
                  pl.BlockSpec((None, N_MOD, r, d), lambda i, t, *_: (i, 0, 0, 0)),
                  _const_spec((1, d)),
                  _const_spec((d, d_ff)), _const_spec((d, d_ff)), _const_spec((d_ff, d))],
        out_specs=[pl.BlockSpec((None, tm, d), lambda i, t, *_: (i, t, 0))],
        out_shape=[jax.ShapeDtypeStruct((b, s, d), F32)],
        operands=(x, mod, nw, wg, wu, wd), rider=rider)


def _head_norm(seg, gain_row, bd):
    sq = (seg * seg).astype(BF16)
    parts = [_dot(sq[:, c * MXU_DIM:(c + 1) * MXU_DIM], bd) for c in range(WIDTH // MXU_DIM)]
    ms = jnp.concatenate(parts, axis=1) * (1.0 / 64.0)
    return seg * lax.rsqrt(ms + RMS_EPS) * gain_row


def _proj_kernel(x_ref, mod_ref, nw_ref, win_ref, gains_ref, bd_ref, *outs, transposed):
    x = x_ref[...]
    d = x.shape[1]
    h = (_rms(x, nw_ref[...]) * (1.0 + mod_ref[4]) + mod_ref[3]).astype(BF16)
    bd = bd_ref[...]

    def seg(j):
        return _dot(h, win_ref[:, j * WIDTH:(j + 1) * WIDTH])

    q_a = _head_norm(seg(0), gains_ref[0:1, :], bd) * QK_SCALE
    k_a = _head_norm(seg(1), gains_ref[1:2, :], bd)
    v_a = seg(2)
    q_b = _head_norm(seg(3), gains_ref[2:3, :], bd) * QK_SCALE
    k_b = _head_norm(seg(4), gains_ref[3:4, :], bd)
    v_b = seg(5)
    g0 = 6 * WIDTH
    g_a = jax.nn.sigmoid(_dot(h, win_ref[:, g0:g0 + d]))
    g_b = jax.nn.sigmoid(_dot(h, win_ref[:, g0 + d:g0 + 2 * d]))

    if not transposed:
        (qa_ref, ka_ref, va_ref, qb_ref, kb_ref, vb_ref, ga_ref, gb_ref) = outs
        qa_ref[...] = q_a
        ka_ref[...] = k_a
        va_ref[...] = v_a
        qb_ref[...] = q_b
        kb_ref[...] = k_b
        vb_ref[...] = v_b
    else:
        (kta_ref, vta32_ref, kb_ref, vb_ref, kaa_ref, kab_ref, qta_ref, qtb_ref,
         vta_ref, vtb_ref, km_ref, ga_ref, gb_ref) = outs
        kta_ref[...] = k_a.T
        vt_a = v_a.T
        vta32_ref[...] = vt_a
        qt_a, qt_b, vt_b = q_a.T, q_b.T, v_b.T
        for r in range(x.shape[0] // MOBA_BLOCK):
            rows = slice(r * MOBA_BLOCK, (r + 1) * MOBA_BLOCK)
            n = pl.program_id(1) * (x.shape[0] // MOBA_BLOCK) + r
            for g, (ka_g, kb_g) in enumerate(zip(_augmented_keys(k_a[rows], n), _augmented_keys(k_b[rows], n))):
                kaa_ref[r, g] = ka_g
                kab_ref[r, g] = kb_g
            qta_ref[r] = qt_a[:, rows].astype(BF16)
            qtb_ref[r] = qt_b[:, rows].astype(BF16)
            vta_ref[r] = vt_a[:, rows].astype(BF16)
            vtb_ref[r] = vt_b[:, rows].astype(BF16)
            km_ref[r] = jnp.sum(k_a[rows], axis=0, keepdims=True) * (1.0 / MOBA_BLOCK)
        for hd in range(N_HEADS_B):
            lanes = slice(V_DIM_B * hd, V_DIM_B * (hd + 1))
            kb_ref[pl.ds(hd, x.shape[0], stride=N_HEADS_B), :] = k_b[:, lanes]
            vb_ref[pl.ds(hd, x.shape[0], stride=N_HEADS_B), :] = v_b[:, lanes]
    ga_ref[...] = g_a.astype(BF16)
    gb_ref[...] = g_b.astype(BF16)


def _proj(x, mod, nw, w_in, gains, bd, tm, transposed, rider=None):
    b, s, d = x.shape
    r = mod.shape[2]
    d_in = w_in.shape[1]
    row_spec = lambda w: pl.BlockSpec((None, tm, w), lambda i, t, *_: (i, t, 0))
    rows = lambda w, dt=F32: jax.ShapeDtypeStruct((b, s, w), dt)
    if transposed:
        nb, bpt = s // MOBA_BLOCK, tm // MOBA_BLOCK
        blk = lambda shp: pl.BlockSpec((None, bpt) + shp, lambda i, t, *_: (i, t, 0, 0))
        col_spec = pl.BlockSpec((None, WIDTH, tm), lambda i, t, *_: (i, 0, t))
        head_rows = pl.BlockSpec((None, tm * N_HEADS_B, V_DIM_B), lambda i, t, *_: (i, t, 0))
        k_aug = pl.BlockSpec((None, bpt, N_GROUPS, MOBA_BLOCK, LANES), lambda i, t, *_: (i, t, 0, 0, 0))
        out_specs = [col_spec] * 2 + [head_rows] * 2 + [k_aug] * 2 \
            + [blk((WIDTH, MOBA_BLOCK))] * 4 + [blk((1, WIDTH))] + [row_spec(d)] * 2
        out_shape = [jax.ShapeDtypeStruct((b, WIDTH, s), F32)] * 2 \
            + [jax.ShapeDtypeStruct((b, s * N_HEADS_B, V_DIM_B), F32)] * 2 \
            + [jax.ShapeDtypeStruct((b, nb, N_GROUPS, MOBA_BLOCK, LANES), BF16)] * 2 \
            + [jax.ShapeDtypeStruct((b, nb, WIDTH, MOBA_BLOCK), BF16)] * 4 \
            + [jax.ShapeDtypeStruct((b, nb, 1, WIDTH), F32)] + [rows(d, BF16)] * 2
    else:
        out_specs = [row_spec(WIDTH)] * 6 + [row_spec(d)] * 2
        out_shape = [rows(WIDTH)] * 6 + [rows(d, BF16)] * 2
    return _dense_call(
        lambda ins, outs: _proj_kernel(*ins, *outs, transposed=transposed), "mixer_proj", (b, s // tm),
        in_specs=[pl.BlockSpec((None, tm, d), lambda i, t, *_: (i, t, 0)),
                  pl.BlockSpec((None, N_MOD, r, d), lambda i, t, *_: (i, 0, 0, 0)),
                  _const_spec((1, d)), _const_spec((d, d_in)),
                  _const_spec((4, WIDTH)), _const_spec((MXU_DIM, MXU_DIM))],
        out_specs=out_specs, out_shape=out_shape, operands=(x, mod, nw, w_in, gains, bd), rider=rider)


FEATURE_ROWS = SUBLANES
MAX_KEY_BLOCKS = HEAD_DIM_A - FEATURE_ROWS


def _augmented_keys(k, n):
    lane = lax.broadcasted_iota(jnp.int32, (MOBA_BLOCK, LANES), 1)
    key = lax.broadcasted_iota(jnp.int32, (MOBA_BLOCK, LANES), 0).astype(F32)
    feat = jnp.where(lane == 64, 1.0, jnp.where(lane == 65, key,
                                                 jnp.where(lane == 64 + FEATURE_ROWS + n, 1.0, 0.0)))
    out = []
    for g in range(N_GROUPS):
        p, half = divmod(g, 2)
        pair = k[:, LANES * p:LANES * (p + 1)]
        k_g = pair if half == 0 else pltpu.roll(pair, 64, axis=1)
        out.append(jnp.where(lane < 64, k_g, feat).astype(BF16))
    return out


def _augmented_queries(qt, slopes, block_biases):
    qry = lax.broadcasted_iota(jnp.int32, (FEATURE_ROWS, MOBA_BLOCK), 1).astype(F32)
    row = lax.broadcasted_iota(jnp.int32, (FEATURE_ROWS, MOBA_BLOCK), 0)
    out = []
    for g, (slope, bias) in enumerate(zip(slopes, block_biases)):
        head = jnp.where(row == 0, -slope * qry, jnp.where(row == 1, slope, 0.0))
        pad = jnp.zeros((64 - FEATURE_ROWS - bias.shape[0], MOBA_BLOCK), F32)
        feat = jnp.concatenate([head, bias, pad], axis=0).astype(BF16)
        out.append(jnp.concatenate([qt[64 * g:64 * (g + 1), :], feat], axis=0))
    return out


def _pair_scores(k_aug, q_aug, p):
    return [_dot(k_aug[2 * p + half], q_aug[2 * p + half]) for half in range(2)]


def _attend_block(k_aug, vtb, q_aug, causal, m_all, l_all, acc_sc, v_rows, st_first, s0_sc, k_next):
    m_out, l_out = [], []
    n_pairs = N_GROUPS // 2
    st_next = _pair_scores(k_aug, q_aug, 0) if st_first is None else st_first
    for p in range(n_pairs):
        st_pair = st_next
        if p + 1 < n_pairs:
            st_next = _pair_scores(k_aug, q_aug, p + 1)
        else:
            nxt = _pair_scores(k_next, q_aug, 0)
            s0_sc[0] = nxt[0]
            s0_sc[1] = nxt[1]
        for half in range(2):
            g = 2 * p + half
            st = st_pair[half]
            if causal is not None:
                st = jnp.where(causal, st, NEG)
            m_old = m_all[g:g + 1, :]
            m_new = jnp.maximum(m_old, jnp.max(st, axis=0, keepdims=True))
            alpha = jnp.exp(m_old - m_new)
            pt = jnp.exp(st - m_new)
            l_out.append(alpha * l_all[g:g + 1, :] + jnp.sum(pt, axis=0, keepdims=True))
            rows = v_rows(g)
            nr = rows.stop - rows.start
            acc_rows = slice(g * nr, (g + 1) * nr)
            acc_sc[acc_rows, :] = alpha * acc_sc[acc_rows, :] + _dot(vtb[rows, :], pt.astype(BF16))
            m_out.append(m_new)
    return jnp.concatenate(m_out, axis=0), jnp.concatenate(l_out, axis=0)


def _attend_all_blocks(qi, k_ref, vt_ref, q_aug, acc_sc, s0_sc, v_rows):
    key_i = lax.broadcasted_iota(jnp.int32, (MOBA_BLOCK, MOBA_BLOCK), 0)
    qry_i = lax.broadcasted_iota(jnp.int32, (MOBA_BLOCK, MOBA_BLOCK), 1)
    acc_sc[...] = jnp.zeros(acc_sc.shape, F32)
    m0 = jnp.full((N_GROUPS, MOBA_BLOCK), NEG, F32)
    l0 = jnp.zeros((N_GROUPS, MOBA_BLOCK), F32)
    m1, l1 = _attend_block(k_ref.at[qi], vt_ref[qi], q_aug, key_i <= qry_i, m0, l0, acc_sc, v_rows,
                           None, s0_sc, k_ref.at[0])

    def past(n, carry):
        return _attend_block(k_ref.at[n], vt_ref[n], q_aug, None, *carry, acc_sc, v_rows,
                             [s0_sc[0], s0_sc[1]], s0_sc, k_ref.at[n + 1])

    _, l_fin = lax.fori_loop(0, qi, past, (m1, l1))
    return l_fin


def _block_distance(qi, nbp):
    blk_i = lax.broadcasted_iota(jnp.int32, (nbp, MOBA_BLOCK), 0)
    return blk_i, ((qi - blk_i) * MOBA_BLOCK).astype(F32)


def _moba_prompt_kernel(qt_ref, k_ref, vt_ref, km_ref, o_ref, acc_sc, s0_sc):
    qi = pl.program_id(1)
    nb = k_ref.shape[0]
    nbp = -(-nb // SUBLANES) * SUBLANES
    blk = MOBA_BLOCK
    slopes = _slopes(N_HEADS_A)
    qt = qt_ref[...]

    km = km_ref[...].astype(BF16)
    blk_i, blk_dist = _block_distance(qi, nbp)
    valid = blk_i[:nb] < qi
    biases = []
    for g in range(N_GROUPS):
        gate = _dot(km[:, 64 * g:64 * (g + 1)], qt[64 * g:64 * (g + 1), :])
        gate = jnp.where(valid, gate, NEG)
        rank = jnp.zeros((nb, blk), jnp.int32)
        for m in range(nb):
            gm = gate[m:m + 1, :]
            beats = (gm > gate) | ((gm == gate) & (m < blk_i[:nb]))
            rank = rank + beats.astype(jnp.int32)
        drop = jnp.where(valid & (rank < MOBA_TOPK), 0.0, NEG)
        if nbp > nb:
            drop = jnp.concatenate([drop, jnp.zeros((nbp - nb, blk), F32)], axis=0)
        biases.append(jnp.where(blk_i == qi, 0.0, drop - slopes[g] * blk_dist))

    q_aug = _augmented_queries(qt, slopes, biases)
    l_fin = _attend_all_blocks(qi, k_ref, vt_ref, q_aug, acc_sc, s0_sc,
                               lambda g: slice(64 * g, 64 * (g + 1)))
    parts = [acc_sc[64 * g:64 * (g + 1), :] * (1.0 / l_fin[g:g + 1, :]) for g in range(N_GROUPS)]
    o_ref[...] = jnp.concatenate(parts, axis=0).T.astype(BF16)


def _lambda(lam_ref, lam_init):
    a = jnp.sum(lam_ref[0:1, :] * lam_ref[1:2, :], axis=-1, keepdims=True)
    b = jnp.sum(lam_ref[2:3, :] * lam_ref[3:4, :], axis=-1, keepdims=True)
    return jnp.exp(a) - jnp.exp(b) + lam_init


def _diff_prompt_kernel(qt_ref, k_ref, vt_ref, lam_ref, subln_ref, o_ref, acc_sc, s0_sc, *, lam_init):
    qi = pl.program_id(1)
    nb = k_ref.shape[0]
    nbp = -(-nb // SUBLANES) * SUBLANES
    slopes = [s for s in _slopes(N_HEADS_B) for _ in range(2)]
    _, blk_dist = _block_distance(qi, nbp)
    q_aug = _augmented_queries(qt_ref[...], slopes, [-s * blk_dist for s in slopes])
    v_rows = lambda g: slice(V_DIM_B * (g // 2), V_DIM_B * (g // 2 + 1))
    l_fin = _attend_all_blocks(qi, k_ref, vt_ref, q_aug, acc_sc, s0_sc, v_rows)

    lam = _lambda(lam_ref, lam_init)
    parts = []
    for h in range(N_HEADS_B):
        o0 = acc_sc[V_DIM_B * (2 * h):V_DIM_B * (2 * h + 1), :] * (1.0 / l_fin[2 * h:2 * h + 1, :])
        o1 = acc_sc[V_DIM_B * (2 * h + 1):V_DIM_B * (2 * h + 2), :] * (1.0 / l_fin[2 * h + 1:2 * h + 2, :])
        o = o0 - lam * o1
        ms = jnp.mean(o * o, axis=0, keepdims=True)
        parts.append(o * lax.rsqrt(ms + RMS_EPS))
    o_t = jnp.concatenate(parts, axis=0).T
    o_ref[...] = (o_t * subln_ref[...] * (1.0 - lam_init)).astype(BF16)


def _prompt_attention(qt_a, k_aug_a, vt_a, km_a, qt_b, k_aug_b, vt_b, lam_vecs, subln_row, lam_init):
    b, nb = qt_a.shape[:2]
    assert nb <= MAX_KEY_BLOCKS
    blk = MOBA_BLOCK
    s = nb * blk
    q_spec = pl.BlockSpec((None, None, WIDTH, blk), lambda i, t: (i, t, 0, 0))
    k_spec = pl.BlockSpec((None, nb, N_GROUPS, blk, LANES), lambda i, t: (i, 0, 0, 0, 0))
    vt_spec = pl.BlockSpec((None, nb, WIDTH, blk), lambda i, t: (i, 0, 0, 0))
    o_spec = pl.BlockSpec((None, blk, WIDTH), lambda i, t: (i, t, 0))
    o_shape = jax.ShapeDtypeStruct((b, s, WIDTH), BF16)
    s0 = pltpu.VMEM((2, blk, blk), F32)
    o_a = pl.pallas_call(
        _moba_prompt_kernel,
        grid=(b, nb),
        in_specs=[q_spec, k_spec, vt_spec, pl.BlockSpec((None, nb, WIDTH), lambda i, t: (i, 0, 0))],
        out_specs=o_spec,
        out_shape=o_shape,
        scratch_shapes=[pltpu.VMEM((WIDTH, blk), F32), s0],
        compiler_params=_params("arbitrary", "arbitrary"),
        name="moba_prompt",
    )(qt_a, k_aug_a, vt_a, km_a)
    o_b = pl.pallas_call(
        functools.partial(_diff_prompt_kernel, lam_init=lam_init),
        grid=(b, nb),
        in_specs=[q_spec, k_spec, vt_spec, _const_spec((4, HEAD_DIM_B)), _const_spec((1, WIDTH))],
        out_specs=o_spec,
        out_shape=o_shape,
        scratch_shapes=[pltpu.VMEM((N_GROUPS * V_DIM_B, blk), F32), s0],
        compiler_params=_params("arbitrary", "arbitrary"),
        name="diff_prompt",
    )(qt_b, k_aug_b, vt_b, lam_vecs, subln_row)
    return o_a, o_b


def _mix_ffn_kernel(x_ref, oa_ref, ob_ref, ga_ref, gb_ref, mod_ref, wba_ref, wbd_ref, wout_ref,
                    nw_ref, wg_ref, wu_ref, wd_ref, o_ref, *, tf):
    y_a = _dot(oa_ref[...], wba_ref[...])
    y_b = _dot(ob_ref[...], wbd_ref[...])
    mixed = _dot((ga_ref[...].astype(F32) * y_a + gb_ref[...].astype(F32) * y_b).astype(BF16), wout_ref[...])
    x = x_ref[...] + mod_ref[5] * mixed
    o_ref[...] = _ffn_update(x, mod_ref[6], mod_ref[7], mod_ref[8], nw_ref[...],
                             wg_ref, wu_ref, wd_ref, tf)


def _mix_ffn(x, o_a, o_b, g_a, g_b, mod, w_ba, w_bd, w_out, nw, wg, wu, wd, tm, rider=None):
    b, s, d = x.shape
    r = mod.shape[2]
    d_ff = wg.shape[1]
    tf = _pick_ff_tile(d_ff, tm)
    row_spec = lambda w: pl.BlockSpec((None, tm, w), lambda i, t, *_: (i, t, 0))
    return _dense_call(
        lambda ins, outs: _mix_ffn_kernel(*ins, *outs, tf=tf), "mix_ffn", (b, s // tm),
        in_specs=[row_spec(d), row_spec(WIDTH), row_spec(WIDTH), row_spec(d), row_spec(d),
                  pl.BlockSpec((None, N_MOD, r, d), lambda i, t, *_: (i, 0, 0, 0)),
                  _const_spec((WIDTH, d)), _const_spec((WIDTH, d)), _const_spec((d, d)),
                  _const_spec((1, d)),
                  _const_spec((d, d_ff)), _const_spec((d, d_ff)), _const_spec((d_ff, d))],
        out_specs=[row_spec(d)], out_shape=[jax.ShapeDtypeStruct((b, s, d), F32)],
        operands=(x, o_a, o_b, g_a, g_b, mod, w_ba, w_bd, w_out, nw, wg, wu, wd), rider=rider)


def _moba_select_kernel(gate_ref, sel_ref):
    gate = gate_ref[...]
    nblk = gate.shape[1]
    blk_i = lax.broadcasted_iota(jnp.int32, gate.shape, 1)
    rank = jnp.zeros(gate.shape, jnp.int32)
    for m in range(nblk):
        gm = gate[:, m:m + 1]
        beats = (gm > gate) | ((gm == gate) & (m < blk_i))
        rank = rank + beats.astype(jnp.int32)
    lane = lax.broadcasted_iota(jnp.int32, sel_ref.shape, 1)
    sel = jnp.zeros(sel_ref.shape, jnp.int32)
    for j in range(MOBA_TOPK):
        sel = jnp.where(lane == j, jnp.sum(jnp.where(rank == j, blk_i, 0), axis=1, keepdims=True), sel)
    sel_ref[...] = sel


def _moba_attend(sel_ref, cols_ref, s_refs, v_refs, o_ref, s, past_len):
    npg = GATE_PAGES_PER_STEP
    lane = lax.broadcasted_iota(jnp.int32, (1, PAGE_SIZE), 1).astype(F32)
    slopes = _slopes(N_HEADS_A)
    for h in range(N_HEADS_A):
        q_h = cols_ref[0, h]
        tiles = range(h * MOBA_TOPK * PAGES_PER_BLOCK, (h + 1) * MOBA_TOPK * PAGES_PER_BLOCK)
        rows = []
        for j in range(MOBA_TOPK):
            blk = sel_ref[s, j * N_HEADS_A + h]
            for i in range(PAGES_PER_BLOCK):
                page = PAGES_PER_BLOCK * blk + i
                raw = s_refs[0][jnp.clip(page, 0, npg - 1), h:h + 1, :]
                for c in range(1, len(s_refs)):
                    other = s_refs[c][jnp.clip(page - c * npg, 0, npg - 1), h:h + 1, :]
                    raw = jnp.where(page >= c * npg, other, raw)
                dist0 = (past_len - blk * MOBA_BLOCK - i * PAGE_SIZE).astype(F32)
                rows.append(raw - slopes[h] * (dist0 - lane))
        s_self = jnp.sum(q_h * cols_ref[1, h], axis=0, keepdims=True)
        m = s_self
        for r in rows:
            m = jnp.maximum(m, jnp.max(r, axis=1, keepdims=True))
        w_self = jnp.exp(s_self - m)
        l = w_self
        acc = jnp.zeros((HEAD_DIM_A, PAGE_SIZE), F32)
        for t, r in zip(tiles, rows):
            p = jnp.exp(r - m)
            l = l + jnp.sum(p, axis=1, keepdims=True)
            acc = acc + p * v_refs[t][...]
        o = jnp.sum(acc, axis=1, keepdims=True) + w_self * cols_ref[2, h]
        o_ref[h] = o * (1.0 / l)


def _moba_pool_tiles(pool):
    return jnp.transpose(pool, (0, 1, 3, 4, 2))


def _moba_select(gates):
    gate = jnp.concatenate(gates, axis=1)[..., 0]
    db, nblk = gate.shape[:2]
    assert nblk >= MOBA_TOPK
    gate = gate.transpose(0, 2, 1).reshape(db * N_HEADS_A, nblk)
    sel = pl.pallas_call(
        _moba_select_kernel,
        grid=(1,),
        in_specs=[_const_spec(gate.shape)],
        out_specs=_const_spec((db * N_HEADS_A, LANES)),
        out_shape=jax.ShapeDtypeStruct((db * N_HEADS_A, LANES), jnp.int32),
        compiler_params=_params("arbitrary"),
        name="moba_select",
    )(gate)
    return sel[:, :MOBA_TOPK].reshape(db, N_HEADS_A, MOBA_TOPK).transpose(0, 2, 1).reshape(db, -1)


def _attend_rider(page_table, sel, cols, raws, vt_pool, layer, steps_per_batch):
    db, n_pages = page_table.shape
    nblk = n_pages // PAGES_PER_BLOCK
    assert len(raws) * GATE_PAGES_PER_STEP == n_pages
    seq = lambda i, t: i * steps_per_batch + t

    def tile_spec(h, j, i_page):
        def index(i, t, pt, sel_):
            blk = jnp.clip(sel_[seq(i, t), j * N_HEADS_A + h], 0, nblk - 1)
            return (layer, pt[seq(i, t), PAGES_PER_BLOCK * blk + i_page], h, 0, 0)
        return pl.BlockSpec((None, None, None, HEAD_DIM_A, PAGE_SIZE), index)

    tile_specs = [tile_spec(h, j, i_page) for h in range(N_HEADS_A) for j in range(MOBA_TOPK)
                  for i_page in range(PAGES_PER_BLOCK)]
    cols_spec = pl.BlockSpec((None, 3, N_HEADS_A, HEAD_DIM_A, 1), lambda i, t, pt, sel_: (seq(i, t), 0, 0, 0, 0))
    raw_spec = pl.BlockSpec((None, GATE_PAGES_PER_STEP, N_HEADS_A, PAGE_SIZE),
                            lambda i, t, pt, sel_: (seq(i, t), 0, 0, 0))
    nc = len(raws)
    return _Rider(
        prefetch=(page_table, sel),
        in_specs=[cols_spec] + [raw_spec] * nc + tile_specs,
        out_specs=[pl.BlockSpec((None, N_HEADS_A, HEAD_DIM_A, 1), lambda i, t, pt, sel_: (seq(i, t), 0, 0, 0))],
        out_shape=[jax.ShapeDtypeStruct((db, N_HEADS_A, HEAD_DIM_A, 1), F32)],
        operands=(cols,) + tuple(raws) + (vt_pool,) * len(tile_specs),
        body=lambda pf, ins, outs, s: _moba_attend(pf[1], ins[0], ins[1:1 + nc], ins[1 + nc:], outs[0], s,
                                                   n_pages * PAGE_SIZE))


def _dot_nt(a, b):
    return lax.dot_general(a, b, (((1,), (1,)), ((), ())), preferred_element_type=F32)


def _diff_decode_kernel(pt_ref, q_ref, kn_ref, vn_ref, slope_ref, tbias_ref, lam_ref, subln_ref,
                        *rest, past_len, lam_init):
    del pt_ref
    npg = PAGES_PER_STEP
    k_refs, v_refs = rest[:npg], rest[npg:2 * npg]
    o_ref, m_sc, l_sc, acc_sc = rest[2 * npg:]
    j = pl.program_id(1)
    q8 = q_ref[...]
    slope8 = slope_ref[...]
    tbias = tbias_ref[...]
    lane_max = lambda x: jnp.max(x, axis=-1, keepdims=True)

    @pl.when(j == 0)
    def _():
        s_self = jnp.sum(_bf16_round(q8) * _bf16_round(kn_ref[...]), axis=-1, keepdims=True)
        m_sc[...] = jnp.broadcast_to(s_self, m_sc.shape)
        l_sc[...] = jnp.ones(l_sc.shape, F32)
        acc_sc[...] = vn_ref[...]

    q8b = q8.astype(BF16)
    scores = []
    for idx in range(npg):
        dist0 = (past_len - (j * npg + idx) * PAGE_SIZE).astype(F32)
        scores.append(_dot_nt(q8b, k_refs[idx][...].astype(BF16)) + tbias - slope8 * dist0)
    m_old = m_sc[...]
    m_step = lane_max(functools.reduce(jnp.maximum, scores))
    m_new = jnp.maximum(m_old, m_step)
    alpha = jnp.exp(m_old - m_new)
    m1 = m_new[:, 0:1]
    p_sum = jnp.zeros(scores[0].shape, F32)
    pv = jnp.zeros(acc_sc.shape, F32)
    for idx in range(npg):
        p = jnp.exp(scores[idx] - m1)
        p_sum = p_sum + p
        pv = pv + _dot(p.astype(BF16), v_refs[idx][...].astype(BF16))
    l_new = alpha * l_sc[...] + jnp.sum(p_sum, axis=-1, keepdims=True)
    acc_new = alpha * acc_sc[...] + pv
    m_sc[...] = m_new
    l_sc[...] = l_new
    acc_sc[...] = acc_new

    @pl.when(j == pl.num_programs(1) - 1)
    def _():
        o_c = acc_new * (1.0 / l_new)
        o = o_c - _lambda(lam_ref, lam_init) * pltpu.roll(o_c, N_HEADS_B, axis=0)
        ms = jnp.mean(o * o, axis=-1, keepdims=True)
        o_ref[...] = o * lax.rsqrt(ms + RMS_EPS) * subln_ref[...] * (1.0 - lam_init)


def _diff_decode(q, k_new, v_new, k_pool, v_pool, page_table, layer, lam_vecs, subln, lam_init):
    db, n_pages = page_table.shape
    depth, n_phys = k_pool.shape[:2]
    rows = PAGE_SIZE * N_HEADS_B
    pages = lambda pool: pool.reshape(depth * n_phys, rows, V_DIM_B)
    twice = lambda a: jnp.concatenate([a, a], axis=1)
    branch = (np.arange(V_DIM_B) // HEAD_DIM_B)[None, :] == np.arange(2)[:, None]
    q8 = jnp.concatenate([q * branch[0].astype(np.float32), q * branch[1].astype(np.float32)], axis=1)
    slopes = np.tile(np.asarray(_slopes(N_HEADS_B), np.float32), 2)
    col = np.arange(rows)
    own = (col % N_HEADS_B)[None, :] == (np.arange(SUBLANES) % N_HEADS_B)[:, None]
    tbias_np = np.where(own, slopes[:, None] * (col // N_HEADS_B)[None, :], NEG).astype(np.float32)
    seq = pl.BlockSpec((None, SUBLANES, V_DIM_B), lambda s, j, pt: (s, 0, 0))
    cst = lambda shp: pl.BlockSpec(shp, lambda s, j, pt: (0, 0))
    page = lambda i: pl.BlockSpec(
        (None, rows, V_DIM_B), lambda s, j, pt: (layer * n_phys + pt[s, j * PAGES_PER_STEP + i], 0, 0))
    page_specs = [page(i) for i in range(PAGES_PER_STEP)]
    stat = pltpu.VMEM((SUBLANES, LANES), F32)
    grid_spec = pltpu.PrefetchScalarGridSpec(
        num_scalar_prefetch=1,
        grid=(db, n_pages // PAGES_PER_STEP),
        in_specs=[seq, seq, seq, cst((SUBLANES, 1)), cst((SUBLANES, rows)), cst((4, HEAD_DIM_B)), cst((1, V_DIM_B))]
        + page_specs * 2,
        out_specs=seq,
        scratch_shapes=[stat, stat, stat],
    )
    o = pl.pallas_call(
        functools.partial(_diff_decode_kernel, past_len=n_pages * PAGE_SIZE, lam_init=lam_init),
        grid_spec=grid_spec,
        out_shape=jax.ShapeDtypeStruct((db, SUBLANES, V_DIM_B), F32),
        compiler_params=_params("arbitrary", "arbitrary"),
        name="diff_decode",
    )(page_table, q8, twice(k_new), twice(v_new), jnp.asarray(slopes[:, None]), jnp.asarray(tbias_np),
      lam_vecs, subln, *([pages(k_pool)] * PAGES_PER_STEP), *([pages(v_pool)] * PAGES_PER_STEP))
    return o[:, :N_HEADS_B]


def _block_diag_ones():
    i = np.arange(MXU_DIM) // 64
    return jnp.asarray((i[:, None] == i[None, :]).astype(np.float32), BF16)


def kernel(x_prompt, x_sample, cache_k_moba, cache_v_moba, cache_k_diff, cache_v_diff, page_table, c_prompt, c_sample, w_ada, b_ada, norm_ffn1, ffn1_w_gate, ffn1_w_up, ffn1_w_down, norm_mix, w_in, qn_moba, kn_moba, qn_diff, kn_diff, lambda_q1, lambda_k1, lambda_q2, lambda_k2, subln_diff, w_branch_moba, w_branch_diff, w_out, norm_ffn2, ffn2_w_gate, ffn2_w_up, ffn2_w_down):
    depth = w_ada.shape[0]
    b, s, d = x_prompt.shape
    db, t_new, _ = x_sample.shape
    assert t_new == 1 and s % TOKEN_TILE == 0 and db % SUBLANES == 0
    n_pages = page_table.shape[1]
    assert n_pages % PAGES_PER_STEP == 0
    bd = _block_diag_ones()
    tile8 = lambda v: jnp.tile(v, WIDTH // v.shape[0]).reshape(1, WIDTH)

    y_p, y_s = x_prompt, x_sample.reshape(1, db, d)
    rows_p, rows_s = [], []
    for l in range(depth):
        lam_init = _lambda_init(l)
        bf = lambda w: w[l].astype(BF16)
        row = lambda v: v[l].reshape(1, -1)
        mod = _ada(jnp.concatenate([c_prompt, c_sample], axis=0), w_ada[l], b_ada[l])
        mod_p = mod[:b].reshape(b, N_MOD, 1, d)
        mod_s = mod[b:].reshape(db, N_MOD, d).transpose(1, 0, 2).reshape(1, N_MOD, db, d)
        ffn1 = (row(norm_ffn1), bf(ffn1_w_gate), bf(ffn1_w_up), bf(ffn1_w_down))
        ffn2 = (row(norm_ffn2), bf(ffn2_w_gate), bf(ffn2_w_up), bf(ffn2_w_down))
        w_in_bf = bf(w_in)
        gains = jnp.concatenate([tile8(qn_moba[l]), tile8(kn_moba[l]), tile8(qn_diff[l]), tile8(kn_diff[l])], 0)
        lam_vecs = jnp.stack([lambda_q1[l], lambda_k1[l], lambda_q2[l], lambda_k2[l]])
        mix_w = (bf(w_branch_moba), bf(w_branch_diff), bf(w_out))

        (x1s,) = _ffn(y_s, mod_s, *ffn1, k0=0, tm=db)
        (q_as, k_as, v_as, q_bs, k_bs, v_bs, g_as, g_bs) = _proj(
            x1s, mod_s, row(norm_mix), w_in_bf, gains, bd, db, False)
        heads_a = lambda a: a.reshape(db, N_HEADS_A, HEAD_DIM_A)
        heads_b = lambda a: a.reshape(db, N_HEADS_B, V_DIM_B)
        cols = jnp.stack([heads_a(q_as), heads_a(k_as), heads_a(v_as)], axis=1)[..., None]

        steps = s // TOKEN_TILE
        kt_pool, vt_pool = _moba_pool_tiles(cache_k_moba), _moba_pool_tiles(cache_v_moba)
        gate_rider = lambda first: _gate_rider(page_table, cols, kt_pool, l, first, steps)
        x1, gate0, raw0 = _ffn(y_p, mod_p, *ffn1, k0=0, tm=TOKEN_TILE, rider=gate_rider(0))
        (kt_a, vt32_a, k_b, v_b, ka_a, ka_b, qt_a, qt_b, vt_a, vt_b, km_a, g_a, g_b, gate1, raw1) = _proj(
            x1, mod_p, row(norm_mix), w_in_bf, gains, bd, TOKEN_TILE, True, rider=gate_rider(GATE_PAGES_PER_STEP))
        o_a, o_b = _prompt_attention(qt_a, ka_a, vt_a, km_a.reshape(b, -1, WIDTH), qt_b, ka_b, vt_b,
                                     lam_vecs, tile8(subln_diff[l]), lam_init)
        attend = _attend_rider(page_table, _moba_select([gate0, gate1]), cols, [raw0, raw1], vt_pool, l, steps)
        y_p, o_as = _mix_ffn(x1, o_a, o_b, g_a, g_b, mod_p, *mix_w, *ffn2, tm=TOKEN_TILE, rider=attend)
        token_major = lambda a: a.reshape(b, N_HEADS_A, HEAD_DIM_A, s).transpose(0, 3, 1, 2)
        rows_p.append((token_major(kt_a), token_major(vt32_a), k_b, v_b))

        o_bs = _diff_decode(heads_b(q_bs), heads_b(k_bs), heads_b(v_bs), cache_k_diff, cache_v_diff,
                            page_table, l, lam_vecs, subln_diff[l].reshape(1, V_DIM_B), lam_init)
        as_rows = lambda a: a.reshape(1, db, WIDTH).astype(BF16)
        (y_s,) = _mix_ffn(x1s, as_rows(o_as), as_rows(o_bs), g_as, g_bs, mod_s, *mix_w, *ffn2, tm=db)
        rows_s.append((k_as, v_as, k_bs, v_bs))

    def stack(rows, i, lead, heads, hd):
        return jnp.stack([r[i].reshape(lead + (heads, hd)) for r in rows])

    lp, ls = (b, s), (db, 1)
    return (y_p, y_s.reshape(db, 1, d),
            stack(rows_p, 0, lp, N_HEADS_A, HEAD_DIM_A), stack(rows_p, 1, lp, N_HEADS_A, HEAD_DIM_A),
            stack(rows_p, 2, lp, N_HEADS_B, V_DIM_B), stack(rows_p, 3, lp, N_HEADS_B, V_DIM_B),
            stack(rows_s, 0, ls, N_HEADS_A, HEAD_DIM_A), stack(rows_s, 1, ls, N_HEADS_A, HEAD_DIM_A),
            stack(rows_s, 2, ls, N_HEADS_B, V_DIM_B), stack(rows_s, 3, ls, N_HEADS_B, V_DIM_B))
```

```python
import functools
import math
from typing import Callable, NamedTuple

import jax
import jax.numpy as jnp
import numpy as np
from jax import lax
from jax.experimental import pallas as pl
from jax.experimental.pallas import tpu as pltpu

F32 = jnp.float32
BF16 = jnp.bfloat16

N_HEADS_A = 8
HEAD_DIM_A = 64
MOBA_BLOCK = 256
MOBA_TOPK = 3
N_HEADS_B = 4
HEAD_DIM_B = 64
V_DIM_B = 2 * HEAD_DIM_B
WIDTH = 512
GROUP = 64
N_GROUPS = WIDTH // GROUP
PAGE_SIZE = 128
PAGES_PER_BLOCK = MOBA_BLOCK // PAGE_SIZE
N_MOD = 9
RMS_EPS = 1e-6
QK_SCALE = GROUP ** -0.5
NEG = -1e30

LANES = 128
SUBLANES = 8
MXU_DIM = 256
VMEM_LIMIT_BYTES = 58 * 1024 * 1024

TOKEN_TILE = 512
MAX_FF_TILE = 1408
ADA_STEPS = 4
PAGES_PER_STEP = 32
GATE_PAGES_PER_STEP = 32


def _slopes(n_heads):
    return [2.0 ** (-8.0 * (i + 1) / n_heads) for i in range(n_heads)]


def _lambda_init(layer):
    return 0.8 - 0.6 * math.exp(-0.3 * layer)


def _dot(a, b):
    return jnp.dot(a, b, preferred_element_type=F32)


def _bf16_round(x):
    return x.astype(BF16).astype(F32)


def _rms(x, w):
    ms = jnp.mean(x * x, axis=-1, keepdims=True)
    return x * lax.rsqrt(ms + RMS_EPS) * w


def _pick_ff_tile(d_ff, rows):
    cap = MXU_DIM if rows >= MXU_DIM else MAX_FF_TILE
    best = LANES
    for t in range(LANES, min(d_ff, cap) + 1, LANES):
        if d_ff % t == 0:
            best = t
    return best


def _params(*sem):
    return pltpu.CompilerParams(dimension_semantics=sem, vmem_limit_bytes=VMEM_LIMIT_BYTES)


def _const_spec(shape):
    nd = len(shape)
    return pl.BlockSpec(shape, lambda *_: (0,) * nd)


def _ada_kernel(c_ref, w_ref, b_ref, o_ref):
    c = c_ref[...]
    s = c * jax.nn.sigmoid(c)
    o_ref[...] = _dot(s.astype(BF16), w_ref[...].astype(BF16)) + b_ref[...]


def _ada(c, w, b):
    m, d = c.shape
    n = w.shape[1]
    tn = n // ADA_STEPS if n % (ADA_STEPS * LANES) == 0 else n
    return pl.pallas_call(
        _ada_kernel,
        grid=(n // tn,),
        in_specs=[pl.BlockSpec((m, d), lambda j: (0, 0)),
                  pl.BlockSpec((d, tn), lambda j: (0, j)),
                  pl.BlockSpec((1, tn), lambda j: (0, j))],
        out_specs=pl.BlockSpec((m, tn), lambda j: (0, j)),
        out_shape=jax.ShapeDtypeStruct((m, n), F32),
        compiler_params=_params("arbitrary"),
        name="ada_mod",
    )(c, w, b.reshape(1, n))


def _ffn_update(x, shift, scale, gate, nw, wg_ref, wu_ref, wd_ref, tf):
    h = (_rms(x, nw) * (1.0 + scale) + shift).astype(BF16)
    d_ff = wg_ref.shape[1]
    acc = jnp.zeros(x.shape, F32)
    for j in range(d_ff // tf):
        g = _dot(h, wg_ref[:, j * tf:(j + 1) * tf])
        u = _dot(h, wu_ref[:, j * tf:(j + 1) * tf])
        a = (g * jax.nn.sigmoid(g) * u).astype(BF16)
        acc = acc + _dot(a, wd_ref[j * tf:(j + 1) * tf, :])
    return x + 0.5 * gate * acc


class _Rider(NamedTuple):
    prefetch: tuple
    in_specs: list
    out_specs: list
    out_shape: list
    operands: tuple
    body: Callable


def _gate_pages(q_ref, k_refs, gate_ref, s_ref):
    q = q_ref[...]
    q_bf = _bf16_round(q)
    for bi in range(len(k_refs) // PAGES_PER_BLOCK):
        pages = [k_refs[bi * PAGES_PER_BLOCK + i][...] for i in range(PAGES_PER_BLOCK)]
        for i, k_page in enumerate(pages):
            s_ref[bi * PAGES_PER_BLOCK + i] = jnp.sum(k_page * q, axis=1)
        ksum = jnp.sum(functools.reduce(jnp.add, pages), axis=-1, keepdims=True)
        gate_ref[bi] = jnp.sum(_bf16_round(ksum * (1.0 / MOBA_BLOCK)) * q_bf, axis=1)


def _gate_rider(page_table, cols, kt_pool, layer, first_page, steps_per_batch):
    db = page_table.shape[0]
    npg = GATE_PAGES_PER_STEP
    seq = lambda i, t: i * steps_per_batch + t
    page = lambda k: pl.BlockSpec(
        (None, None, N_HEADS_A, HEAD_DIM_A, PAGE_SIZE),
        lambda i, t, pt: (layer, pt[seq(i, t), first_page + k], 0, 0, 0))
    per_seq = lambda shp: pl.BlockSpec((None,) + shp, lambda i, t, pt: (seq(i, t), 0, 0, 0))
    q_spec = pl.BlockSpec((None, None, N_HEADS_A, HEAD_DIM_A, 1), lambda i, t, pt: (seq(i, t), 0, 0, 0, 0))
    return _Rider(
        prefetch=(page_table,),
        in_specs=[q_spec] + [page(k) for k in range(npg)],
        out_specs=[per_seq((npg // PAGES_PER_BLOCK, N_HEADS_A, 1)), per_seq((npg, N_HEADS_A, PAGE_SIZE))],
        out_shape=[jax.ShapeDtypeStruct((db, npg // PAGES_PER_BLOCK, N_HEADS_A, 1), F32),
                   jax.ShapeDtypeStruct((db, npg, N_HEADS_A, PAGE_SIZE), F32)],
        operands=(cols,) + (kt_pool,) * npg,
        body=lambda pf, ins, outs, s: _gate_pages(ins[0], ins[1:], *outs))


def _dense_call(body, name, grid, in_specs, out_specs, out_shape, operands, rider=None):
    n_in, n_out = len(in_specs), len(out_specs)
    if rider is None:
        def kernel_fn(*refs):
            body(refs[:n_in], refs[n_in:])
        return pl.pallas_call(kernel_fn, grid=grid, in_specs=in_specs, out_specs=out_specs, out_shape=out_shape,
                              compiler_params=_params("arbitrary", "arbitrary"), name=name)(*operands)
    n_pf = len(rider.prefetch)
    assert rider.out_shape[0].shape[0] == grid[0] * grid[1]

    def kernel_fn(*refs):
        pf, refs = refs[:n_pf], refs[n_pf:]
        n_all_in = n_in + len(rider.in_specs)
        ins, outs = refs[:n_all_in], refs[n_all_in:]
        body(ins[:n_in], outs[:n_out])
        rider.body(pf, ins[n_in:], outs[n_out:], pl.program_id(0) * grid[1] + pl.program_id(1))

    return pl.pallas_call(
        kernel_fn,
        grid_spec=pltpu.PrefetchScalarGridSpec(
            num_scalar_prefetch=n_pf, grid=grid, in_specs=list(in_specs) + rider.in_specs,
            out_specs=list(out_specs) + rider.out_specs),
        out_shape=list(out_shape) + rider.out_shape,
        compiler_params=_params("arbitrary", "arbitrary"),
        name=name,
    )(*rider.prefetch, *operands, *rider.operands)


def _ffn(x, mod, nw, wg, wu, wd, k0, tm, rider=None):
    b, s, d = x.shape
    r = mod.shape[2]
    d_ff = wg.shape[1]
    tf = _pick_ff_tile(d_ff, tm)

    def body(ins, outs):
        x_ref, mod_ref, nw_ref, wg_ref, wu_ref, wd_ref = ins
        outs[0][...] = _ffn_update(x_ref[...], mod_ref[k0], mod_ref[k0 + 1], mod_ref[k0 + 2],
                                   nw_ref[...], wg_ref, wu_ref, wd_ref, tf)

    return _dense_call(
        body, "ffn", (b, s // tm),
        in_specs=[pl.BlockSpec((None, tm, d), lambda i, t, *_: (i, t, 0)),
                  pl.BlockSpec((None, N_MOD, r, d), lambda i, t, *_: (i, 0, 0, 0)),
                  _const_spec((1, d)),
                  _const_spec((d, d_ff)), _const_spec((d, d_ff)), _const_spec((d_ff, d))],
        out_specs=[pl.BlockSpec((None, tm, d), lambda i, t, *_: (i, t, 0))],
        out_shape=[jax.ShapeDtypeStruct((b, s, d), F32)],
        operands=(x, mod, nw, wg, wu, wd), rider=rider)


def _head_norm(seg, gain_row, bd):
    sq = (seg * seg).astype(BF16)
    parts = [_dot(sq[:, c * MXU_DIM:(c + 1) * MXU_DIM], bd) for c in range(WIDTH // MXU_DIM)]
    ms = jnp.concatenate(parts, axis=1) * (1.0 / GROUP)
    return seg * lax.rsqrt(ms + RMS_EPS) * gain_row


def _proj_kernel(x_ref, mod_ref, nw_ref, win_ref, gains_ref, bd_ref, *outs, transposed):
    x = x_ref[...]
    d = x.shape[1]
    h = (_rms(x, nw_ref[...]) * (1.0 + mod_ref[4]) + mod_ref[3]).astype(BF16)
    bd = bd_ref[...]

    def seg(j):
        return _dot(h, win_ref[:, j * WIDTH:(j + 1) * WIDTH])

    q_a = _head_norm(seg(0), gains_ref[0:1, :], bd) * QK_SCALE
    k_a = _head_norm(seg(1), gains_ref[1:2, :], bd)
    v_a = seg(2)
    q_b = _head_norm(seg(3), gains_ref[2:3, :], bd) * QK_SCALE
    k_b = _head_norm(seg(4), gains_ref[3:4, :], bd)
    v_b = seg(5)
    g0 = 6 * WIDTH
    g_a = jax.nn.sigmoid(_dot(h, win_ref[:, g0:g0 + d]))
    g_b = jax.nn.sigmoid(_dot(h, win_ref[:, g0 + d:g0 + 2 * d]))

    if not transposed:
        (qa_ref, ka_ref, va_ref, qb_ref, kb_ref, vb_ref, ga_ref, gb_ref) = outs
        qa_ref[...] = q_a
        ka_ref[...] = k_a
        va_ref[...] = v_a
        qb_ref[...] = q_b
        kb_ref[...] = k_b
        vb_ref[...] = v_b
    else:
        (kta_ref, vta32_ref, kb_ref, vb_ref, kaa_ref, kab_ref, qta_ref, qtb_ref,
         vta_ref, vtb_ref, km_ref, ga_ref, gb_ref) = outs
        kta_ref[...] = k_a.T
        vt_a = v_a.T
        vta32_ref[...] = vt_a
        qt_a, qt_b, vt_b = q_a.T, q_b.T, v_b.T
        for r in range(x.shape[0] // MOBA_BLOCK):
            rows = slice(r * MOBA_BLOCK, (r + 1) * MOBA_BLOCK)
            n = pl.program_id(1) * (x.shape[0] // MOBA_BLOCK) + r
            for g, (ka_g, kb_g) in enumerate(zip(_augmented_keys(k_a[rows], n), _augmented_keys(k_b[rows], n))):
                kaa_ref[r, g] = ka_g
                kab_ref[r, g] = kb_g
            qta_ref[r] = qt_a[:, rows].astype(BF16)
            qtb_ref[r] = qt_b[:, rows].astype(BF16)
            vta_ref[r] = vt_a[:, rows].astype(BF16)
            vtb_ref[r] = vt_b[:, rows].astype(BF16)
            km_ref[r] = jnp.sum(k_a[rows], axis=0, keepdims=True) * (1.0 / MOBA_BLOCK)
        for hd in range(N_HEADS_B):
            lanes = slice(V_DIM_B * hd, V_DIM_B * (hd + 1))
            kb_ref[pl.ds(hd, x.shape[0], stride=N_HEADS_B), :] = k_b[:, lanes]
            vb_ref[pl.ds(hd, x.shape[0], stride=N_HEADS_B), :] = v_b[:, lanes]
    ga_ref[...] = g_a.astype(BF16)
    gb_ref[...] = g_b.astype(BF16)


def _proj(x, mod, nw, w_in, gains, bd, tm, transposed, rider=None):
    b, s, d = x.shape
    r = mod.shape[2]
    d_in = w_in.shape[1]
    row_spec = lambda w: pl.BlockSpec((None, tm, w), lambda i, t, *_: (i, t, 0))
    rows = lambda w, dt=F32: jax.ShapeDtypeStruct((b, s, w), dt)
    if transposed:
        nb, bpt = s // MOBA_BLOCK, tm // MOBA_BLOCK
        blk = lambda shp: pl.BlockSpec((None, bpt) + shp, lambda i, t, *_: (i, t, 0, 0))
        col_spec = pl.BlockSpec((None, WIDTH, tm), lambda i, t, *_: (i, 0, t))
        head_rows = pl.BlockSpec((None, tm * N_HEADS_B, V_DIM_B), lambda i, t, *_: (i, t, 0))
        k_aug = pl.BlockSpec((None, bpt, N_GROUPS, MOBA_BLOCK, LANES), lambda i, t, *_: (i, t, 0, 0, 0))
        out_specs = [col_spec] * 2 + [head_rows] * 2 + [k_aug] * 2 \
            + [blk((WIDTH, MOBA_BLOCK))] * 4 + [blk((1, WIDTH))] + [row_spec(d)] * 2
        out_shape = [jax.ShapeDtypeStruct((b, WIDTH, s), F32)] * 2 \
            + [jax.ShapeDtypeStruct((b, s * N_HEADS_B, V_DIM_B), F32)] * 2 \
            + [jax.ShapeDtypeStruct((b, nb, N_GROUPS, MOBA_BLOCK, LANES), BF16)] * 2 \
            + [jax.ShapeDtypeStruct((b, nb, WIDTH, MOBA_BLOCK), BF16)] * 4 \
            + [jax.ShapeDtypeStruct((b, nb, 1, WIDTH), F32)] + [rows(d, BF16)] * 2
    else:
        out_specs = [row_spec(WIDTH)] * 6 + [row_spec(d)] * 2
        out_shape = [rows(WIDTH)] * 6 + [rows(d, BF16)] * 2
    return _dense_call(
        lambda ins, outs: _proj_kernel(*ins, *outs, transposed=transposed), "mixer_proj", (b, s // tm),
        in_specs=[pl.BlockSpec((None, tm, d), lambda i, t, *_: (i, t, 0)),
                  pl.BlockSpec((None, N_MOD, r, d), lambda i, t, *_: (i, 0, 0, 0)),
                  _const_spec((1, d)), _const_spec((d, d_in)),
                  _const_spec((4, WIDTH)), _const_spec((MXU_DIM, MXU_DIM))],
        out_specs=out_specs, out_shape=out_shape, operands=(x, mod, nw, w_in, gains, bd), rider=rider)


FEATURE_ROWS = SUBLANES
MAX_KEY_BLOCKS = GROUP - FEATURE_ROWS


def _augmented_keys(k, n):
    lane = lax.broadcasted_iota(jnp.int32, (MOBA_BLOCK, LANES), 1)
    key = lax.broadcasted_iota(jnp.int32, (MOBA_BLOCK, LANES), 0).astype(F32)
    feat = jnp.where(lane == GROUP, 1.0, jnp.where(lane == GROUP + 1, key,
                                                    jnp.where(lane == GROUP + FEATURE_ROWS + n, 1.0, 0.0)))
    out = []
    for g in range(N_GROUPS):
        p, half = divmod(g, 2)
        pair = k[:, LANES * p:LANES * (p + 1)]
        k_g = pair if half == 0 else pltpu.roll(pair, GROUP, axis=1)
        out.append(jnp.where(lane < GROUP, k_g, feat).astype(BF16))
    return out


def _augmented_queries(qt, slopes, block_biases):
    qry = lax.broadcasted_iota(jnp.int32, (FEATURE_ROWS, MOBA_BLOCK), 1).astype(F32)
    row = lax.broadcasted_iota(jnp.int32, (FEATURE_ROWS, MOBA_BLOCK), 0)
    out = []
    for g, (slope, bias) in enumerate(zip(slopes, block_biases)):
        head = jnp.where(row == 0, -slope * qry, jnp.where(row == 1, slope, 0.0))
        pad = jnp.zeros((GROUP - FEATURE_ROWS - bias.shape[0], MOBA_BLOCK), F32)
        feat = jnp.concatenate([head, bias, pad], axis=0).astype(BF16)
        out.append(jnp.concatenate([qt[GROUP * g:GROUP * (g + 1), :], feat], axis=0))
    return out


def _pair_scores(k_aug, q_aug, p):
    return [_dot(k_aug[2 * p + half], q_aug[2 * p + half]) for half in range(2)]


def _attend_block(k_aug, vtb, q_aug, causal, m_all, l_all, acc_sc, v_rows, st_first, s0_sc, k_next):
    m_out, l_out = [], []
    n_pairs = N_GROUPS // 2
    st_next = _pair_scores(k_aug, q_aug, 0) if st_first is None else st_first
    for p in range(n_pairs):
        st_pair = st_next
        if p + 1 < n_pairs:
            st_next = _pair_scores(k_aug, q_aug, p + 1)
        else:
            nxt = _pair_scores(k_next, q_aug, 0)
            s0_sc[0] = nxt[0]
            s0_sc[1] = nxt[1]
        for half in range(2):
            g = 2 * p + half
            st = st_pair[half]
            if causal is not None:
                st = jnp.where(causal, st, NEG)
            m_old = m_all[g:g + 1, :]
            m_new = jnp.maximum(m_old, jnp.max(st, axis=0, keepdims=True))
            alpha = jnp.exp(m_old - m_new)
            pt = jnp.exp(st - m_new)
            l_out.append(alpha * l_all[g:g + 1, :] + jnp.sum(pt, axis=0, keepdims=True))
            rows = v_rows(g)
            nr = rows.stop - rows.start
            acc_rows = slice(g * nr, (g + 1) * nr)
            acc_sc[acc_rows, :] = alpha * acc_sc[acc_rows, :] + _dot(vtb[rows, :], pt.astype(BF16))
            m_out.append(m_new)
    return jnp.concatenate(m_out, axis=0), jnp.concatenate(l_out, axis=0)


def _attend_all_blocks(qi, k_ref, vt_ref, q_aug, acc_sc, s0_sc, v_rows):
    key_i = lax.broadcasted_iota(jnp.int32, (MOBA_BLOCK, MOBA_BLOCK), 0)
    qry_i = lax.broadcasted_iota(jnp.int32, (MOBA_BLOCK, MOBA_BLOCK), 1)
    acc_sc[...] = jnp.zeros(acc_sc.shape, F32)
    m0 = jnp.full((N_GROUPS, MOBA_BLOCK), NEG, F32)
    l0 = jnp.zeros((N_GROUPS, MOBA_BLOCK), F32)
    m1, l1 = _attend_block(k_ref.at[qi], vt_ref[qi], q_aug, key_i <= qry_i, m0, l0, acc_sc, v_rows,
                           None, s0_sc, k_ref.at[0])

    def past(n, carry):
        return _attend_block(k_ref.at[n], vt_ref[n], q_aug, None, *carry, acc_sc, v_rows,
                             [s0_sc[0], s0_sc[1]], s0_sc, k_ref.at[n + 1])

    _, l_fin = lax.fori_loop(0, qi, past, (m1, l1))
    return l_fin


def _block_distance(qi, nbp):
    blk_i = lax.broadcasted_iota(jnp.int32, (nbp, MOBA_BLOCK), 0)
    return blk_i, ((qi - blk_i) * MOBA_BLOCK).astype(F32)


def _moba_prompt_kernel(qt_ref, k_ref, vt_ref, km_ref, o_ref, acc_sc, s0_sc):
    qi = pl.program_id(1)
    nb = k_ref.shape[0]
    nbp = -(-nb // SUBLANES) * SUBLANES
    blk = MOBA_BLOCK
    slopes = _slopes(N_HEADS_A)
    qt = qt_ref[...]

    km = km_ref[...].astype(BF16)
    blk_i, blk_dist = _block_distance(qi, nbp)
    valid = blk_i[:nb] < qi
    biases = []
    for g in range(N_GROUPS):
        gate = _dot(km[:, GROUP * g:GROUP * (g + 1)], qt[GROUP * g:GROUP * (g + 1), :])
        gate = jnp.where(valid, gate, NEG)
        rank = jnp.zeros((nb, blk), jnp.int32)
        for m in range(nb):
            gm = gate[m:m + 1, :]
            beats = (gm > gate) | ((gm == gate) & (m < blk_i[:nb]))
            rank = rank + beats.astype(jnp.int32)
        drop = jnp.where(valid & (rank < MOBA_TOPK), 0.0, NEG)
        if nbp > nb:
            drop = jnp.concatenate([drop, jnp.zeros((nbp - nb, blk), F32)], axis=0)
        biases.append(jnp.where(blk_i == qi, 0.0, drop - slopes[g] * blk_dist))

    q_aug = _augmented_queries(qt, slopes, biases)
    l_fin = _attend_all_blocks(qi, k_ref, vt_ref, q_aug, acc_sc, s0_sc,
                               lambda g: slice(GROUP * g, GROUP * (g + 1)))
    parts = [acc_sc[GROUP * g:GROUP * (g + 1), :] * (1.0 / l_fin[g:g + 1, :]) for g in range(N_GROUPS)]
    o_ref[...] = jnp.concatenate(parts, axis=0).T.astype(BF16)


def _lambda(lam_ref, lam_init):
    a = jnp.sum(lam_ref[0:1, :] * lam_ref[1:2, :], axis=-1, keepdims=True)
    b = jnp.sum(lam_ref[2:3, :] * lam_ref[3:4, :], axis=-1, keepdims=True)
    return jnp.exp(a) - jnp.exp(b) + lam_init


def _diff_prompt_kernel(qt_ref, k_ref, vt_ref, lam_ref, subln_ref, o_ref, acc_sc, s0_sc, *, lam_init):
    qi = pl.program_id(1)
    nb = k_ref.shape[0]
    nbp = -(-nb // SUBLANES) * SUBLANES
    slopes = [s for s in _slopes(N_HEADS_B) for _ in range(2)]
    _, blk_dist = _block_distance(qi, nbp)
    q_aug = _augmented_queries(qt_ref[...], slopes, [-s * blk_dist for s in slopes])
    v_rows = lambda g: slice(V_DIM_B * (g // 2), V_DIM_B * (g // 2 + 1))
    l_fin = _attend_all_blocks(qi, k_ref, vt_ref, q_aug, acc_sc, s0_sc, v_rows)

    lam = _lambda(lam_ref, lam_init)
    parts = []
    for h in range(N_HEADS_B):
        o0 = acc_sc[V_DIM_B * (2 * h):V_DIM_B * (2 * h + 1), :] * (1.0 / l_fin[2 * h:2 * h + 1, :])
        o1 = acc_sc[V_DIM_B * (2 * h + 1):V_DIM_B * (2 * h + 2), :] * (1.0 / l_fin[2 * h + 1:2 * h + 2, :])
        o = o0 - lam * o1
        ms = jnp.mean(o * o, axis=0, keepdims=True)
        parts.append(o * lax.rsqrt(ms + RMS_EPS))
    o_t = jnp.concatenate(parts, axis=0).T
    o_ref[...] = (o_t * subln_ref[...] * (1.0 - lam_init)).astype(BF16)


def _prompt_attention(qt_a, k_aug_a, vt_a, km_a, qt_b, k_aug_b, vt_b, lam_vecs, subln_row, lam_init):
    b, nb = qt_a.shape[:2]
    assert nb <= MAX_KEY_BLOCKS
    blk = MOBA_BLOCK
    s = nb * blk
    q_spec = pl.BlockSpec((None, None, WIDTH, blk), lambda i, t: (i, t, 0, 0))
    k_spec = pl.BlockSpec((None, nb, N_GROUPS, blk, LANES), lambda i, t: (i, 0, 0, 0, 0))
    vt_spec = pl.BlockSpec((None, nb, WIDTH, blk), lambda i, t: (i, 0, 0, 0))
    o_spec = pl.BlockSpec((None, blk, WIDTH), lambda i, t: (i, t, 0))
    o_shape = jax.ShapeDtypeStruct((b, s, WIDTH), BF16)
    s0 = pltpu.VMEM((2, blk, blk), F32)
    o_a = pl.pallas_call(
        _moba_prompt_kernel,
        grid=(b, nb),
        in_specs=[q_spec, k_spec, vt_spec, pl.BlockSpec((None, nb, WIDTH), lambda i, t: (i, 0, 0))],
        out_specs=o_spec,
        out_shape=o_shape,
        scratch_shapes=[pltpu.VMEM((WIDTH, blk), F32), s0],
        compiler_params=_params("arbitrary", "arbitrary"),
        name="moba_prompt",
    )(qt_a, k_aug_a, vt_a, km_a)
    o_b = pl.pallas_call(
        functools.partial(_diff_prompt_kernel, lam_init=lam_init),
        grid=(b, nb),
        in_specs=[q_spec, k_spec, vt_spec, _const_spec((4, HEAD_DIM_B)), _const_spec((1, WIDTH))],
        out_specs=o_spec,
        out_shape=o_shape,
        scratch_shapes=[pltpu.VMEM((N_GROUPS * V_DIM_B, blk), F32), s0],
        compiler_params=_params("arbitrary", "arbitrary"),
        name="diff_prompt",
    )(qt_b, k_aug_b, vt_b, lam_vecs, subln_row)
    return o_a, o_b


def _mix_ffn_kernel(x_ref, oa_ref, ob_ref, ga_ref, gb_ref, mod_ref, wba_ref, wbd_ref, wout_ref,
                    nw_ref, wg_ref, wu_ref, wd_ref, o_ref, *, tf):
    y_a = _dot(oa_ref[...], wba_ref[...])
    y_b = _dot(ob_ref[...], wbd_ref[...])
    mixed = _dot((ga_ref[...].astype(F32) * y_a + gb_ref[...].astype(F32) * y_b).astype(BF16), wout_ref[...])
    x = x_ref[...] + mod_ref[5] * mixed
    o_ref[...] = _ffn_update(x, mod_ref[6], mod_ref[7], mod_ref[8], nw_ref[...],
                             wg_ref, wu_ref, wd_ref, tf)


def _mix_ffn(x, o_a, o_b, g_a, g_b, mod, w_ba, w_bd, w_out, nw, wg, wu, wd, tm, rider=None):
    b, s, d = x.shape
    r = mod.shape[2]
    d_ff = wg.shape[1]
    tf = _pick_ff_tile(d_ff, tm)
    row_spec = lambda w: pl.BlockSpec((None, tm, w), lambda i, t, *_: (i, t, 0))
    return _dense_call(
        lambda ins, outs: _mix_ffn_kernel(*ins, *outs, tf=tf), "mix_ffn", (b, s // tm),
        in_specs=[row_spec(d), row_spec(WIDTH), row_spec(WIDTH), row_spec(d), row_spec(d),
                  pl.BlockSpec((None, N_MOD, r, d), lambda i, t, *_: (i, 0, 0, 0)),
                  _const_spec((WIDTH, d)), _const_spec((WIDTH, d)), _const_spec((d, d)),
                  _const_spec((1, d)),
                  _const_spec((d, d_ff)), _const_spec((d, d_ff)), _const_spec((d_ff, d))],
        out_specs=[row_spec(d)], out_shape=[jax.ShapeDtypeStruct((b, s, d), F32)],
        operands=(x, o_a, o_b, g_a, g_b, mod, w_ba, w_bd, w_out, nw, wg, wu, wd), rider=rider)


def _moba_select_kernel(gate_ref, sel_ref):
    gate = gate_ref[...]
    nblk = gate.shape[1]
    blk_i = lax.broadcasted_iota(jnp.int32, gate.shape, 1)
    rank = jnp.zeros(gate.shape, jnp.int32)
    for m in range(nblk):
        gm = gate[:, m:m + 1]
        beats = (gm > gate) | ((gm == gate) & (m < blk_i))
        rank = rank + beats.astype(jnp.int32)
    lane = lax.broadcasted_iota(jnp.int32, sel_ref.shape, 1)
    sel = jnp.zeros(sel_ref.shape, jnp.int32)
    for j in range(MOBA_TOPK):
        sel = jnp.where(lane == j, jnp.sum(jnp.where(rank == j, blk_i, 0), axis=1, keepdims=True), sel)
    sel_ref[...] = sel


def _moba_attend(sel_ref, cols_ref, s_refs, v_refs, o_ref, s, past_len):
    npg = GATE_PAGES_PER_STEP
    lane = lax.broadcasted_iota(jnp.int32, (1, PAGE_SIZE), 1).astype(F32)
    slopes = _slopes(N_HEADS_A)
    for h in range(N_HEADS_A):
        q_h = cols_ref[0, h]
        tiles = range(h * MOBA_TOPK * PAGES_PER_BLOCK, (h + 1) * MOBA_TOPK * PAGES_PER_BLOCK)
        rows = []
        for j in range(MOBA_TOPK):
            blk = sel_ref[s, j * N_HEADS_A + h]
            for i in range(PAGES_PER_BLOCK):
                page = PAGES_PER_BLOCK * blk + i
                raw = s_refs[0][jnp.clip(page, 0, npg - 1), h:h + 1, :]
                for c in range(1, len(s_refs)):
                    other = s_refs[c][jnp.clip(page - c * npg, 0, npg - 1), h:h + 1, :]
                    raw = jnp.where(page >= c * npg, other, raw)
                dist0 = (past_len - blk * MOBA_BLOCK - i * PAGE_SIZE).astype(F32)
                rows.append(raw - slopes[h] * (dist0 - lane))
        s_self = jnp.sum(q_h * cols_ref[1, h], axis=0, keepdims=True)
        m = s_self
        for r in rows:
            m = jnp.maximum(m, jnp.max(r, axis=1, keepdims=True))
        w_self = jnp.exp(s_self - m)
        l = w_self
        acc = jnp.zeros((HEAD_DIM_A, PAGE_SIZE), F32)
        for t, r in zip(tiles, rows):
            p = jnp.exp(r - m)
            l = l + jnp.sum(p, axis=1, keepdims=True)
            acc = acc + p * v_refs[t][...]
        o = jnp.sum(acc, axis=1, keepdims=True) + w_self * cols_ref[2, h]
        o_ref[h] = o * (1.0 / l)


def _moba_pool_tiles(pool):
    return jnp.transpose(pool, (0, 1, 3, 4, 2))


def _moba_select(gates):
    gate = jnp.concatenate(gates, axis=1)[..., 0]
    db, nblk = gate.shape[:2]
    assert nblk >= MOBA_TOPK
    gate = gate.transpose(0, 2, 1).reshape(db * N_HEADS_A, nblk)
    sel = pl.pallas_call(
        _moba_select_kernel,
        grid=(1,),
        in_specs=[_const_spec(gate.shape)],
        out_specs=_const_spec((db * N_HEADS_A, LANES)),
        out_shape=jax.ShapeDtypeStruct((db * N_HEADS_A, LANES), jnp.int32),
        compiler_params=_params("arbitrary"),
        name="moba_select",
    )(gate)
    return sel[:, :MOBA_TOPK].reshape(db, N_HEADS_A, MOBA_TOPK).transpose(0, 2, 1).reshape(db, -1)


def _attend_rider(page_table, sel, cols, raws, vt_pool, layer, steps_per_batch):
    db, n_pages = page_table.shape
    nblk = n_pages // PAGES_PER_BLOCK
    assert len(raws) * GATE_PAGES_PER_STEP == n_pages
    seq = lambda i, t: i * steps_per_batch + t

    def tile_spec(h, j, i_page):
        def index(i, t, pt, sel_):
            blk = jnp.clip(sel_[seq(i, t), j * N_HEADS_A + h], 0, nblk - 1)
            return (layer, pt[seq(i, t), PAGES_PER_BLOCK * blk + i_page], h, 0, 0)
        return pl.BlockSpec((None, None, None, HEAD_DIM_A, PAGE_SIZE), index)

    tile_specs = [tile_spec(h, j, i_page) for h in range(N_HEADS_A) for j in range(MOBA_TOPK)
                  for i_page in range(PAGES_PER_BLOCK)]
    cols_spec = pl.BlockSpec((None, 3, N_HEADS_A, HEAD_DIM_A, 1), lambda i, t, pt, sel_: (seq(i, t), 0, 0, 0, 0))
    raw_spec = pl.BlockSpec((None, GATE_PAGES_PER_STEP, N_HEADS_A, PAGE_SIZE),
                            lambda i, t, pt, sel_: (seq(i, t), 0, 0, 0))
    nc = len(raws)
    return _Rider(
        prefetch=(page_table, sel),
        in_specs=[cols_spec] + [raw_spec] * nc + tile_specs,
        out_specs=[pl.BlockSpec((None, N_HEADS_A, HEAD_DIM_A, 1), lambda i, t, pt, sel_: (seq(i, t), 0, 0, 0))],
        out_shape=[jax.ShapeDtypeStruct((db, N_HEADS_A, HEAD_DIM_A, 1), F32)],
        operands=(cols,) + tuple(raws) + (vt_pool,) * len(tile_specs),
        body=lambda pf, ins, outs, s: _moba_attend(pf[1], ins[0], ins[1:1 + nc], ins[1 + nc:], outs[0], s,
                                                   n_pages * PAGE_SIZE))


def _dot_nt(a, b):
    return lax.dot_general(a, b, (((1,), (1,)), ((), ())), preferred_element_type=F32)


def _diff_decode_kernel(pt_ref, q_ref, kn_ref, vn_ref, slope_ref, tbias_ref, lam_ref, subln_ref,
                        *rest, past_len, lam_init):
    del pt_ref
    npg = PAGES_PER_STEP
    k_refs, v_refs = rest[:npg], rest[npg:2 * npg]
    o_ref, m_sc, l_sc, acc_sc = rest[2 * npg:]
    j = pl.program_id(1)
    q8 = q_ref[...]
    slope8 = slope_ref[...]
    tbias = tbias_ref[...]
    lane_max = lambda x: jnp.max(x, axis=-1, keepdims=True)

    @pl.when(j == 0)
    def _():
        s_self = jnp.sum(_bf16_round(q8) * _bf16_round(kn_ref[...]), axis=-1, keepdims=True)
        m_sc[...] = jnp.broadcast_to(s_self, m_sc.shape)
        l_sc[...] = jnp.ones(l_sc.shape, F32)
        acc_sc[...] = vn_ref[...]

    q8b = q8.astype(BF16)
    scores = []
    for idx in range(npg):
        dist0 = (past_len - (j * npg + idx) * PAGE_SIZE).astype(F32)
        scores.append(_dot_nt(q8b, k_refs[idx][...].astype(BF16)) + tbias - slope8 * dist0)
    m_old = m_sc[...]
    m_step = lane_max(functools.reduce(jnp.maximum, scores))
    m_new = jnp.maximum(m_old, m_step)
    alpha = jnp.exp(m_old - m_new)
    m1 = m_new[:, 0:1]
    p_sum = jnp.zeros(scores[0].shape, F32)
    pv = jnp.zeros(acc_sc.shape, F32)
    for idx in range(npg):
        p = jnp.exp(scores[idx] - m1)
        p_sum = p_sum + p
        pv = pv + _dot(p.astype(BF16), v_refs[idx][...].astype(BF16))
    l_new = alpha * l_sc[...] + jnp.sum(p_sum, axis=-1, keepdims=True)
    acc_new = alpha * acc_sc[...] + pv
    m_sc[...] = m_new
    l_sc[...] = l_new
    acc_sc[...] = acc_new

    @pl.when(j == pl.num_programs(1) - 1)
    def _():
        o_c = acc_new * (1.0 / l_new)
        o = o_c - _lambda(lam_ref, lam_init) * pltpu.roll(o_c, N_HEADS_B, axis=0)
        ms = jnp.mean(o * o, axis=-1, keepdims=True)
        o_ref[...] = o * lax.rsqrt(ms + RMS_EPS) * subln_ref[...] * (1.0 - lam_init)


def _diff_decode(q, k_new, v_new, k_pool, v_pool, page_table, layer, lam_vecs, subln, lam_init):
    db, n_pages = page_table.shape
    depth, n_phys = k_pool.shape[:2]
    rows = PAGE_SIZE * N_HEADS_B
    pages = lambda pool: pool.reshape(depth * n_phys, rows, V_DIM_B)
    twice = lambda a: jnp.concatenate([a, a], axis=1)
    branch = (np.arange(V_DIM_B) // HEAD_DIM_B)[None, :] == np.arange(2)[:, None]
    q8 = jnp.concatenate([q * branch[0].astype(np.float32), q * branch[1].astype(np.float32)], axis=1)
    slopes = np.tile(np.asarray(_slopes(N_HEADS_B), np.float32), 2)
    col = np.arange(rows)
    own = (col % N_HEADS_B)[None, :] == (np.arange(SUBLANES) % N_HEADS_B)[:, None]
    tbias_np = np.where(own, slopes[:, None] * (col // N_HEADS_B)[None, :], NEG).astype(np.float32)
    seq = pl.BlockSpec((None, SUBLANES, V_DIM_B), lambda s, j, pt: (s, 0, 0))
    cst = lambda shp: pl.BlockSpec(shp, lambda s, j, pt: (0, 0))
    page = lambda i: pl.BlockSpec(
        (None, rows, V_DIM_B), lambda s, j, pt: (layer * n_phys + pt[s, j * PAGES_PER_STEP + i], 0, 0))
    page_specs = [page(i) for i in range(PAGES_PER_STEP)]
    stat = pltpu.VMEM((SUBLANES, LANES), F32)
    grid_spec = pltpu.PrefetchScalarGridSpec(
        num_scalar_prefetch=1,
        grid=(db, n_pages // PAGES_PER_STEP),
        in_specs=[seq, seq, seq, cst((SUBLANES, 1)), cst((SUBLANES, rows)), cst((4, HEAD_DIM_B)), cst((1, V_DIM_B))]
        + page_specs * 2,
        out_specs=seq,
        scratch_shapes=[stat, stat, stat],
    )
    o = pl.pallas_call(
        functools.partial(_diff_decode_kernel, past_len=n_pages * PAGE_SIZE, lam_init=lam_init),
        grid_spec=grid_spec,
        out_shape=jax.ShapeDtypeStruct((db, SUBLANES, V_DIM_B), F32),
        compiler_params=_params("arbitrary", "arbitrary"),
        name="diff_decode",
    )(page_table, q8, twice(k_new), twice(v_new), jnp.asarray(slopes[:, None]), jnp.asarray(tbias_np),
      lam_vecs, subln, *([pages(k_pool)] * PAGES_PER_STEP), *([pages(v_pool)] * PAGES_PER_STEP))
    return o[:, :N_HEADS_B]


def _block_diag_ones():
    i = np.arange(MXU_DIM) // GROUP
    return jnp.asarray((i[:, None] == i[None, :]).astype(np.float32), BF16)


def kernel(x_prompt, x_sample, cache_k_moba, cache_v_moba, cache_k_diff, cache_v_diff, page_table, c_prompt, c_sample, w_ada, b_ada, norm_ffn1, ffn1_w_gate, ffn1_w_up, ffn1_w_down, norm_mix, w_in, qn_moba, kn_moba, qn_diff, kn_diff, lambda_q1, lambda_k1, lambda_q2, lambda_k2, subln_diff, w_branch_moba, w_branch_diff, w_out, norm_ffn2, ffn2_w_gate, ffn2_w_up, ffn2_w_down):
    depth = w_ada.shape[0]
    b, s, d = x_prompt.shape
    db, t_new, _ = x_sample.shape
    assert t_new == 1 and s % TOKEN_TILE == 0 and db % SUBLANES == 0
    n_pages = page_table.shape[1]
    assert n_pages % PAGES_PER_STEP == 0
    bd = _block_diag_ones()
    tile8 = lambda v: jnp.tile(v, WIDTH // v.shape[0]).reshape(1, WIDTH)

    y_p, y_s = x_prompt, x_sample.reshape(1, db, d)
    rows_p, rows_s = [], []
    for l in range(depth):
        lam_init = _lambda_init(l)
        bf = lambda w: w[l].astype(BF16)
        row = lambda v: v[l].reshape(1, -1)
        mod = _ada(jnp.concatenate([c_prompt, c_sample], axis=0), w_ada[l], b_ada[l])
        mod_p = mod[:b].reshape(b, N_MOD, 1, d)
        mod_s = mod[b:].reshape(db, N_MOD, d).transpose(1, 0, 2).reshape(1, N_MOD, db, d)
        ffn1 = (row(norm_ffn1), bf(ffn1_w_gate), bf(ffn1_w_up), bf(ffn1_w_down))
        ffn2 = (row(norm_ffn2), bf(ffn2_w_gate), bf(ffn2_w_up), bf(ffn2_w_down))
        w_in_bf = bf(w_in)
        gains = jnp.concatenate([tile8(qn_moba[l]), tile8(kn_moba[l]), tile8(qn_diff[l]), tile8(kn_diff[l])], 0)
        lam_vecs = jnp.stack([lambda_q1[l], lambda_k1[l], lambda_q2[l], lambda_k2[l]])
        mix_w = (bf(w_branch_moba), bf(w_branch_diff), bf(w_out))

        (x1s,) = _ffn(y_s, mod_s, *ffn1, k0=0, tm=db)
        (q_as, k_as, v_as, q_bs, k_bs, v_bs, g_as, g_bs) = _proj(
            x1s, mod_s, row(norm_mix), w_in_bf, gains, bd, db, False)
        heads_a = lambda a: a.reshape(db, N_HEADS_A, HEAD_DIM_A)
        heads_b = lambda a: a.reshape(db, N_HEADS_B, V_DIM_B)
        cols = jnp.stack([heads_a(q_as), heads_a(k_as), heads_a(v_as)], axis=1)[..., None]

        steps = s // TOKEN_TILE
        kt_pool, vt_pool = _moba_pool_tiles(cache_k_moba), _moba_pool_tiles(cache_v_moba)
        gate_rider = lambda first: _gate_rider(page_table, cols, kt_pool, l, first, steps)
        x1, gate0, raw0 = _ffn(y_p, mod_p, *ffn1, k0=0, tm=TOKEN_TILE, rider=gate_rider(0))
        (kt_a, vt32_a, k_b, v_b, ka_a, ka_b, qt_a, qt_b, vt_a, vt_b, km_a, g_a, g_b, gate1, raw1) = _proj(
            x1, mod_p, row(norm_mix), w_in_bf, gains, bd, TOKEN_TILE, True, rider=gate_rider(GATE_PAGES_PER_STEP))
        o_a, o_b = _prompt_attention(qt_a, ka_a, vt_a, km_a.reshape(b, -1, WIDTH), qt_b, ka_b, vt_b,
                                     lam_vecs, tile8(subln_diff[l]), lam_init)
        attend = _attend_rider(page_table, _moba_select([gate0, gate1]), cols, [raw0, raw1], vt_pool, l, steps)
        y_p, o_as = _mix_ffn(x1, o_a, o_b, g_a, g_b, mod_p, *mix_w, *ffn2, tm=TOKEN_TILE, rider=attend)
        token_major = lambda a: a.reshape(b, N_HEADS_A, HEAD_DIM_A, s).transpose(0, 3, 1, 2)
        rows_p.append((token_major(kt_a), token_major(vt32_a), k_b, v_b))

        o_bs = _diff_decode(heads_b(q_bs), heads_b(k_bs), heads_b(v_bs), cache_k_diff, cache_v_diff,
                            page_table, l, lam_vecs, subln_diff[l].reshape(1, V_DIM_B), lam_init)
        as_rows = lambda a: a.reshape(1, db, WIDTH).astype(BF16)
        (y_s,) = _mix_ffn(x1s, as_rows(o_as), as_rows(o_bs), g_as, g_bs, mod_s, *mix_w, *ffn2, tm=db)
        rows_s.append((k_as, v_as, k_bs, v_bs))

    def stack(rows, i, lead, heads, hd):
        return jnp.stack([r[i].reshape(lead + (heads, hd)) for r in rows])

    lp, ls = (b, s), (db, 1)
    return (y_p, y_s.reshape(db, 1, d),
            stack(rows_p, 0, lp, N_HEADS_A, HEAD_DIM_A), stack(rows_p, 1, lp, N_HEADS_A, HEAD_DIM_A),
            stack(rows_p, 2, lp, N_HEADS_B, V_DIM_B), stack(rows_p, 3, lp, N_HEADS_B, V_DIM_B),
            stack(rows_s, 0, ls, N_HEADS_A, HEAD_DIM_A), stack(rows_s, 1, ls, N_HEADS_A, HEAD_DIM_A),
            stack(rows_s, 2, ls, N_HEADS_B, V_DIM_B), stack(rows_s, 3, ls, N_HEADS_B, V_DIM_B))
```

```python
import functools
import math
from typing import Callable, NamedTuple

import jax
import jax.numpy as jnp
import numpy as np
from jax import lax
from jax.experimental import pallas as pl
from jax.experimental.pallas import tpu as pltpu

F32 = jnp.float32
BF16 = jnp.bfloat16

N_HEADS_A = 8
HEAD_DIM_A = 64
MOBA_BLOCK = 256
MOBA_TOPK = 3
N_HEADS_B = 4
HEAD_DIM_B = 64
V_DIM_B = 2 * HEAD_DIM_B
WIDTH = 512
GROUP = 64
N_GROUPS = WIDTH // GROUP
PAGE_SIZE = 128
PAGES_PER_BLOCK = MOBA_BLOCK // PAGE_SIZE
N_MOD = 9
RMS_EPS = 1e-6
QK_SCALE = GROUP ** -0.5
NEG = -1e30

LANES = 128
SUBLANES = 8
BF16_SUBLANES = 16
MXU_DIM = 256
VMEM_LIMIT_BYTES = 58 * 1024 * 1024

TOKEN_TILE = 512
MAX_FF_TILE = 1408
ADA_STEPS = 4
PAGES_PER_STEP = 32
GATE_PAGES_PER_STEP = 32


def _slopes(n_heads):
    return [2.0 ** (-8.0 * (i + 1) / n_heads) for i in range(n_heads)]


def _lambda_init(layer):
    return 0.8 - 0.6 * math.exp(-0.3 * layer)


def _dot(a, b):
    return jnp.dot(a, b, preferred_element_type=F32)


def _bf16_round(x):
    return x.astype(BF16).astype(F32)


def _rms(x, w):
    ms = jnp.mean(x * x, axis=-1, keepdims=True)
    return x * lax.rsqrt(ms + RMS_EPS) * w


def _pick_ff_tile(d_ff, rows):
    cap = MXU_DIM if rows >= MXU_DIM else MAX_FF_TILE
    best = LANES
    for t in range(LANES, min(d_ff, cap) + 1, LANES):
        if d_ff % t == 0:
            best = t
    return best


def _params(*sem):
    return pltpu.CompilerParams(dimension_semantics=sem, vmem_limit_bytes=VMEM_LIMIT_BYTES)


def _const_spec(shape):
    nd = len(shape)
    return pl.BlockSpec(shape, lambda *_: (0,) * nd)


def _ada_kernel(c_ref, w_ref, b_ref, o_ref):
    c = c_ref[...]
    s = c * jax.nn.sigmoid(c)
    o_ref[...] = _dot(s.astype(BF16), w_ref[...].astype(BF16)) + b_ref[...]


def _ada(c, w, b):
    m, d = c.shape
    n = w.shape[1]
    tn = n // ADA_STEPS if n % (ADA_STEPS * LANES) == 0 else n
    return pl.pallas_call(
        _ada_kernel,
        grid=(n // tn,),
        in_specs=[pl.BlockSpec((m, d), lambda j: (0, 0)),
                  pl.BlockSpec((d, tn), lambda j: (0, j)),
                  pl.BlockSpec((1, tn), lambda j: (0, j))],
        out_specs=pl.BlockSpec((m, tn), lambda j: (0, j)),
        out_shape=jax.ShapeDtypeStruct((m, n), F32),
        compiler_params=_params("arbitrary"),
        name="ada_mod",
    )(c, w, b.reshape(1, n))


def _ffn_update(x, shift, scale, gate, nw, wg_ref, wu_ref, wd_ref, tf):
    h = (_rms(x, nw) * (1.0 + scale) + shift).astype(BF16)
    d_ff = wg_ref.shape[1]
    acc = jnp.zeros(x.shape, F32)
    for j in range(d_ff // tf):
        g = _dot(h, wg_ref[:, j * tf:(j + 1) * tf])
        u = _dot(h, wu_ref[:, j * tf:(j + 1) * tf])
        a = (g * jax.nn.sigmoid(g) * u).astype(BF16)
        acc = acc + _dot(a, wd_ref[j * tf:(j + 1) * tf, :])
    return x + 0.5 * gate * acc


class _Rider(NamedTuple):
    prefetch: tuple
    in_specs: list
    out_specs: list
    out_shape: list
    operands: tuple
    body: Callable


def _gate_pages(q_ref, k_refs, gate_ref, s_ref):
    q = q_ref[...]
    q_bf = _bf16_round(q)
    for bi in range(len(k_refs) // PAGES_PER_BLOCK):
        pages = [k_refs[bi * PAGES_PER_BLOCK + i][...] for i in range(PAGES_PER_BLOCK)]
        for i, k_page in enumerate(pages):
            s_ref[bi * PAGES_PER_BLOCK + i] = jnp.sum(k_page * q, axis=1)
        ksum = jnp.sum(functools.reduce(jnp.add, pages), axis=-1, keepdims=True)
        gate_ref[bi] = jnp.sum(_bf16_round(ksum * (1.0 / MOBA_BLOCK)) * q_bf, axis=1)


def _gate_rider(page_table, cols, kt_pool, layer, first_page, steps_per_batch):
    db = page_table.shape[0]
    npg = GATE_PAGES_PER_STEP
    seq = lambda i, t: i * steps_per_batch + t
    page = lambda k: pl.BlockSpec(
        (None, None, N_HEADS_A, HEAD_DIM_A, PAGE_SIZE),
        lambda i, t, pt: (layer, pt[seq(i, t), first_page + k], 0, 0, 0))
    per_seq = lambda shp: pl.BlockSpec((None,) + shp, lambda i, t, pt: (seq(i, t), 0, 0, 0))
    q_spec = pl.BlockSpec((None, None, N_HEADS_A, HEAD_DIM_A, 1), lambda i, t, pt: (seq(i, t), 0, 0, 0, 0))
    return _Rider(
        prefetch=(page_table,),
        in_specs=[q_spec] + [page(k) for k in range(npg)],
        out_specs=[per_seq((npg // PAGES_PER_BLOCK, N_HEADS_A, 1)), per_seq((npg, N_HEADS_A, PAGE_SIZE))],
        out_shape=[jax.ShapeDtypeStruct((db, npg // PAGES_PER_BLOCK, N_HEADS_A, 1), F32),
                   jax.ShapeDtypeStruct((db, npg, N_HEADS_A, PAGE_SIZE), F32)],
        operands=(cols,) + (kt_pool,) * npg,
        body=lambda pf, ins, outs, s: _gate_pages(ins[0], ins[1:], *outs))


def _dense_call(body, name, grid, in_specs, out_specs, out_shape, operands, rider=None):
    n_in, n_out = len(in_specs), len(out_specs)
    if rider is None:
        def kernel_fn(*refs):
            body(refs[:n_in], refs[n_in:])
        return pl.pallas_call(kernel_fn, grid=grid, in_specs=in_specs, out_specs=out_specs, out_shape=out_shape,
                              compiler_params=_params("arbitrary", "arbitrary"), name=name)(*operands)
    n_pf = len(rider.prefetch)
    assert rider.out_shape[0].shape[0] == grid[0] * grid[1]

    def kernel_fn(*refs):
        pf, refs = refs[:n_pf], refs[n_pf:]
        n_all_in = n_in + len(rider.in_specs)
        ins, outs = refs[:n_all_in], refs[n_all_in:]
        body(ins[:n_in], outs[:n_out])
        rider.body(pf, ins[n_in:], outs[n_out:], pl.program_id(0) * grid[1] + pl.program_id(1))

    return pl.pallas_call(
        kernel_fn,
        grid_spec=pltpu.PrefetchScalarGridSpec(
            num_scalar_prefetch=n_pf, grid=grid, in_specs=list(in_specs) + rider.in_specs,
            out_specs=list(out_specs) + rider.out_specs),
        out_shape=list(out_shape) + rider.out_shape,
        compiler_params=_params("arbitrary", "arbitrary"),
        name=name,
    )(*rider.prefetch, *operands, *rider.operands)


def _ffn(x, mod, nw, wg, wu, wd, k0, tm, rider=None):
    b, s, d = x.shape
    r = mod.shape[2]
    d_ff = wg.shape[1]
    tf = _pick_ff_tile(d_ff, tm)

    def body(ins, outs):
        x_ref, mod_ref, nw_ref, wg_ref, wu_ref, wd_ref = ins
        outs[0][...] = _ffn_update(x_ref[...], mod_ref[k0], mod_ref[k0 + 1], mod_ref[k0 + 2],
                                   nw_ref[...], wg_ref, wu_ref, wd_ref, tf)

    return _dense_call(
        body, "ffn", (b, s // tm),
        in_specs=[pl.BlockSpec((None, tm, d), lambda i, t, *_: (i, t, 0)),
                  pl.BlockSpec((None, N_MOD, r, d), lambda i, t, *_: (i, 0, 0, 0)),
                  _const_spec((1, d)),
                  _const_spec((d, d_ff)), _const_spec((d, d_ff)), _const_spec((d_ff, d))],
        out_specs=[pl.BlockSpec((None, tm, d), lambda i, t, *_: (i, t, 0))],
        out_shape=[jax.ShapeDtypeStruct((b, s, d), F32)],
        operands=(x, mod, nw, wg, wu, wd), rider=rider)


def _head_norm(seg, gain_row, bd):
    sq = (seg * seg).astype(BF16)
    parts = [_dot(sq[:, c * MXU_DIM:(c + 1) * MXU_DIM], bd) for c in range(WIDTH // MXU_DIM)]
    ms = jnp.concatenate(parts, axis=1) * (1.0 / GROUP)
    return seg * lax.rsqrt(ms + RMS_EPS) * gain_row


def _proj_kernel(x_ref, mod_ref, nw_ref, win_ref, gains_ref, bd_ref, *outs, transposed):
    x = x_ref[...]
    d = x.shape[1]
    h = (_rms(x, nw_ref[...]) * (1.0 + mod_ref[4]) + mod_ref[3]).astype(BF16)
    bd = bd_ref[...]

    def seg(j):
        return _dot(h, win_ref[:, j * WIDTH:(j + 1) * WIDTH])

    q_a = _head_norm(seg(0), gains_ref[0:1, :], bd) * QK_SCALE
    k_a = _head_norm(seg(1), gains_ref[1:2, :], bd)
    v_a = seg(2)
    q_b = _head_norm(seg(3), gains_ref[2:3, :], bd) * QK_SCALE
    k_b = _head_norm(seg(4), gains_ref[3:4, :], bd)
    v_b = seg(5)
    g0 = 6 * WIDTH
    g_a = jax.nn.sigmoid(_dot(h, win_ref[:, g0:g0 + d]))
    g_b = jax.nn.sigmoid(_dot(h, win_ref[:, g0 + d:g0 + 2 * d]))

    if not transposed:
        (qa_ref, ka_ref, va_ref, qb_ref, kb_ref, vb_ref, ga_ref, gb_ref) = outs
        qa_ref[...] = q_a
        ka_ref[...] = k_a
        va_ref[...] = v_a
        qb_ref[...] = q_b
        kb_ref[...] = k_b
        vb_ref[...] = v_b
    else:
        (kta_ref, vta32_ref, kb_ref, vb_ref, kaa_ref, kab_ref, qta_ref, qtb_ref,
         vta_ref, vtb_ref, km_ref, ga_ref, gb_ref) = outs
        kta_ref[...] = k_a.T
        vt_a = v_a.T
        vta32_ref[...] = vt_a
        qt_a, qt_b, vt_b = q_a.T, q_b.T, v_b.T
        for r in range(x.shape[0] // MOBA_BLOCK):
            rows = slice(r * MOBA_BLOCK, (r + 1) * MOBA_BLOCK)
            n = pl.program_id(1) * (x.shape[0] // MOBA_BLOCK) + r
            for g, (ka_g, kb_g) in enumerate(zip(_augmented_keys(k_a[rows], n), _augmented_keys(k_b[rows], n))):
                kaa_ref[r, g] = ka_g
                kab_ref[r, g] = kb_g
            qta_ref[r] = qt_a[:, rows].astype(BF16)
            qtb_ref[r] = qt_b[:, rows].astype(BF16)
            vta_ref[r] = vt_a[:, rows].astype(BF16)
            vtb_ref[r] = vt_b[:, rows].astype(BF16)
            km_ref[r] = jnp.sum(k_a[rows], axis=0, keepdims=True) * (1.0 / MOBA_BLOCK)
        for hd in range(N_HEADS_B):
            lanes = slice(V_DIM_B * hd, V_DIM_B * (hd + 1))
            kb_ref[pl.ds(hd, x.shape[0], stride=N_HEADS_B), :] = k_b[:, lanes]
            vb_ref[pl.ds(hd, x.shape[0], stride=N_HEADS_B), :] = v_b[:, lanes]
    ga_ref[...] = g_a.astype(BF16)
    gb_ref[...] = g_b.astype(BF16)


def _proj(x, mod, nw, w_in, gains, bd, tm, transposed, rider=None):
    b, s, d = x.shape
    r = mod.shape[2]
    d_in = w_in.shape[1]
    row_spec = lambda w: pl.BlockSpec((None, tm, w), lambda i, t, *_: (i, t, 0))
    rows = lambda w, dt=F32: jax.ShapeDtypeStruct((b, s, w), dt)
    if transposed:
        nb, bpt = s // MOBA_BLOCK, tm // MOBA_BLOCK
        blk = lambda shp: pl.BlockSpec((None, bpt) + shp, lambda i, t, *_: (i, t, 0, 0))
        col_spec = pl.BlockSpec((None, WIDTH, tm), lambda i, t, *_: (i, 0, t))
        head_rows = pl.BlockSpec((None, tm * N_HEADS_B, V_DIM_B), lambda i, t, *_: (i, t, 0))
        k_aug = pl.BlockSpec((None, bpt, N_GROUPS, MOBA_BLOCK, LANES), lambda i, t, *_: (i, t, 0, 0, 0))
        out_specs = [col_spec] * 2 + [head_rows] * 2 + [k_aug] * 2 \
            + [blk((WIDTH, MOBA_BLOCK))] * 4 + [blk((1, WIDTH))] + [row_spec(d)] * 2
        out_shape = [jax.ShapeDtypeStruct((b, WIDTH, s), F32)] * 2 \
            + [jax.ShapeDtypeStruct((b, s * N_HEADS_B, V_DIM_B), F32)] * 2 \
            + [jax.ShapeDtypeStruct((b, nb, N_GROUPS, MOBA_BLOCK, LANES), BF16)] * 2 \
            + [jax.ShapeDtypeStruct((b, nb, WIDTH, MOBA_BLOCK), BF16)] * 4 \
            + [jax.ShapeDtypeStruct((b, nb, 1, WIDTH), F32)] + [rows(d, BF16)] * 2
    else:
        out_specs = [row_spec(WIDTH)] * 6 + [row_spec(d)] * 2
        out_shape = [rows(WIDTH)] * 6 + [rows(d, BF16)] * 2
    return _dense_call(
        lambda ins, outs: _proj_kernel(*ins, *outs, transposed=transposed), "mixer_proj", (b, s // tm),
        in_specs=[pl.BlockSpec((None, tm, d), lambda i, t, *_: (i, t, 0)),
                  pl.BlockSpec((None, N_MOD, r, d), lambda i, t, *_: (i, 0, 0, 0)),
                  _const_spec((1, d)), _const_spec((d, d_in)),
                  _const_spec((4, WIDTH)), _const_spec((MXU_DIM, MXU_DIM))],
        out_specs=out_specs, out_shape=out_shape, operands=(x, mod, nw, w_in, gains, bd), rider=rider)


FEATURE_ROWS = SUBLANES
MAX_KEY_BLOCKS = GROUP - FEATURE_ROWS


def _augmented_keys(k, n):
    lane = lax.broadcasted_iota(jnp.int32, (MOBA_BLOCK, LANES), 1)
    key = lax.broadcasted_iota(jnp.int32, (MOBA_BLOCK, LANES), 0).astype(F32)
    feat = jnp.where(lane == GROUP, 1.0, jnp.where(lane == GROUP + 1, key,
                                                    jnp.where(lane == GROUP + FEATURE_ROWS + n, 1.0, 0.0)))
    out = []
    for g in range(N_GROUPS):
        p, half = divmod(g, 2)
        pair = k[:, LANES * p:LANES * (p + 1)]
        k_g = pair if half == 0 else pltpu.roll(pair, GROUP, axis=1)
        out.append(jnp.where(lane < GROUP, k_g, feat).astype(BF16))
    return out


def _augmented_queries(qt, slopes, block_biases):
    qry = lax.broadcasted_iota(jnp.int32, (FEATURE_ROWS, MOBA_BLOCK), 1).astype(F32)
    row = lax.broadcasted_iota(jnp.int32, (FEATURE_ROWS, MOBA_BLOCK), 0)
    out = []
    for g, (slope, bias) in enumerate(zip(slopes, block_biases)):
        head = jnp.where(row == 0, -slope * qry, jnp.where(row == 1, slope, 0.0))
        pad = jnp.zeros((GROUP - FEATURE_ROWS - bias.shape[0], MOBA_BLOCK), F32)
        feat = jnp.concatenate([head, bias, pad], axis=0).astype(BF16)
        out.append(jnp.concatenate([qt[GROUP * g:GROUP * (g + 1), :], feat], axis=0))
    return out


def _pair_scores(k_aug, q_aug, p):
    return [_dot(k_aug[2 * p + half], q_aug[2 * p + half]) for half in range(2)]


def _attend_block(k_aug, vtb, q_aug, causal, m_all, l_all, acc_sc, v_rows, st_first, s0_sc, k_next):
    m_out, l_out = [], []
    n_pairs = N_GROUPS // 2
    ones_rows = jnp.ones((BF16_SUBLANES, MOBA_BLOCK), BF16)
    st_next = _pair_scores(k_aug, q_aug, 0) if st_first is None else st_first
    for p in range(n_pairs):
        st_pair = st_next
        if p + 1 < n_pairs:
            st_next = _pair_scores(k_aug, q_aug, p + 1)
        else:
            nxt = _pair_scores(k_next, q_aug, 0)
            s0_sc[0] = nxt[0]
            s0_sc[1] = nxt[1]
        for half in range(2):
            g = 2 * p + half
            st = st_pair[half]
            if causal is not None:
                st = jnp.where(causal, st, NEG)
            m_old = m_all[g:g + 1, :]
            m_new = jnp.maximum(m_old, jnp.max(st, axis=0, keepdims=True))
            alpha = jnp.exp(m_old - m_new)
            pt = jnp.exp(st - m_new).astype(BF16)
            rows = v_rows(g)
            nr = rows.stop - rows.start
            acc_rows = slice(g * nr, (g + 1) * nr)
            pv = _dot(jnp.concatenate([vtb[rows, :], ones_rows], axis=0), pt)
            l_out.append(alpha * l_all[g:g + 1, :] + pv[nr:nr + 1, :])
            acc_sc[acc_rows, :] = alpha * acc_sc[acc_rows, :] + pv[:nr, :]
            m_out.append(m_new)
    return jnp.concatenate(m_out, axis=0), jnp.concatenate(l_out, axis=0)


def _attend_all_blocks(qi, k_ref, vt_ref, q_aug, acc_sc, s0_sc, v_rows):
    key_i = lax.broadcasted_iota(jnp.int32, (MOBA_BLOCK, MOBA_BLOCK), 0)
    qry_i = lax.broadcasted_iota(jnp.int32, (MOBA_BLOCK, MOBA_BLOCK), 1)
    acc_sc[...] = jnp.zeros(acc_sc.shape, F32)
    m0 = jnp.full((N_GROUPS, MOBA_BLOCK), NEG, F32)
    l0 = jnp.zeros((N_GROUPS, MOBA_BLOCK), F32)
    m1, l1 = _attend_block(k_ref.at[qi], vt_ref[qi], q_aug, key_i <= qry_i, m0, l0, acc_sc, v_rows,
                           None, s0_sc, k_ref.at[0])

    def past(n, carry):
        return _attend_block(k_ref.at[n], vt_ref[n], q_aug, None, *carry, acc_sc, v_rows,
                             [s0_sc[0], s0_sc[1]], s0_sc, k_ref.at[n + 1])

    _, l_fin = lax.fori_loop(0, qi, past, (m1, l1))
    return l_fin


def _block_distance(qi, nbp):
    blk_i = lax.broadcasted_iota(jnp.int32, (nbp, MOBA_BLOCK), 0)
    return blk_i, ((qi - blk_i) * MOBA_BLOCK).astype(F32)


def _moba_prompt_kernel(qt_ref, k_ref, vt_ref, km_ref, o_ref, acc_sc, s0_sc):
    qi = pl.program_id(1)
    nb = k_ref.shape[0]
    nbp = -(-nb // SUBLANES) * SUBLANES
    blk = MOBA_BLOCK
    slopes = _slopes(N_HEADS_A)
    qt = qt_ref[...]

    km = km_ref[...].astype(BF16)
    blk_i, blk_dist = _block_distance(qi, nbp)
    valid = blk_i[:nb] < qi
    biases = []
    for g in range(N_GROUPS):
        gate = _dot(km[:, GROUP * g:GROUP * (g + 1)], qt[GROUP * g:GROUP * (g + 1), :])
        gate = jnp.where(valid, gate, NEG)
        rank = jnp.zeros((nb, blk), jnp.int32)
        for m in range(nb):
            gm = gate[m:m + 1, :]
            beats = (gm > gate) | ((gm == gate) & (m < blk_i[:nb]))
            rank = rank + beats.astype(jnp.int32)
        drop = jnp.where(valid & (rank < MOBA_TOPK), 0.0, NEG)
        if nbp > nb:
            drop = jnp.concatenate([drop, jnp.zeros((nbp - nb, blk), F32)], axis=0)
        biases.append(jnp.where(blk_i == qi, 0.0, drop - slopes[g] * blk_dist))

    q_aug = _augmented_queries(qt, slopes, biases)
    l_fin = _attend_all_blocks(qi, k_ref, vt_ref, q_aug, acc_sc, s0_sc,
                               lambda g: slice(GROUP * g, GROUP * (g + 1)))
    parts = [acc_sc[GROUP * g:GROUP * (g + 1), :] * (1.0 / l_fin[g:g + 1, :]) for g in range(N_GROUPS)]
    o_ref[...] = jnp.concatenate(parts, axis=0).T.astype(BF16)


def _lambda(lam_ref, lam_init):
    a = jnp.sum(lam_ref[0:1, :] * lam_ref[1:2, :], axis=-1, keepdims=True)
    b = jnp.sum(lam_ref[2:3, :] * lam_ref[3:4, :], axis=-1, keepdims=True)
    return jnp.exp(a) - jnp.exp(b) + lam_init


def _diff_prompt_kernel(qt_ref, k_ref, vt_ref, lam_ref, subln_ref, o_ref, acc_sc, s0_sc, *, lam_init):
    qi = pl.program_id(1)
    nb = k_ref.shape[0]
    nbp = -(-nb // SUBLANES) * SUBLANES
    slopes = [s for s in _slopes(N_HEADS_B) for _ in range(2)]
    _, blk_dist = _block_distance(qi, nbp)
    q_aug = _augmented_queries(qt_ref[...], slopes, [-s * blk_dist for s in slopes])
    v_rows = lambda g: slice(V_DIM_B * (g // 2), V_DIM_B * (g // 2 + 1))
    l_fin = _attend_all_blocks(qi, k_ref, vt_ref, q_aug, acc_sc, s0_sc, v_rows)

    lam = _lambda(lam_ref, lam_init)
    parts = []
    for h in range(N_HEADS_B):
        o0 = acc_sc[V_DIM_B * (2 * h):V_DIM_B * (2 * h + 1), :] * (1.0 / l_fin[2 * h:2 * h + 1, :])
        o1 = acc_sc[V_DIM_B * (2 * h + 1):V_DIM_B * (2 * h + 2), :] * (1.0 / l_fin[2 * h + 1:2 * h + 2, :])
        o = o0 - lam * o1
        ms = jnp.mean(o * o, axis=0, keepdims=True)
        parts.append(o * lax.rsqrt(ms + RMS_EPS))
    o_t = jnp.concatenate(parts, axis=0).T
    o_ref[...] = (o_t * subln_ref[...] * (1.0 - lam_init)).astype(BF16)


def _prompt_attention(qt_a, k_aug_a, vt_a, km_a, qt_b, k_aug_b, vt_b, lam_vecs, subln_row, lam_init):
    b, nb = qt_a.shape[:2]
    assert nb <= MAX_KEY_BLOCKS
    blk = MOBA_BLOCK
    s = nb * blk
    q_spec = pl.BlockSpec((None, None, WIDTH, blk), lambda i, t: (i, t, 0, 0))
    k_spec = pl.BlockSpec((None, nb, N_GROUPS, blk, LANES), lambda i, t: (i, 0, 0, 0, 0))
    vt_spec = pl.BlockSpec((None, nb, WIDTH, blk), lambda i, t: (i, 0, 0, 0))
    o_spec = pl.BlockSpec((None, blk, WIDTH), lambda i, t: (i, t, 0))
    o_shape = jax.ShapeDtypeStruct((b, s, WIDTH), BF16)
    s0 = pltpu.VMEM((2, blk, blk), F32)
    o_a = pl.pallas_call(
        _moba_prompt_kernel,
        grid=(b, nb),
        in_specs=[q_spec, k_spec, vt_spec, pl.BlockSpec((None, nb, WIDTH), lambda i, t: (i, 0, 0))],
        out_specs=o_spec,
        out_shape=o_shape,
        scratch_shapes=[pltpu.VMEM((WIDTH, blk), F32), s0],
        compiler_params=_params("arbitrary", "arbitrary"),
        name="moba_prompt",
    )(qt_a, k_aug_a, vt_a, km_a)
    o_b = pl.pallas_call(
        functools.partial(_diff_prompt_kernel, lam_init=lam_init),
        grid=(b, nb),
        in_specs=[q_spec, k_spec, vt_spec, _const_spec((4, HEAD_DIM_B)), _const_spec((1, WIDTH))],
        out_specs=o_spec,
        out_shape=o_shape,
        scratch_shapes=[pltpu.VMEM((N_GROUPS * V_DIM_B, blk), F32), s0],
        compiler_params=_params("arbitrary", "arbitrary"),
        name="diff_prompt",
    )(qt_b, k_aug_b, vt_b, lam_vecs, subln_row)
    return o_a, o_b


def _mix_ffn_kernel(x_ref, oa_ref, ob_ref, ga_ref, gb_ref, mod_ref, wba_ref, wbd_ref, wout_ref,
                    nw_ref, wg_ref, wu_ref, wd_ref, o_ref, *, tf):
    y_a = _dot(oa_ref[...], wba_ref[...])
    y_b = _dot(ob_ref[...], wbd_ref[...])
    mixed = _dot((ga_ref[...].astype(F32) * y_a + gb_ref[...].astype(F32) * y_b).astype(BF16), wout_ref[...])
    x = x_ref[...] + mod_ref[5] * mixed
    o_ref[...] = _ffn_update(x, mod_ref[6], mod_ref[7], mod_ref[8], nw_ref[...],
                             wg_ref, wu_ref, wd_ref, tf)


def _mix_ffn(x, o_a, o_b, g_a, g_b, mod, w_ba, w_bd, w_out, nw, wg, wu, wd, tm, rider=None):
    b, s, d = x.shape
    r = mod.shape[2]
    d_ff = wg.shape[1]
    tf = _pick_ff_tile(d_ff, tm)
    row_spec = lambda w: pl.BlockSpec((None, tm, w), lambda i, t, *_: (i, t, 0))
    return _dense_call(
        lambda ins, outs: _mix_ffn_kernel(*ins, *outs, tf=tf), "mix_ffn", (b, s // tm),
        in_specs=[row_spec(d), row_spec(WIDTH), row_spec(WIDTH), row_spec(d), row_spec(d),
                  pl.BlockSpec((None, N_MOD, r, d), lambda i, t, *_: (i, 0, 0, 0)),
                  _const_spec((WIDTH, d)), _const_spec((WIDTH, d)), _const_spec((d, d)),
                  _const_spec((1, d)),
                  _const_spec((d, d_ff)), _const_spec((d, d_ff)), _const_spec((d_ff, d))],
        out_specs=[row_spec(d)], out_shape=[jax.ShapeDtypeStruct((b, s, d), F32)],
        operands=(x, o_a, o_b, g_a, g_b, mod, w_ba, w_bd, w_out, nw, wg, wu, wd), rider=rider)


def _moba_select_kernel(gate_ref, sel_ref):
    gate = gate_ref[...]
    nblk = gate.shape[1]
    blk_i = lax.broadcasted_iota(jnp.int32, gate.shape, 1)
    rank = jnp.zeros(gate.shape, jnp.int32)
    for m in range(nblk):
        gm = gate[:, m:m + 1]
        beats = (gm > gate) | ((gm == gate) & (m < blk_i))
        rank = rank + beats.astype(jnp.int32)
    lane = lax.broadcasted_iota(jnp.int32, sel_ref.shape, 1)
    sel = jnp.zeros(sel_ref.shape, jnp.int32)
    for j in range(MOBA_TOPK):
        sel = jnp.where(lane == j, jnp.sum(jnp.where(rank == j, blk_i, 0), axis=1, keepdims=True), sel)
    sel_ref[...] = sel


def _moba_attend(sel_ref, cols_ref, s_refs, v_refs, o_ref, s, past_len):
    npg = GATE_PAGES_PER_STEP
    lane = lax.broadcasted_iota(jnp.int32, (1, PAGE_SIZE), 1).astype(F32)
    slopes = _slopes(N_HEADS_A)
    for h in range(N_HEADS_A):
        q_h = cols_ref[0, h]
        tiles = range(h * MOBA_TOPK * PAGES_PER_BLOCK, (h + 1) * MOBA_TOPK * PAGES_PER_BLOCK)
        rows = []
        for j in range(MOBA_TOPK):
            blk = sel_ref[s, j * N_HEADS_A + h]
            for i in range(PAGES_PER_BLOCK):
                page = PAGES_PER_BLOCK * blk + i
                raw = s_refs[0][jnp.clip(page, 0, npg - 1), h:h + 1, :]
                for c in range(1, len(s_refs)):
                    other = s_refs[c][jnp.clip(page - c * npg, 0, npg - 1), h:h + 1, :]
                    raw = jnp.where(page >= c * npg, other, raw)
                dist0 = (past_len - blk * MOBA_BLOCK - i * PAGE_SIZE).astype(F32)
                rows.append(raw - slopes[h] * (dist0 - lane))
        s_self = jnp.sum(q_h * cols_ref[1, h], axis=0, keepdims=True)
        m = s_self
        for r in rows:
            m = jnp.maximum(m, jnp.max(r, axis=1, keepdims=True))
        w_self = jnp.exp(s_self - m)
        l = w_self
        acc = jnp.zeros((HEAD_DIM_A, PAGE_SIZE), F32)
        for t, r in zip(tiles, rows):
            p = jnp.exp(r - m)
            l = l + jnp.sum(p, axis=1, keepdims=True)
            acc = acc + p * v_refs[t][...]
        o = jnp.sum(acc, axis=1, keepdims=True) + w_self * cols_ref[2, h]
        o_ref[h] = o * (1.0 / l)


def _moba_pool_tiles(pool):
    return jnp.transpose(pool, (0, 1, 3, 4, 2))


def _moba_select(gates):
    gate = jnp.concatenate(gates, axis=1)[..., 0]
    db, nblk = gate.shape[:2]
    assert nblk >= MOBA_TOPK
    gate = gate.transpose(0, 2, 1).reshape(db * N_HEADS_A, nblk)
    sel = pl.pallas_call(
        _moba_select_kernel,
        grid=(1,),
        in_specs=[_const_spec(gate.shape)],
        out_specs=_const_spec((db * N_HEADS_A, LANES)),
        out_shape=jax.ShapeDtypeStruct((db * N_HEADS_A, LANES), jnp.int32),
        compiler_params=_params("arbitrary"),
        name="moba_select",
    )(gate)
    return sel[:, :MOBA_TOPK].reshape(db, N_HEADS_A, MOBA_TOPK).transpose(0, 2, 1).reshape(db, -1)


def _attend_rider(page_table, sel, cols, raws, vt_pool, layer, steps_per_batch):
    db, n_pages = page_table.shape
    nblk = n_pages // PAGES_PER_BLOCK
    assert len(raws) * GATE_PAGES_PER_STEP == n_pages
    seq = lambda i, t: i * steps_per_batch + t

    def tile_spec(h, j, i_page):
        def index(i, t, pt, sel_):
            blk = jnp.clip(sel_[seq(i, t), j * N_HEADS_A + h], 0, nblk - 1)
            return (layer, pt[seq(i, t), PAGES_PER_BLOCK * blk + i_page], h, 0, 0)
        return pl.BlockSpec((None, None, None, HEAD_DIM_A, PAGE_SIZE), index)

    tile_specs = [tile_spec(h, j, i_page) for h in range(N_HEADS_A) for j in range(MOBA_TOPK)
                  for i_page in range(PAGES_PER_BLOCK)]
    cols_spec = pl.BlockSpec((None, 3, N_HEADS_A, HEAD_DIM_A, 1), lambda i, t, pt, sel_: (seq(i, t), 0, 0, 0, 0))
    raw_spec = pl.BlockSpec((None, GATE_PAGES_PER_STEP, N_HEADS_A, PAGE_SIZE),
                            lambda i, t, pt, sel_: (seq(i, t), 0, 0, 0))
    nc = len(raws)
    return _Rider(
        prefetch=(page_table, sel),
        in_specs=[cols_spec] + [raw_spec] * nc + tile_specs,
        out_specs=[pl.BlockSpec((None, N_HEADS_A, HEAD_DIM_A, 1), lambda i, t, pt, sel_: (seq(i, t), 0, 0, 0))],
        out_shape=[jax.ShapeDtypeStruct((db, N_HEADS_A, HEAD_DIM_A, 1), F32)],
        operands=(cols,) + tuple(raws) + (vt_pool,) * len(tile_specs),
        body=lambda pf, ins, outs, s: _moba_attend(pf[1], ins[0], ins[1:1 + nc], ins[1 + nc:], outs[0], s,
                                                   n_pages * PAGE_SIZE))


def _dot_nt(a, b):
    return lax.dot_general(a, b, (((1,), (1,)), ((), ())), preferred_element_type=F32)


def _diff_decode_kernel(pt_ref, q_ref, kn_ref, vn_ref, slope_ref, tbias_ref, lam_ref, subln_ref,
                        *rest, past_len, lam_init):
    del pt_ref
    npg = PAGES_PER_STEP
    k_refs, v_refs = rest[:npg], rest[npg:2 * npg]
    o_ref, m_sc, l_sc, acc_sc = rest[2 * npg:]
    j = pl.program_id(1)
    q8 = q_ref[...]
    slope8 = slope_ref[...]
    tbias = tbias_ref[...]
    lane_max = lambda x: jnp.max(x, axis=-1, keepdims=True)

    @pl.when(j == 0)
    def _():
        s_self = jnp.sum(_bf16_round(q8) * _bf16_round(kn_ref[...]), axis=-1, keepdims=True)
        m_sc[...] = jnp.broadcast_to(s_self, m_sc.shape)
        l_sc[...] = jnp.ones(l_sc.shape, F32)
        acc_sc[...] = vn_ref[...]

    q8b = q8.astype(BF16)
    scores = []
    for idx in range(npg):
        dist0 = (past_len - (j * npg + idx) * PAGE_SIZE).astype(F32)
        scores.append(_dot_nt(q8b, k_refs[idx][...].astype(BF16)) + tbias - slope8 * dist0)
    m_old = m_sc[...]
    m_step = lane_max(functools.reduce(jnp.maximum, scores))
    m_new = jnp.maximum(m_old, m_step)
    alpha = jnp.exp(m_old - m_new)
    m1 = m_new[:, 0:1]
    p_sum = jnp.zeros(scores[0].shape, F32)
    pv = jnp.zeros(acc_sc.shape, F32)
    for idx in range(npg):
        p = jnp.exp(scores[idx] - m1)
        p_sum = p_sum + p
        pv = pv + _dot(p.astype(BF16), v_refs[idx][...].astype(BF16))
    l_new = alpha * l_sc[...] + jnp.sum(p_sum, axis=-1, keepdims=True)
    acc_new = alpha * acc_sc[...] + pv
    m_sc[...] = m_new
    l_sc[...] = l_new
    acc_sc[...] = acc_new

    @pl.when(j == pl.num_programs(1) - 1)
    def _():
        o_c = acc_new * (1.0 / l_new)
        o = o_c - _lambda(lam_ref, lam_init) * pltpu.roll(o_c, N_HEADS_B, axis=0)
        ms = jnp.mean(o * o, axis=-1, keepdims=True)
        o_ref[...] = o * lax.rsqrt(ms + RMS_EPS) * subln_ref[...] * (1.0 - lam_init)


def _diff_decode(q, k_new, v_new, k_pool, v_pool, page_table, layer, lam_vecs, subln, lam_init):
    db, n_pages = page_table.shape
    depth, n_phys = k_pool.shape[:2]
    rows = PAGE_SIZE * N_HEADS_B
    pages = lambda pool: pool.reshape(depth * n_phys, rows, V_DIM_B)
    twice = lambda a: jnp.concatenate([a, a], axis=1)
    branch = (np.arange(V_DIM_B) // HEAD_DIM_B)[None, :] == np.arange(2)[:, None]
    q8 = jnp.concatenate([q * branch[0].astype(np.float32), q * branch[1].astype(np.float32)], axis=1)
    slopes = np.tile(np.asarray(_slopes(N_HEADS_B), np.float32), 2)
    col = np.arange(rows)
    own = (col % N_HEADS_B)[None, :] == (np.arange(SUBLANES) % N_HEADS_B)[:, None]
    tbias_np = np.where(own, slopes[:, None] * (col // N_HEADS_B)[None, :], NEG).astype(np.float32)
    seq = pl.BlockSpec((None, SUBLANES, V_DIM_B), lambda s, j, pt: (s, 0, 0))
    cst = lambda shp: pl.BlockSpec(shp, lambda s, j, pt: (0, 0))
    page = lambda i: pl.BlockSpec(
        (None, rows, V_DIM_B), lambda s, j, pt: (layer * n_phys + pt[s, j * PAGES_PER_STEP + i], 0, 0))
    page_specs = [page(i) for i in range(PAGES_PER_STEP)]
    stat = pltpu.VMEM((SUBLANES, LANES), F32)
    grid_spec = pltpu.PrefetchScalarGridSpec(
        num_scalar_prefetch=1,
        grid=(db, n_pages // PAGES_PER_STEP),
        in_specs=[seq, seq, seq, cst((SUBLANES, 1)), cst((SUBLANES, rows)), cst((4, HEAD_DIM_B)), cst((1, V_DIM_B))]
        + page_specs * 2,
        out_specs=seq,
        scratch_shapes=[stat, stat, stat],
    )
    o = pl.pallas_call(
        functools.partial(_diff_decode_kernel, past_len=n_pages * PAGE_SIZE, lam_init=lam_init),
        grid_spec=grid_spec,
        out_shape=jax.ShapeDtypeStruct((db, SUBLANES, V_DIM_B), F32),
        compiler_params=_params("arbitrary", "arbitrary"),
        name="diff_decode",
    )(page_table, q8, twice(k_new), twice(v_new), jnp.asarray(slopes[:, None]), jnp.asarray(tbias_np),
      lam_vecs, subln, *([pages(k_pool)] * PAGES_PER_STEP), *([pages(v_pool)] * PAGES_PER_STEP))
    return o[:, :N_HEADS_B]


def _block_diag_ones():
    i = np.arange(MXU_DIM) // GROUP
    return jnp.asarray((i[:, None] == i[None, :]).astype(np.float32), BF16)


def kernel(x_prompt, x_sample, cache_k_moba, cache_v_moba, cache_k_diff, cache_v_diff, page_table, c_prompt, c_sample, w_ada, b_ada, norm_ffn1, ffn1_w_gate, ffn1_w_up, ffn1_w_down, norm_mix, w_in, qn_moba, kn_moba, qn_diff, kn_diff, lambda_q1, lambda_k1, lambda_q2, lambda_k2, subln_diff, w_branch_moba, w_branch_diff, w_out, norm_ffn2, ffn2_w_gate, ffn2_w_up, ffn2_w_down):
    depth = w_ada.shape[0]
    b, s, d = x_prompt.shape
    db, t_new, _ = x_sample.shape
    assert t_new == 1 and s % TOKEN_TILE == 0 and db % SUBLANES == 0
    n_pages = page_table.shape[1]
    assert n_pages % PAGES_PER_STEP == 0
    bd = _block_diag_ones()
    tile8 = lambda v: jnp.tile(v, WIDTH // v.shape[0]).reshape(1, WIDTH)

    y_p, y_s = x_prompt, x_sample.reshape(1, db, d)
    rows_p, rows_s = [], []
    for l in range(depth):
        lam_init = _lambda_init(l)
        bf = lambda w: w[l].astype(BF16)
        row = lambda v: v[l].reshape(1, -1)
        mod = _ada(jnp.concatenate([c_prompt, c_sample], axis=0), w_ada[l], b_ada[l])
        mod_p = mod[:b].reshape(b, N_MOD, 1, d)
        mod_s = mod[b:].reshape(db, N_MOD, d).transpose(1, 0, 2).reshape(1, N_MOD, db, d)
        ffn1 = (row(norm_ffn1), bf(ffn1_w_gate), bf(ffn1_w_up), bf(ffn1_w_down))
        ffn2 = (row(norm_ffn2), bf(ffn2_w_gate), bf(ffn2_w_up), bf(ffn2_w_down))
        w_in_bf = bf(w_in)
        gains = jnp.concatenate([tile8(qn_moba[l]), tile8(kn_moba[l]), tile8(qn_diff[l]), tile8(kn_diff[l])], 0)
        lam_vecs = jnp.stack([lambda_q1[l], lambda_k1[l], lambda_q2[l], lambda_k2[l]])
        mix_w = (bf(w_branch_moba), bf(w_branch_diff), bf(w_out))

        (x1s,) = _ffn(y_s, mod_s, *ffn1, k0=0, tm=db)
        (q_as, k_as, v_as, q_bs, k_bs, v_bs, g_as, g_bs) = _proj(
            x1s, mod_s, row(norm_mix), w_in_bf, gains, bd, db, False)
        heads_a = lambda a: a.reshape(db, N_HEADS_A, HEAD_DIM_A)
        heads_b = lambda a: a.reshape(db, N_HEADS_B, V_DIM_B)
        cols = jnp.stack([heads_a(q_as), heads_a(k_as), heads_a(v_as)], axis=1)[..., None]

        steps = s // TOKEN_TILE
        kt_pool, vt_pool = _moba_pool_tiles(cache_k_moba), _moba_pool_tiles(cache_v_moba)
        gate_rider = lambda first: _gate_rider(page_table, cols, kt_pool, l, first, steps)
        x1, gate0, raw0 = _ffn(y_p, mod_p, *ffn1, k0=0, tm=TOKEN_TILE, rider=gate_rider(0))
        (kt_a, vt32_a, k_b, v_b, ka_a, ka_b, qt_a, qt_b, vt_a, vt_b, km_a, g_a, g_b, gate1, raw1) = _proj(
            x1, mod_p, row(norm_mix), w_in_bf, gains, bd, TOKEN_TILE, True, rider=gate_rider(GATE_PAGES_PER_STEP))
        o_a, o_b = _prompt_attention(qt_a, ka_a, vt_a, km_a.reshape(b, -1, WIDTH), qt_b, ka_b, vt_b,
                                     lam_vecs, tile8(subln_diff[l]), lam_init)
        attend = _attend_rider(page_table, _moba_select([gate0, gate1]), cols, [raw0, raw1], vt_pool, l, steps)
        y_p, o_as = _mix_ffn(x1, o_a, o_b, g_a, g_b, mod_p, *mix_w, *ffn2, tm=TOKEN_TILE, rider=attend)
        token_major = lambda a: a.reshape(b, N_HEADS_A, HEAD_DIM_A, s).transpose(0, 3, 1, 2)
        rows_p.append((token_major(kt_a), token_major(vt32_a), k_b, v_b))

        o_bs = _diff_decode(heads_b(q_bs), heads_b(k_bs), heads_b(v_bs), cache_k_diff, cache_v_diff,
                            page_table, l, lam_vecs, subln_diff[l].reshape(1, V_DIM_B), lam_init)
        as_rows = lambda a: a.reshape(1, db, WIDTH).astype(BF16)
        (y_s,) = _mix_ffn(x1s, as_rows(o_as), as_rows(o_bs), g_as, g_bs, mod_s, *mix_w, *ffn2, tm=db)
        rows_s.append((k_as, v_as, k_bs, v_bs))

    def stack(rows, i, lead, heads, hd):
        return jnp.stack([r[i].reshape(lead + (heads, hd)) for r in rows])

    lp, ls = (b, s), (db, 1)
    return (y_p, y_s.reshape(db, 1, d),
            stack(rows_p, 0, lp, N_HEADS_A, HEAD_DIM_A), stack(rows_p, 1, lp, N_HEADS_A, HEAD_DIM_A),
            stack(rows_p, 2, lp, N_HEADS_B, V_DIM_B), stack(rows_p, 3, lp, N_HEADS_B, V_DIM_B),
            stack(rows_s, 0, ls, N_HEADS_A, HEAD_DIM_A), stack(rows_s, 1, ls, N_HEADS_A, HEAD_DIM_A),
            stack(rows_s, 2, ls, N_HEADS_B, V_DIM_B), stack(rows_s, 3, ls, N_HEADS_B, V_DIM_B))
```

```python
import functools
import math
from typing import Callable, NamedTuple

import jax
import jax.numpy as jnp
import numpy as np
from jax import lax
from jax.experimental import pallas as pl
from jax.experimental.pallas import tpu as pltpu

F32 = jnp.float32
BF16 = jnp.bfloat16

N_HEADS_A = 8
HEAD_DIM_A = 64
MOBA_BLOCK = 256
MOBA_TOPK = 3
N_HEADS_B = 4
HEAD_DIM_B = 64
V_DIM_B = 2 * HEAD_DIM_B
WIDTH = 512
GROUP = 64
N_GROUPS = WIDTH // GROUP
PAGE_SIZE = 128
PAGES_PER_BLOCK = MOBA_BLOCK // PAGE_SIZE
N_MOD = 9
RMS_EPS = 1e-6
QK_SCALE = GROUP ** -0.5
NEG = -1e30

LANES = 128
SUBLANES = 8
BF16_SUBLANES = 16
MXU_DIM = 256
VMEM_LIMIT_BYTES = 58 * 1024 * 1024

TOKEN_TILE = 512
MAX_FF_TILE = 1408
ADA_STEPS = 4
CAST_STEPS = 8
PAGES_PER_STEP = 32
GATE_PAGES_PER_STEP = 32


def _slopes(n_heads):
    return [2.0 ** (-8.0 * (i + 1) / n_heads) for i in range(n_heads)]


def _lambda_init(layer):
    return 0.8 - 0.6 * math.exp(-0.3 * layer)


def _dot(a, b):
    return jnp.dot(a, b, preferred_element_type=F32)


def _bf16_round(x):
    return x.astype(BF16).astype(F32)


def _rms(x, w):
    ms = jnp.mean(x * x, axis=-1, keepdims=True)
    return x * lax.rsqrt(ms + RMS_EPS) * w


def _pick_ff_tile(d_ff, rows):
    cap = MXU_DIM if rows >= MXU_DIM else MAX_FF_TILE
    best = LANES
    for t in range(LANES, min(d_ff, cap) + 1, LANES):
        if d_ff % t == 0:
            best = t
    return best


def _params(*sem):
    return pltpu.CompilerParams(dimension_semantics=sem, vmem_limit_bytes=VMEM_LIMIT_BYTES)


def _const_spec(shape):
    nd = len(shape)
    return pl.BlockSpec(shape, lambda *_: (0,) * nd)


def _cast_kernel(*refs):
    n = len(refs) // 2
    for w_ref, o_ref in zip(refs[:n], refs[n:]):
        o_ref[...] = w_ref[...].astype(BF16)


def _to_bf16(ws):
    specs = [pl.BlockSpec((w.shape[0] // CAST_STEPS, w.shape[1]), lambda i: (i, 0)) for w in ws]
    assert all(w.shape[0] % (CAST_STEPS * BF16_SUBLANES) == 0 for w in ws)
    return pl.pallas_call(
        _cast_kernel,
        grid=(CAST_STEPS,),
        in_specs=specs,
        out_specs=specs,
        out_shape=[jax.ShapeDtypeStruct(w.shape, BF16) for w in ws],
        compiler_params=_params("arbitrary"),
        name="to_bf16",
    )(*ws)


def _ada_kernel(c_ref, w_ref, b_ref, o_ref):
    c = c_ref[...]
    s = c * jax.nn.sigmoid(c)
    o_ref[...] = _dot(s.astype(BF16), w_ref[...].astype(BF16)) + b_ref[...]


def _ada(c, w, b):
    m, d = c.shape
    n = w.shape[1]
    tn = n // ADA_STEPS if n % (ADA_STEPS * LANES) == 0 else n
    return pl.pallas_call(
        _ada_kernel,
        grid=(n // tn,),
        in_specs=[pl.BlockSpec((m, d), lambda j: (0, 0)),
                  pl.BlockSpec((d, tn), lambda j: (0, j)),
                  pl.BlockSpec((1, tn), lambda j: (0, j))],
        out_specs=pl.BlockSpec((m, tn), lambda j: (0, j)),
        out_shape=jax.ShapeDtypeStruct((m, n), F32),
        compiler_params=_params("arbitrary"),
        name="ada_mod",
    )(c, w, b.reshape(1, n))


def _ffn_update(x, shift, scale, gate, nw, wg_ref, wu_ref, wd_ref, tf):
    h = (_rms(x, nw) * (1.0 + scale) + shift).astype(BF16)
    d_ff = wg_ref.shape[1]
    acc = jnp.zeros(x.shape, F32)
    for j in range(d_ff // tf):
        g = _dot(h, wg_ref[:, j * tf:(j + 1) * tf])
        u = _dot(h, wu_ref[:, j * tf:(j + 1) * tf])
        a = (g * jax.nn.sigmoid(g) * u).astype(BF16)
        acc = acc + _dot(a, wd_ref[j * tf:(j + 1) * tf, :])
    return x + 0.5 * gate * acc


class _Rider(NamedTuple):
    prefetch: tuple
    in_specs: list
    out_specs: list
    out_shape: list
    operands: tuple
    body: Callable


def _gate_pages(q_ref, k_refs, gate_ref, s_ref):
    q = q_ref[...]
    q_bf = _bf16_round(q)
    for bi in range(len(k_refs) // PAGES_PER_BLOCK):
        pages = [k_refs[bi * PAGES_PER_BLOCK + i][...] for i in range(PAGES_PER_BLOCK)]
        for i, k_page in enumerate(pages):
            s_ref[bi * PAGES_PER_BLOCK + i] = jnp.sum(k_page * q, axis=1)
        ksum = jnp.sum(functools.reduce(jnp.add, pages), axis=-1, keepdims=True)
        gate_ref[bi] = jnp.sum(_bf16_round(ksum * (1.0 / MOBA_BLOCK)) * q_bf, axis=1)


def _gate_rider(page_table, cols, kt_pool, layer, first_page, steps_per_batch):
    db = page_table.shape[0]
    npg = GATE_PAGES_PER_STEP
    seq = lambda i, t: i * steps_per_batch + t
    page = lambda k: pl.BlockSpec(
        (None, None, N_HEADS_A, HEAD_DIM_A, PAGE_SIZE),
        lambda i, t, pt: (layer, pt[seq(i, t), first_page + k], 0, 0, 0))
    per_seq = lambda shp: pl.BlockSpec((None,) + shp, lambda i, t, pt: (seq(i, t), 0, 0, 0))
    q_spec = pl.BlockSpec((None, None, N_HEADS_A, HEAD_DIM_A, 1), lambda i, t, pt: (seq(i, t), 0, 0, 0, 0))
    return _Rider(
        prefetch=(page_table,),
        in_specs=[q_spec] + [page(k) for k in range(npg)],
        out_specs=[per_seq((npg // PAGES_PER_BLOCK, N_HEADS_A, 1)), per_seq((npg, N_HEADS_A, PAGE_SIZE))],
        out_shape=[jax.ShapeDtypeStruct((db, npg // PAGES_PER_BLOCK, N_HEADS_A, 1), F32),
                   jax.ShapeDtypeStruct((db, npg, N_HEADS_A, PAGE_SIZE), F32)],
        operands=(cols,) + (kt_pool,) * npg,
        body=lambda pf, ins, outs, s: _gate_pages(ins[0], ins[1:], *outs))


def _dense_call(body, name, grid, in_specs, out_specs, out_shape, operands, rider=None):
    n_in, n_out = len(in_specs), len(out_specs)
    if rider is None:
        def kernel_fn(*refs):
            body(refs[:n_in], refs[n_in:])
        return pl.pallas_call(kernel_fn, grid=grid, in_specs=in_specs, out_specs=out_specs, out_shape=out_shape,
                              compiler_params=_params("arbitrary", "arbitrary"), name=name)(*operands)
    n_pf = len(rider.prefetch)
    assert rider.out_shape[0].shape[0] == grid[0] * grid[1]

    def kernel_fn(*refs):
        pf, refs = refs[:n_pf], refs[n_pf:]
        n_all_in = n_in + len(rider.in_specs)
        ins, outs = refs[:n_all_in], refs[n_all_in:]
        body(ins[:n_in], outs[:n_out])
        rider.body(pf, ins[n_in:], outs[n_out:], pl.program_id(0) * grid[1] + pl.program_id(1))

    return pl.pallas_call(
        kernel_fn,
        grid_spec=pltpu.PrefetchScalarGridSpec(
            num_scalar_prefetch=n_pf, grid=grid, in_specs=list(in_specs) + rider.in_specs,
            out_specs=list(out_specs) + rider.out_specs),
        out_shape=list(out_shape) + rider.out_shape,
        compiler_params=_params("arbitrary", "arbitrary"),
        name=name,
    )(*rider.prefetch, *operands, *rider.operands)


def _ffn(x, mod, nw, wg, wu, wd, k0, tm, rider=None):
    b, s, d = x.shape
    r = mod.shape[2]
    d_ff = wg.shape[1]
    tf = _pick_ff_tile(d_ff, tm)

    def body(ins, outs):
        x_ref, mod_ref, nw_ref, wg_ref, wu_ref, wd_ref = ins
        outs[0][...] = _ffn_update(x_ref[...], mod_ref[k0], mod_ref[k0 + 1], mod_ref[k0 + 2],
                                   nw_ref[...], wg_ref, wu_ref, wd_ref, tf)

    return _dense_call(
        body, "ffn", (b, s // tm),
        in_specs=[pl.BlockSpec((None, tm, d), lambda i, t, *_: (i, t, 0)),
                  pl.BlockSpec((None, N_MOD, r, d), lambda i, t, *_: (i, 0, 0, 0)),
                  _const_spec((1, d)),
                  _const_spec((d, d_ff)), _const_spec((d, d_ff)), _const_spec((d_ff, d))],
        out_specs=[pl.BlockSpec((None, tm, d), lambda i, t, *_: (i, t, 0))],
        out_shape=[jax.ShapeDtypeStruct((b, s, d), F32)],
        operands=(x, mod, nw, wg, wu, wd), rider=rider)


def _head_norm(seg, gain_row, bd):
    sq = (seg * seg).astype(BF16)
    parts = [_dot(sq[:, c * MXU_DIM:(c + 1) * MXU_DIM], bd) for c in range(WIDTH // MXU_DIM)]
    ms = jnp.concatenate(parts, axis=1) * (1.0 / GROUP)
    return seg * lax.rsqrt(ms + RMS_EPS) * gain_row


def _proj_kernel(x_ref, mod_ref, nw_ref, win_ref, gains_ref, bd_ref, *outs, transposed):
    x = x_ref[...]
    d = x.shape[1]
    h = (_rms(x, nw_ref[...]) * (1.0 + mod_ref[4]) + mod_ref[3]).astype(BF16)
    bd = bd_ref[...]

    def seg(j):
        return _dot(h, win_ref[:, j * WIDTH:(j + 1) * WIDTH])

    q_a = _head_norm(seg(0), gains_ref[0:1, :], bd) * QK_SCALE
    k_a = _head_norm(seg(1), gains_ref[1:2, :], bd)
    v_a = seg(2)
    q_b = _head_norm(seg(3), gains_ref[2:3, :], bd) * QK_SCALE
    k_b = _head_norm(seg(4), gains_ref[3:4, :], bd)
    v_b = seg(5)
    g0 = 6 * WIDTH
    g_a = jax.nn.sigmoid(_dot(h, win_ref[:, g0:g0 + d]))
    g_b = jax.nn.sigmoid(_dot(h, win_ref[:, g0 + d:g0 + 2 * d]))

    if not transposed:
        (qa_ref, ka_ref, va_ref, qb_ref, kb_ref, vb_ref, ga_ref, gb_ref) = outs
        qa_ref[...] = q_a
        ka_ref[...] = k_a
        va_ref[...] = v_a
        qb_ref[...] = q_b
        kb_ref[...] = k_b
        vb_ref[...] = v_b
    else:
        (kta_ref, vta32_ref, kb_ref, vb_ref, kaa_ref, kab_ref, qta_ref, qtb_ref,
         vta_ref, vtb_ref, km_ref, ga_ref, gb_ref) = outs
        kta_ref[...] = k_a.T
        vt_a = v_a.T
        vta32_ref[...] = vt_a
        qt_a, qt_b, vt_b = q_a.T, q_b.T, v_b.T
        for r in range(x.shape[0] // MOBA_BLOCK):
            rows = slice(r * MOBA_BLOCK, (r + 1) * MOBA_BLOCK)
            n = pl.program_id(1) * (x.shape[0] // MOBA_BLOCK) + r
            for g, (ka_g, kb_g) in enumerate(zip(_augmented_keys(k_a[rows], n), _augmented_keys(k_b[rows], n))):
                kaa_ref[r, g] = ka_g
                kab_ref[r, g] = kb_g
            qta_ref[r] = qt_a[:, rows].astype(BF16)
            qtb_ref[r] = qt_b[:, rows].astype(BF16)
            vta_ref[r] = vt_a[:, rows].astype(BF16)
            vtb_ref[r] = vt_b[:, rows].astype(BF16)
            km_ref[r] = jnp.sum(k_a[rows], axis=0, keepdims=True) * (1.0 / MOBA_BLOCK)
        for hd in range(N_HEADS_B):
            lanes = slice(V_DIM_B * hd, V_DIM_B * (hd + 1))
            kb_ref[pl.ds(hd, x.shape[0], stride=N_HEADS_B), :] = k_b[:, lanes]
            vb_ref[pl.ds(hd, x.shape[0], stride=N_HEADS_B), :] = v_b[:, lanes]
    ga_ref[...] = g_a.astype(BF16)
    gb_ref[...] = g_b.astype(BF16)


def _proj(x, mod, nw, w_in, gains, bd, tm, transposed, rider=None):
    b, s, d = x.shape
    r = mod.shape[2]
    d_in = w_in.shape[1]
    row_spec = lambda w: pl.BlockSpec((None, tm, w), lambda i, t, *_: (i, t, 0))
    rows = lambda w, dt=F32: jax.ShapeDtypeStruct((b, s, w), dt)
    if transposed:
        nb, bpt = s // MOBA_BLOCK, tm // MOBA_BLOCK
        blk = lambda shp: pl.BlockSpec((None, bpt) + shp, lambda i, t, *_: (i, t, 0, 0))
        col_spec = pl.BlockSpec((None, WIDTH, tm), lambda i, t, *_: (i, 0, t))
        head_rows = pl.BlockSpec((None, tm * N_HEADS_B, V_DIM_B), lambda i, t, *_: (i, t, 0))
        k_aug = pl.BlockSpec((None, bpt, N_GROUPS, MOBA_BLOCK, LANES), lambda i, t, *_: (i, t, 0, 0, 0))
        out_specs = [col_spec] * 2 + [head_rows] * 2 + [k_aug] * 2 \
            + [blk((WIDTH, MOBA_BLOCK))] * 4 + [blk((1, WIDTH))] + [row_spec(d)] * 2
        out_shape = [jax.ShapeDtypeStruct((b, WIDTH, s), F32)] * 2 \
            + [jax.ShapeDtypeStruct((b, s * N_HEADS_B, V_DIM_B), F32)] * 2 \
            + [jax.ShapeDtypeStruct((b, nb, N_GROUPS, MOBA_BLOCK, LANES), BF16)] * 2 \
            + [jax.ShapeDtypeStruct((b, nb, WIDTH, MOBA_BLOCK), BF16)] * 4 \
            + [jax.ShapeDtypeStruct((b, nb, 1, WIDTH), F32)] + [rows(d, BF16)] * 2
    else:
        out_specs = [row_spec(WIDTH)] * 6 + [row_spec(d)] * 2
        out_shape = [rows(WIDTH)] * 6 + [rows(d, BF16)] * 2
    return _dense_call(
        lambda ins, outs: _proj_kernel(*ins, *outs, transposed=transposed), "mixer_proj", (b, s // tm),
        in_specs=[pl.BlockSpec((None, tm, d), lambda i, t, *_: (i, t, 0)),
                  pl.BlockSpec((None, N_MOD, r, d), lambda i, t, *_: (i, 0, 0, 0)),
                  _const_spec((1, d)), _const_spec((d, d_in)),
                  _const_spec((4, WIDTH)), _const_spec((MXU_DIM, MXU_DIM))],
        out_specs=out_specs, out_shape=out_shape, operands=(x, mod, nw, w_in, gains, bd), rider=rider)


FEATURE_ROWS = SUBLANES
MAX_KEY_BLOCKS = GROUP - FEATURE_ROWS


def _augmented_keys(k, n):
    lane = lax.broadcasted_iota(jnp.int32, (MOBA_BLOCK, LANES), 1)
    key = lax.broadcasted_iota(jnp.int32, (MOBA_BLOCK, LANES), 0).astype(F32)
    feat = jnp.where(lane == GROUP, 1.0, jnp.where(lane == GROUP + 1, key,
                                                    jnp.where(lane == GROUP + FEATURE_ROWS + n, 1.0, 0.0)))
    out = []
    for g in range(N_GROUPS):
        p, half = divmod(g, 2)
        pair = k[:, LANES * p:LANES * (p + 1)]
        k_g = pair if half == 0 else pltpu.roll(pair, GROUP, axis=1)
        out.append(jnp.where(lane < GROUP, k_g, feat).astype(BF16))
    return out


def _augmented_queries(qt, slopes, block_biases):
    qry = lax.broadcasted_iota(jnp.int32, (FEATURE_ROWS, MOBA_BLOCK), 1).astype(F32)
    row = lax.broadcasted_iota(jnp.int32, (FEATURE_ROWS, MOBA_BLOCK), 0)
    out = []
    for g, (slope, bias) in enumerate(zip(slopes, block_biases)):
        head = jnp.where(row == 0, -slope * qry, jnp.where(row == 1, slope, 0.0))
        pad = jnp.zeros((GROUP - FEATURE_ROWS - bias.shape[0], MOBA_BLOCK), F32)
        feat = jnp.concatenate([head, bias, pad], axis=0).astype(BF16)
        out.append(jnp.concatenate([qt[GROUP * g:GROUP * (g + 1), :], feat], axis=0))
    return out


def _pair_scores(k_aug, q_aug, p):
    return [_dot(k_aug[2 * p + half], q_aug[2 * p + half]) for half in range(2)]


def _attend_block(k_aug, vtb, q_aug, causal, m_all, l_all, acc_sc, v_rows, st_first, s0_sc, k_next):
    m_out, l_out = [], []
    n_pairs = N_GROUPS // 2
    ones_rows = jnp.ones((BF16_SUBLANES, MOBA_BLOCK), BF16)
    st_next = _pair_scores(k_aug, q_aug, 0) if st_first is None else st_first
    for p in range(n_pairs):
        st_pair = st_next
        if p + 1 < n_pairs:
            st_next = _pair_scores(k_aug, q_aug, p + 1)
        else:
            nxt = _pair_scores(k_next, q_aug, 0)
            s0_sc[0] = nxt[0]
            s0_sc[1] = nxt[1]
        for half in range(2):
            g = 2 * p + half
            st = st_pair[half]
            if causal is not None:
                st = jnp.where(causal, st, NEG)
            m_old = m_all[g:g + 1, :]
            m_new = jnp.maximum(m_old, jnp.max(st, axis=0, keepdims=True))
            alpha = jnp.exp(m_old - m_new)
            pt = jnp.exp(st - m_new).astype(BF16)
            rows = v_rows(g)
            nr = rows.stop - rows.start
            acc_rows = slice(g * nr, (g + 1) * nr)
            pv = _dot(jnp.concatenate([vtb[rows, :], ones_rows], axis=0), pt)
            l_out.append(alpha * l_all[g:g + 1, :] + pv[nr:nr + 1, :])
            acc_sc[acc_rows, :] = alpha * acc_sc[acc_rows, :] + pv[:nr, :]
            m_out.append(m_new)
    return jnp.concatenate(m_out, axis=0), jnp.concatenate(l_out, axis=0)


def _attend_all_blocks(qi, k_ref, vt_ref, q_aug, acc_sc, s0_sc, v_rows):
    key_i = lax.broadcasted_iota(jnp.int32, (MOBA_BLOCK, MOBA_BLOCK), 0)
    qry_i = lax.broadcasted_iota(jnp.int32, (MOBA_BLOCK, MOBA_BLOCK), 1)
    acc_sc[...] = jnp.zeros(acc_sc.shape, F32)
    m0 = jnp.full((N_GROUPS, MOBA_BLOCK), NEG, F32)
    l0 = jnp.zeros((N_GROUPS, MOBA_BLOCK), F32)
    m1, l1 = _attend_block(k_ref.at[qi], vt_ref[qi], q_aug, key_i <= qry_i, m0, l0, acc_sc, v_rows,
                           None, s0_sc, k_ref.at[0])

    def past(n, carry):
        return _attend_block(k_ref.at[n], vt_ref[n], q_aug, None, *carry, acc_sc, v_rows,
                             [s0_sc[0], s0_sc[1]], s0_sc, k_ref.at[n + 1])

    _, l_fin = lax.fori_loop(0, qi, past, (m1, l1))
    return l_fin


def _block_distance(qi, nbp):
    blk_i = lax.broadcasted_iota(jnp.int32, (nbp, MOBA_BLOCK), 0)
    return blk_i, ((qi - blk_i) * MOBA_BLOCK).astype(F32)


def _moba_prompt_kernel(qt_ref, k_ref, vt_ref, km_ref, o_ref, acc_sc, s0_sc):
    qi = pl.program_id(1)
    nb = k_ref.shape[0]
    nbp = -(-nb // SUBLANES) * SUBLANES
    blk = MOBA_BLOCK
    slopes = _slopes(N_HEADS_A)
    qt = qt_ref[...]

    km = km_ref[...].astype(BF16)
    blk_i, blk_dist = _block_distance(qi, nbp)
    valid = blk_i[:nb] < qi
    biases = []
    for g in range(N_GROUPS):
        gate = _dot(km[:, GROUP * g:GROUP * (g + 1)], qt[GROUP * g:GROUP * (g + 1), :])
        gate = jnp.where(valid, gate, NEG)
        rank = jnp.zeros((nb, blk), jnp.int32)
        for m in range(nb):
            gm = gate[m:m + 1, :]
            beats = (gm > gate) | ((gm == gate) & (m < blk_i[:nb]))
            rank = rank + beats.astype(jnp.int32)
        drop = jnp.where(valid & (rank < MOBA_TOPK), 0.0, NEG)
        if nbp > nb:
            drop = jnp.concatenate([drop, jnp.zeros((nbp - nb, blk), F32)], axis=0)
        biases.append(jnp.where(blk_i == qi, 0.0, drop - slopes[g] * blk_dist))

    q_aug = _augmented_queries(qt, slopes, biases)
    l_fin = _attend_all_blocks(qi, k_ref, vt_ref, q_aug, acc_sc, s0_sc,
                               lambda g: slice(GROUP * g, GROUP * (g + 1)))
    parts = [acc_sc[GROUP * g:GROUP * (g + 1), :] * (1.0 / l_fin[g:g + 1, :]) for g in range(N_GROUPS)]
    o_ref[...] = jnp.concatenate(parts, axis=0).T.astype(BF16)


def _lambda(lam_ref, lam_init):
    a = jnp.sum(lam_ref[0:1, :] * lam_ref[1:2, :], axis=-1, keepdims=True)
    b = jnp.sum(lam_ref[2:3, :] * lam_ref[3:4, :], axis=-1, keepdims=True)
    return jnp.exp(a) - jnp.exp(b) + lam_init


def _diff_prompt_kernel(qt_ref, k_ref, vt_ref, lam_ref, subln_ref, o_ref, acc_sc, s0_sc, *, lam_init):
    qi = pl.program_id(1)
    nb = k_ref.shape[0]
    nbp = -(-nb // SUBLANES) * SUBLANES
    slopes = [s for s in _slopes(N_HEADS_B) for _ in range(2)]
    _, blk_dist = _block_distance(qi, nbp)
    q_aug = _augmented_queries(qt_ref[...], slopes, [-s * blk_dist for s in slopes])
    v_rows = lambda g: slice(V_DIM_B * (g // 2), V_DIM_B * (g // 2 + 1))
    l_fin = _attend_all_blocks(qi, k_ref, vt_ref, q_aug, acc_sc, s0_sc, v_rows)

    lam = _lambda(lam_ref, lam_init)
    parts = []
    for h in range(N_HEADS_B):
        o0 = acc_sc[V_DIM_B * (2 * h):V_DIM_B * (2 * h + 1), :] * (1.0 / l_fin[2 * h:2 * h + 1, :])
        o1 = acc_sc[V_DIM_B * (2 * h + 1):V_DIM_B * (2 * h + 2), :] * (1.0 / l_fin[2 * h + 1:2 * h + 2, :])
        o = o0 - lam * o1
        ms = jnp.mean(o * o, axis=0, keepdims=True)
        parts.append(o * lax.rsqrt(ms + RMS_EPS))
    o_t = jnp.concatenate(parts, axis=0).T
    o_ref[...] = (o_t * subln_ref[...] * (1.0 - lam_init)).astype(BF16)


def _prompt_attention(qt_a, k_aug_a, vt_a, km_a, qt_b, k_aug_b, vt_b, lam_vecs, subln_row, lam_init):
    b, nb = qt_a.shape[:2]
    assert nb <= MAX_KEY_BLOCKS
    blk = MOBA_BLOCK
    s = nb * blk
    q_spec = pl.BlockSpec((None, None, WIDTH, blk), lambda i, t: (i, t, 0, 0))
    k_spec = pl.BlockSpec((None, nb, N_GROUPS, blk, LANES), lambda i, t: (i, 0, 0, 0, 0))
    vt_spec = pl.BlockSpec((None, nb, WIDTH, blk), lambda i, t: (i, 0, 0, 0))
    o_spec = pl.BlockSpec((None, blk, WIDTH), lambda i, t: (i, t, 0))
    o_shape = jax.ShapeDtypeStruct((b, s, WIDTH), BF16)
    s0 = pltpu.VMEM((2, blk, blk), F32)
    o_a = pl.pallas_call(
        _moba_prompt_kernel,
        grid=(b, nb),
        in_specs=[q_spec, k_spec, vt_spec, pl.BlockSpec((None, nb, WIDTH), lambda i, t: (i, 0, 0))],
        out_specs=o_spec,
        out_shape=o_shape,
        scratch_shapes=[pltpu.VMEM((WIDTH, blk), F32), s0],
        compiler_params=_params("arbitrary", "arbitrary"),
        name="moba_prompt",
    )(qt_a, k_aug_a, vt_a, km_a)
    o_b = pl.pallas_call(
        functools.partial(_diff_prompt_kernel, lam_init=lam_init),
        grid=(b, nb),
        in_specs=[q_spec, k_spec, vt_spec, _const_spec((4, HEAD_DIM_B)), _const_spec((1, WIDTH))],
        out_specs=o_spec,
        out_shape=o_shape,
        scratch_shapes=[pltpu.VMEM((N_GROUPS * V_DIM_B, blk), F32), s0],
        compiler_params=_params("arbitrary", "arbitrary"),
        name="diff_prompt",
    )(qt_b, k_aug_b, vt_b, lam_vecs, subln_row)
    return o_a, o_b


def _mix_ffn_kernel(x_ref, oa_ref, ob_ref, ga_ref, gb_ref, mod_ref, wba_ref, wbd_ref, wout_ref,
                    nw_ref, wg_ref, wu_ref, wd_ref, o_ref, *, tf):
    y_a = _dot(oa_ref[...], wba_ref[...])
    y_b = _dot(ob_ref[...], wbd_ref[...])
    mixed = _dot((ga_ref[...].astype(F32) * y_a + gb_ref[...].astype(F32) * y_b).astype(BF16), wout_ref[...])
    x = x_ref[...] + mod_ref[5] * mixed
    o_ref[...] = _ffn_update(x, mod_ref[6], mod_ref[7], mod_ref[8], nw_ref[...],
                             wg_ref, wu_ref, wd_ref, tf)


def _mix_ffn(x, o_a, o_b, g_a, g_b, mod, w_ba, w_bd, w_out, nw, wg, wu, wd, tm, rider=None):
    b, s, d = x.shape
    r = mod.shape[2]
    d_ff = wg.shape[1]
    tf = _pick_ff_tile(d_ff, tm)
    row_spec = lambda w: pl.BlockSpec((None, tm, w), lambda i, t, *_: (i, t, 0))
    return _dense_call(
        lambda ins, outs: _mix_ffn_kernel(*ins, *outs, tf=tf), "mix_ffn", (b, s // tm),
        in_specs=[row_spec(d), row_spec(WIDTH), row_spec(WIDTH), row_spec(d), row_spec(d),
                  pl.BlockSpec((None, N_MOD, r, d), lambda i, t, *_: (i, 0, 0, 0)),
                  _const_spec((WIDTH, d)), _const_spec((WIDTH, d)), _const_spec((d, d)),
                  _const_spec((1, d)),
                  _const_spec((d, d_ff)), _const_spec((d, d_ff)), _const_spec((d_ff, d))],
        out_specs=[row_spec(d)], out_shape=[jax.ShapeDtypeStruct((b, s, d), F32)],
        operands=(x, o_a, o_b, g_a, g_b, mod, w_ba, w_bd, w_out, nw, wg, wu, wd), rider=rider)


def _moba_select_kernel(gate_ref, sel_ref):
    gate = gate_ref[...]
    nblk = gate.shape[1]
    blk_i = lax.broadcasted_iota(jnp.int32, gate.shape, 1)
    rank = jnp.zeros(gate.shape, jnp.int32)
    for m in range(nblk):
        gm = gate[:, m:m + 1]
        beats = (gm > gate) | ((gm == gate) & (m < blk_i))
        rank = rank + beats.astype(jnp.int32)
    lane = lax.broadcasted_iota(jnp.int32, sel_ref.shape, 1)
    sel = jnp.zeros(sel_ref.shape, jnp.int32)
    for j in range(MOBA_TOPK):
        sel = jnp.where(lane == j, jnp.sum(jnp.where(rank == j, blk_i, 0), axis=1, keepdims=True), sel)
    sel_ref[...] = sel


def _moba_attend(sel_ref, cols_ref, s_refs, v_refs, o_ref, s, past_len):
    npg = GATE_PAGES_PER_STEP
    lane = lax.broadcasted_iota(jnp.int32, (1, PAGE_SIZE), 1).astype(F32)
    slopes = _slopes(N_HEADS_A)
    for h in range(N_HEADS_A):
        q_h = cols_ref[0, h]
        tiles = range(h * MOBA_TOPK * PAGES_PER_BLOCK, (h + 1) * MOBA_TOPK * PAGES_PER_BLOCK)
        rows = []
        for j in range(MOBA_TOPK):
            blk = sel_ref[s, j * N_HEADS_A + h]
            for i in range(PAGES_PER_BLOCK):
                page = PAGES_PER_BLOCK * blk + i
                raw = s_refs[0][jnp.clip(page, 0, npg - 1), h:h + 1, :]
                for c in range(1, len(s_refs)):
                    other = s_refs[c][jnp.clip(page - c * npg, 0, npg - 1), h:h + 1, :]
                    raw = jnp.where(page >= c * npg, other, raw)
                dist0 = (past_len - blk * MOBA_BLOCK - i * PAGE_SIZE).astype(F32)
                rows.append(raw - slopes[h] * (dist0 - lane))
        s_self = jnp.sum(q_h * cols_ref[1, h], axis=0, keepdims=True)
        m = s_self
        for r in rows:
            m = jnp.maximum(m, jnp.max(r, axis=1, keepdims=True))
        w_self = jnp.exp(s_self - m)
        l = w_self
        acc = jnp.zeros((HEAD_DIM_A, PAGE_SIZE), F32)
        for t, r in zip(tiles, rows):
            p = jnp.exp(r - m)
            l = l + jnp.sum(p, axis=1, keepdims=True)
            acc = acc + p * v_refs[t][...]
        o = jnp.sum(acc, axis=1, keepdims=True) + w_self * cols_ref[2, h]
        o_ref[h] = o * (1.0 / l)


def _moba_pool_tiles(pool):
    return jnp.transpose(pool, (0, 1, 3, 4, 2))


def _moba_select(gates):
    gate = jnp.concatenate(gates, axis=1)[..., 0]
    db, nblk = gate.shape[:2]
    assert nblk >= MOBA_TOPK
    gate = gate.transpose(0, 2, 1).reshape(db * N_HEADS_A, nblk)
    sel = pl.pallas_call(
        _moba_select_kernel,
        grid=(1,),
        in_specs=[_const_spec(gate.shape)],
        out_specs=_const_spec((db * N_HEADS_A, LANES)),
        out_shape=jax.ShapeDtypeStruct((db * N_HEADS_A, LANES), jnp.int32),
        compiler_params=_params("arbitrary"),
        name="moba_select",
    )(gate)
    return sel[:, :MOBA_TOPK].reshape(db, N_HEADS_A, MOBA_TOPK).transpose(0, 2, 1).reshape(db, -1)


def _attend_rider(page_table, sel, cols, raws, vt_pool, layer, steps_per_batch):
    db, n_pages = page_table.shape
    nblk = n_pages // PAGES_PER_BLOCK
    assert len(raws) * GATE_PAGES_PER_STEP == n_pages
    seq = lambda i, t: i * steps_per_batch + t

    def tile_spec(h, j, i_page):
        def index(i, t, pt, sel_):
            blk = jnp.clip(sel_[seq(i, t), j * N_HEADS_A + h], 0, nblk - 1)
            return (layer, pt[seq(i, t), PAGES_PER_BLOCK * blk + i_page], h, 0, 0)
        return pl.BlockSpec((None, None, None, HEAD_DIM_A, PAGE_SIZE), index)

    tile_specs = [tile_spec(h, j, i_page) for h in range(N_HEADS_A) for j in range(MOBA_TOPK)
                  for i_page in range(PAGES_PER_BLOCK)]
    cols_spec = pl.BlockSpec((None, 3, N_HEADS_A, HEAD_DIM_A, 1), lambda i, t, pt, sel_: (seq(i, t), 0, 0, 0, 0))
    raw_spec = pl.BlockSpec((None, GATE_PAGES_PER_STEP, N_HEADS_A, PAGE_SIZE),
                            lambda i, t, pt, sel_: (seq(i, t), 0, 0, 0))
    nc = len(raws)
    return _Rider(
        prefetch=(page_table, sel),
        in_specs=[cols_spec] + [raw_spec] * nc + tile_specs,
        out_specs=[pl.BlockSpec((None, N_HEADS_A, HEAD_DIM_A, 1), lambda i, t, pt, sel_: (seq(i, t), 0, 0, 0))],
        out_shape=[jax.ShapeDtypeStruct((db, N_HEADS_A, HEAD_DIM_A, 1), F32)],
        operands=(cols,) + tuple(raws) + (vt_pool,) * len(tile_specs),
        body=lambda pf, ins, outs, s: _moba_attend(pf[1], ins[0], ins[1:1 + nc], ins[1 + nc:], outs[0], s,
                                                   n_pages * PAGE_SIZE))


def _dot_nt(a, b):
    return lax.dot_general(a, b, (((1,), (1,)), ((), ())), preferred_element_type=F32)


def _diff_decode_kernel(pt_ref, q_ref, kn_ref, vn_ref, slope_ref, tbias_ref, lam_ref, subln_ref,
                        *rest, past_len, lam_init):
    del pt_ref
    npg = PAGES_PER_STEP
    k_refs, v_refs = rest[:npg], rest[npg:2 * npg]
    o_ref, m_sc, l_sc, acc_sc = rest[2 * npg:]
    j = pl.program_id(1)
    q8 = q_ref[...]
    slope8 = slope_ref[...]
    tbias = tbias_ref[...]
    lane_max = lambda x: jnp.max(x, axis=-1, keepdims=True)

    @pl.when(j == 0)
    def _():
        s_self = jnp.sum(_bf16_round(q8) * _bf16_round(kn_ref[...]), axis=-1, keepdims=True)
        m_sc[...] = jnp.broadcast_to(s_self, m_sc.shape)
        l_sc[...] = jnp.ones(l_sc.shape, F32)
        acc_sc[...] = vn_ref[...]

    q8b = q8.astype(BF16)
    scores = []
    for idx in range(npg):
        dist0 = (past_len - (j * npg + idx) * PAGE_SIZE).astype(F32)
        scores.append(_dot_nt(q8b, k_refs[idx][...].astype(BF16)) + tbias - slope8 * dist0)
    m_old = m_sc[...]
    m_step = lane_max(functools.reduce(jnp.maximum, scores))
    m_new = jnp.maximum(m_old, m_step)
    alpha = jnp.exp(m_old - m_new)
    m1 = m_new[:, 0:1]
    p_sum = jnp.zeros(scores[0].shape, F32)
    pv = jnp.zeros(acc_sc.shape, F32)
    for idx in range(npg):
        p = jnp.exp(scores[idx] - m1)
        p_sum = p_sum + p
        pv = pv + _dot(p.astype(BF16), v_refs[idx][...].astype(BF16))
    l_new = alpha * l_sc[...] + jnp.sum(p_sum, axis=-1, keepdims=True)
    acc_new = alpha * acc_sc[...] + pv
    m_sc[...] = m_new
    l_sc[...] = l_new
    acc_sc[...] = acc_new

    @pl.when(j == pl.num_programs(1) - 1)
    def _():
        o_c = acc_new * (1.0 / l_new)
        o = o_c - _lambda(lam_ref, lam_init) * pltpu.roll(o_c, N_HEADS_B, axis=0)
        ms = jnp.mean(o * o, axis=-1, keepdims=True)
        o_ref[...] = o * lax.rsqrt(ms + RMS_EPS) * subln_ref[...] * (1.0 - lam_init)


def _diff_decode(q, k_new, v_new, k_pool, v_pool, page_table, layer, lam_vecs, subln, lam_init):
    db, n_pages = page_table.shape
    depth, n_phys = k_pool.shape[:2]
    rows = PAGE_SIZE * N_HEADS_B
    pages = lambda pool: pool.reshape(depth * n_phys, rows, V_DIM_B)
    twice = lambda a: jnp.concatenate([a, a], axis=1)
    branch = (np.arange(V_DIM_B) // HEAD_DIM_B)[None, :] == np.arange(2)[:, None]
    q8 = jnp.concatenate([q * branch[0].astype(np.float32), q * branch[1].astype(np.float32)], axis=1)
    slopes = np.tile(np.asarray(_slopes(N_HEADS_B), np.float32), 2)
    col = np.arange(rows)
    own = (col % N_HEADS_B)[None, :] == (np.arange(SUBLANES) % N_HEADS_B)[:, None]
    tbias_np = np.where(own, slopes[:, None] * (col // N_HEADS_B)[None, :], NEG).astype(np.float32)
    seq = pl.BlockSpec((None, SUBLANES, V_DIM_B), lambda s, j, pt: (s, 0, 0))
    cst = lambda shp: pl.BlockSpec(shp, lambda s, j, pt: (0, 0))
    page = lambda i: pl.BlockSpec(
        (None, rows, V_DIM_B), lambda s, j, pt: (layer * n_phys + pt[s, j * PAGES_PER_STEP + i], 0, 0))
    page_specs = [page(i) for i in range(PAGES_PER_STEP)]
    stat = pltpu.VMEM((SUBLANES, LANES), F32)
    grid_spec = pltpu.PrefetchScalarGridSpec(
        num_scalar_prefetch=1,
        grid=(db, n_pages // PAGES_PER_STEP),
        in_specs=[seq, seq, seq, cst((SUBLANES, 1)), cst((SUBLANES, rows)), cst((4, HEAD_DIM_B)), cst((1, V_DIM_B))]
        + page_specs * 2,
        out_specs=seq,
        scratch_shapes=[stat, stat, stat],
    )
    o = pl.pallas_call(
        functools.partial(_diff_decode_kernel, past_len=n_pages * PAGE_SIZE, lam_init=lam_init),
        grid_spec=grid_spec,
        out_shape=jax.ShapeDtypeStruct((db, SUBLANES, V_DIM_B), F32),
        compiler_params=_params("arbitrary", "arbitrary"),
        name="diff_decode",
    )(page_table, q8, twice(k_new), twice(v_new), jnp.asarray(slopes[:, None]), jnp.asarray(tbias_np),
      lam_vecs, subln, *([pages(k_pool)] * PAGES_PER_STEP), *([pages(v_pool)] * PAGES_PER_STEP))
    return o[:, :N_HEADS_B]


def _block_diag_ones():
    i = np.arange(MXU_DIM) // GROUP
    return jnp.asarray((i[:, None] == i[None, :]).astype(np.float32), BF16)


def kernel(x_prompt, x_sample, cache_k_moba, cache_v_moba, cache_k_diff, cache_v_diff, page_table, c_prompt, c_sample, w_ada, b_ada, norm_ffn1, ffn1_w_gate, ffn1_w_up, ffn1_w_down, norm_mix, w_in, qn_moba, kn_moba, qn_diff, kn_diff, lambda_q1, lambda_k1, lambda_q2, lambda_k2, subln_diff, w_branch_moba, w_branch_diff, w_out, norm_ffn2, ffn2_w_gate, ffn2_w_up, ffn2_w_down):
    depth = w_ada.shape[0]
    b, s, d = x_prompt.shape
    db, t_new, _ = x_sample.shape
    assert t_new == 1 and s % TOKEN_TILE == 0 and db % SUBLANES == 0
    n_pages = page_table.shape[1]
    assert n_pages % PAGES_PER_STEP == 0
    bd = _block_diag_ones()
    tile8 = lambda v: jnp.tile(v, WIDTH // v.shape[0]).reshape(1, WIDTH)

    y_p, y_s = x_prompt, x_sample.reshape(1, db, d)
    rows_p, rows_s = [], []
    for l in range(depth):
        lam_init = _lambda_init(l)
        row = lambda v: v[l].reshape(1, -1)
        mod = _ada(jnp.concatenate([c_prompt, c_sample], axis=0), w_ada[l], b_ada[l])
        mod_p = mod[:b].reshape(b, N_MOD, 1, d)
        mod_s = mod[b:].reshape(db, N_MOD, d).transpose(1, 0, 2).reshape(1, N_MOD, db, d)
        w_bf = _to_bf16([w[l] for w in (ffn1_w_gate, ffn1_w_up, ffn1_w_down, ffn2_w_gate, ffn2_w_up, ffn2_w_down,
                                        w_in, w_branch_moba, w_branch_diff, w_out)])
        ffn1 = (row(norm_ffn1),) + tuple(w_bf[0:3])
        ffn2 = (row(norm_ffn2),) + tuple(w_bf[3:6])
        w_in_bf = w_bf[6]
        gains = jnp.concatenate([tile8(qn_moba[l]), tile8(kn_moba[l]), tile8(qn_diff[l]), tile8(kn_diff[l])], 0)
        lam_vecs = jnp.stack([lambda_q1[l], lambda_k1[l], lambda_q2[l], lambda_k2[l]])
        mix_w = tuple(w_bf[7:10])

        (x1s,) = _ffn(y_s, mod_s, *ffn1, k0=0, tm=db)
        (q_as, k_as, v_as, q_bs, k_bs, v_bs, g_as, g_bs) = _proj(
            x1s, mod_s, row(norm_mix), w_in_bf, gains, bd, db, False)
        heads_a = lambda a: a.reshape(db, N_HEADS_A, HEAD_DIM_A)
        heads_b = lambda a: a.reshape(db, N_HEADS_B, V_DIM_B)
        cols = jnp.stack([heads_a(q_as), heads_a(k_as), heads_a(v_as)], axis=1)[..., None]

        steps = s // TOKEN_TILE
        kt_pool, vt_pool = _moba_pool_tiles(cache_k_moba), _moba_pool_tiles(cache_v_moba)
        gate_rider = lambda first: _gate_rider(page_table, cols, kt_pool, l, first, steps)
        x1, gate0, raw0 = _ffn(y_p, mod_p, *ffn1, k0=0, tm=TOKEN_TILE, rider=gate_rider(0))
        (kt_a, vt32_a, k_b, v_b, ka_a, ka_b, qt_a, qt_b, vt_a, vt_b, km_a, g_a, g_b, gate1, raw1) = _proj(
            x1, mod_p, row(norm_mix), w_in_bf, gains, bd, TOKEN_TILE, True, rider=gate_rider(GATE_PAGES_PER_STEP))
        o_a, o_b = _prompt_attention(qt_a, ka_a, vt_a, km_a.reshape(b, -1, WIDTH), qt_b, ka_b, vt_b,
                                     lam_vecs, tile8(subln_diff[l]), lam_init)
        attend = _attend_rider(page_table, _moba_select([gate0, gate1]), cols, [raw0, raw1], vt_pool, l, steps)
        y_p, o_as = _mix_ffn(x1, o_a, o_b, g_a, g_b, mod_p, *mix_w, *ffn2, tm=TOKEN_TILE, rider=attend)
        token_major = lambda a: a.reshape(b, N_HEADS_A, HEAD_DIM_A, s).transpose(0, 3, 1, 2)
        rows_p.append((token_major(kt_a), token_major(vt32_a), k_b, v_b))

        o_bs = _diff_decode(heads_b(q_bs), heads_b(k_bs), heads_b(v_bs), cache_k_diff, cache_v_diff,
                            page_table, l, lam_vecs, subln_diff[l].reshape(1, V_DIM_B), lam_init)
        as_rows = lambda a: a.reshape(1, db, WIDTH).astype(BF16)
        (y_s,) = _mix_ffn(x1s, as_rows(o_as), as_rows(o_bs), g_as, g_bs, mod_s, *mix_w, *ffn2, tm=db)
        rows_s.append((k_as, v_as, k_bs, v_bs))

    def stack(rows, i, lead, heads, hd):
        return jnp.stack([r[i].reshape(lead + (heads, hd)) for r in rows])

    lp, ls = (b, s), (db, 1)
    return (y_p, y_s.reshape(db, 1, d),
            stack(rows_p, 0, lp, N_HEADS_A, HEAD_DIM_A), stack(rows_p, 1, lp, N_HEADS_A, HEAD_DIM_A),
            stack(rows_p, 2, lp, N_HEADS_B, V_DIM_B), stack(rows_p, 3, lp, N_HEADS_B, V_DIM_B),
            stack(rows_s, 0, ls, N_HEADS_A, HEAD_DIM_A), stack(rows_s, 1, ls, N_HEADS_A, HEAD_DIM_A),
            stack(rows_s, 2, ls, N_HEADS_B, V_DIM_B), stack(rows_s, 3, ls, N_HEADS_B, V_DIM_B))
```

```python
import functools
import math
from typing import Callable, NamedTuple

import jax
import jax.numpy as jnp
import numpy as np
from jax import lax
from jax.experimental import pallas as pl
from jax.experimental.pallas import tpu as pltpu

F32 = jnp.float32
BF16 = jnp.bfloat16

N_HEADS_A = 8
HEAD_DIM_A = 64
MOBA_BLOCK = 256
MOBA_TOPK = 3
N_HEADS_B = 4
HEAD_DIM_B = 64
V_DIM_B = 2 * HEAD_DIM_B
WIDTH = 512
GROUP = 64
N_GROUPS = WIDTH // GROUP
PAGE_SIZE = 128
PAGES_PER_BLOCK = MOBA_BLOCK // PAGE_SIZE
N_MOD = 9
RMS_EPS = 1e-6
QK_SCALE = GROUP ** -0.5
NEG = -1e30

LANES = 128
SUBLANES = 8
BF16_SUBLANES = 16
MXU_DIM = 256
VMEM_LIMIT_BYTES = 58 * 1024 * 1024

TOKEN_TILE = 512
MAX_FF_TILE = 1408
ADA_STEPS = 4
CAST_STEPS = 8
PAGES_PER_STEP = 32
GATE_PAGES_PER_STEP = 32


def _slopes(n_heads):
    return [2.0 ** (-8.0 * (i + 1) / n_heads) for i in range(n_heads)]


def _lambda_init(layer):
    return 0.8 - 0.6 * math.exp(-0.3 * layer)


def _dot(a, b):
    return jnp.dot(a, b, preferred_element_type=F32)


def _bf16_round(x):
    return x.astype(BF16).astype(F32)


def _rms(x, w):
    ms = jnp.mean(x * x, axis=-1, keepdims=True)
    return x * lax.rsqrt(ms + RMS_EPS) * w


def _pick_ff_tile(d_ff, rows):
    cap = MXU_DIM if rows >= MXU_DIM else MAX_FF_TILE
    best = LANES
    for t in range(LANES, min(d_ff, cap) + 1, LANES):
        if d_ff % t == 0:
            best = t
    return best


def _params(*sem):
    return pltpu.CompilerParams(dimension_semantics=sem, vmem_limit_bytes=VMEM_LIMIT_BYTES)


def _const_spec(shape):
    nd = len(shape)
    return pl.BlockSpec(shape, lambda *_: (0,) * nd)


def _cast_kernel(*refs):
    n = len(refs) // 2
    for w_ref, o_ref in zip(refs[:n], refs[n:]):
        o_ref[...] = w_ref[...].astype(BF16)


def _to_bf16(ws):
    specs = [pl.BlockSpec((w.shape[0] // CAST_STEPS, w.shape[1]), lambda i: (i, 0)) for w in ws]
    assert all(w.shape[0] % (CAST_STEPS * BF16_SUBLANES) == 0 for w in ws)
    return pl.pallas_call(
        _cast_kernel,
        grid=(CAST_STEPS,),
        in_specs=specs,
        out_specs=specs,
        out_shape=[jax.ShapeDtypeStruct(w.shape, BF16) for w in ws],
        compiler_params=_params("arbitrary"),
        name="to_bf16",
    )(*ws)


def _ada_kernel(c_ref, w_ref, b_ref, o_ref):
    c = c_ref[...]
    s = c * jax.nn.sigmoid(c)
    o_ref[...] = _dot(s.astype(BF16), w_ref[...].astype(BF16)) + b_ref[...]


def _ada(c, w, b):
    m, d = c.shape
    n = w.shape[1]
    tn = n // ADA_STEPS if n % (ADA_STEPS * LANES) == 0 else n
    return pl.pallas_call(
        _ada_kernel,
        grid=(n // tn,),
        in_specs=[pl.BlockSpec((m, d), lambda j: (0, 0)),
                  pl.BlockSpec((d, tn), lambda j: (0, j)),
                  pl.BlockSpec((1, tn), lambda j: (0, j))],
        out_specs=pl.BlockSpec((m, tn), lambda j: (0, j)),
        out_shape=jax.ShapeDtypeStruct((m, n), F32),
        compiler_params=_params("arbitrary"),
        name="ada_mod",
    )(c, w, b.reshape(1, n))


def _ffn_update(x, shift, scale, gate, nw, wg_ref, wu_ref, wd_ref, tf):
    h = (_rms(x, nw) * (1.0 + scale) + shift).astype(BF16)
    d_ff = wg_ref.shape[1]
    acc = jnp.zeros(x.shape, F32)
    for j in range(d_ff // tf):
        g = _dot(h, wg_ref[:, j * tf:(j + 1) * tf])
        u = _dot(h, wu_ref[:, j * tf:(j + 1) * tf])
        a = (g * jax.nn.sigmoid(g) * u).astype(BF16)
        acc = acc + _dot(a, wd_ref[j * tf:(j + 1) * tf, :])
    return x + 0.5 * gate * acc


class _Rider(NamedTuple):
    prefetch: tuple
    in_specs: list
    out_specs: list
    out_shape: list
    operands: tuple
    body: Callable


def _gate_pages(q_ref, k_refs, gate_ref, s_ref):
    q = q_ref[...]
    q_bf = _bf16_round(q)
    for bi in range(len(k_refs) // PAGES_PER_BLOCK):
        pages = [k_refs[bi * PAGES_PER_BLOCK + i][...] for i in range(PAGES_PER_BLOCK)]
        for i, k_page in enumerate(pages):
            s_ref[bi * PAGES_PER_BLOCK + i] = jnp.sum(k_page * q, axis=1)
        ksum = jnp.sum(functools.reduce(jnp.add, pages), axis=-1, keepdims=True)
        gate_ref[bi] = jnp.sum(_bf16_round(ksum * (1.0 / MOBA_BLOCK)) * q_bf, axis=1)


def _gate_rider(page_table, cols, kt_pool, layer, first_page, steps_per_batch):
    db = page_table.shape[0]
    npg = GATE_PAGES_PER_STEP
    seq = lambda i, t: i * steps_per_batch + t
    page = lambda k: pl.BlockSpec(
        (None, None, N_HEADS_A, HEAD_DIM_A, PAGE_SIZE),
        lambda i, t, pt: (layer, pt[seq(i, t), first_page + k], 0, 0, 0))
    per_seq = lambda shp: pl.BlockSpec((None,) + shp, lambda i, t, pt: (seq(i, t), 0, 0, 0))
    q_spec = pl.BlockSpec((None, None, N_HEADS_A, HEAD_DIM_A, 1), lambda i, t, pt: (seq(i, t), 0, 0, 0, 0))
    return _Rider(
        prefetch=(page_table,),
        in_specs=[q_spec] + [page(k) for k in range(npg)],
        out_specs=[per_seq((npg // PAGES_PER_BLOCK, N_HEADS_A, 1)), per_seq((npg, N_HEADS_A, PAGE_SIZE))],
        out_shape=[jax.ShapeDtypeStruct((db, npg // PAGES_PER_BLOCK, N_HEADS_A, 1), F32),
                   jax.ShapeDtypeStruct((db, npg, N_HEADS_A, PAGE_SIZE), F32)],
        operands=(cols,) + (kt_pool,) * npg,
        body=lambda pf, ins, outs, s: _gate_pages(ins[0], ins[1:], *outs))


def _cast_rider(ws, steps_per_batch, n_seq):
    def spec(w):
        rows = w.shape[0] // n_seq
        assert w.shape[0] % n_seq == 0 and (2 * rows) % BF16_SUBLANES == 0
        if rows % BF16_SUBLANES == 0:
            return pl.BlockSpec((rows, w.shape[1]), lambda i, t, *_: (i * steps_per_batch + t, 0))
        return pl.BlockSpec((2 * rows, w.shape[1]), lambda i, t, *_: ((i * steps_per_batch + t) // 2, 0))

    def body(pf, ins, outs, s):
        for w_ref, o_ref in zip(ins, outs):
            o_ref[...] = w_ref[...].astype(BF16)

    specs = [spec(w) for w in ws]
    return _Rider(prefetch=(), in_specs=specs, out_specs=specs,
                  out_shape=[jax.ShapeDtypeStruct(w.shape, BF16) for w in ws], operands=tuple(ws), body=body)


def _both(first, second):
    assert not second.prefetch
    n_in, n_out = len(first.in_specs), len(first.out_specs)

    def body(pf, ins, outs, s):
        first.body(pf, ins[:n_in], outs[:n_out], s)
        second.body((), ins[n_in:], outs[n_out:], s)

    return _Rider(first.prefetch, first.in_specs + second.in_specs, first.out_specs + second.out_specs,
                  first.out_shape + second.out_shape, first.operands + second.operands, body)


def _dense_call(body, name, grid, in_specs, out_specs, out_shape, operands, rider=None):
    n_in, n_out = len(in_specs), len(out_specs)
    if rider is None:
        def kernel_fn(*refs):
            body(refs[:n_in], refs[n_in:])
        return pl.pallas_call(kernel_fn, grid=grid, in_specs=in_specs, out_specs=out_specs, out_shape=out_shape,
                              compiler_params=_params("arbitrary", "arbitrary"), name=name)(*operands)
    n_pf = len(rider.prefetch)
    assert rider.out_shape[0].shape[0] == grid[0] * grid[1]

    def kernel_fn(*refs):
        pf, refs = refs[:n_pf], refs[n_pf:]
        n_all_in = n_in + len(rider.in_specs)
        ins, outs = refs[:n_all_in], refs[n_all_in:]
        body(ins[:n_in], outs[:n_out])
        rider.body(pf, ins[n_in:], outs[n_out:], pl.program_id(0) * grid[1] + pl.program_id(1))

    return pl.pallas_call(
        kernel_fn,
        grid_spec=pltpu.PrefetchScalarGridSpec(
            num_scalar_prefetch=n_pf, grid=grid, in_specs=list(in_specs) + rider.in_specs,
            out_specs=list(out_specs) + rider.out_specs),
        out_shape=list(out_shape) + rider.out_shape,
        compiler_params=_params("arbitrary", "arbitrary"),
        name=name,
    )(*rider.prefetch, *operands, *rider.operands)


def _ffn(x, mod, nw, wg, wu, wd, k0, tm, rider=None):
    b, s, d = x.shape
    r = mod.shape[2]
    d_ff = wg.shape[1]
    tf = _pick_ff_tile(d_ff, tm)

    def body(ins, outs):
        x_ref, mod_ref, nw_ref, wg_ref, wu_ref, wd_ref = ins
        outs[0][...] = _ffn_update(x_ref[...], mod_ref[k0], mod_ref[k0 + 1], mod_ref[k0 + 2],
                                   nw_ref[...], wg_ref, wu_ref, wd_ref, tf)

    return _dense_call(
        body, "ffn", (b, s // tm),
        in_specs=[pl.BlockSpec((None, tm, d), lambda i, t, *_: (i, t, 0)),
                  pl.BlockSpec((None, N_MOD, r, d), lambda i, t, *_: (i, 0, 0, 0)),
                  _const_spec((1, d)),
                  _const_spec((d, d_ff)), _const_spec((d, d_ff)), _const_spec((d_ff, d))],
        out_specs=[pl.BlockSpec((None, tm, d), lambda i, t, *_: (i, t, 0))],
        out_shape=[jax.ShapeDtypeStruct((b, s, d), F32)],
        operands=(x, mod, nw, wg, wu, wd), rider=rider)


def _head_norm(seg, gain_row, bd):
    sq = (seg * seg).astype(BF16)
    parts = [_dot(sq[:, c * MXU_DIM:(c + 1) * MXU_DIM], bd) for c in range(WIDTH // MXU_DIM)]
    ms = jnp.concatenate(parts, axis=1) * (1.0 / GROUP)
    return seg * lax.rsqrt(ms + RMS_EPS) * gain_row


def _proj_kernel(x_ref, mod_ref, nw_ref, win_ref, gains_ref, bd_ref, *outs, transposed):
    x = x_ref[...]
    d = x.shape[1]
    h = (_rms(x, nw_ref[...]) * (1.0 + mod_ref[4]) + mod_ref[3]).astype(BF16)
    bd = bd_ref[...]

    def seg(j):
        return _dot(h, win_ref[:, j * WIDTH:(j + 1) * WIDTH])

    q_a = _head_norm(seg(0), gains_ref[0:1, :], bd) * QK_SCALE
    k_a = _head_norm(seg(1), gains_ref[1:2, :], bd)
    v_a = seg(2)
    q_b = _head_norm(seg(3), gains_ref[2:3, :], bd) * QK_SCALE
    k_b = _head_norm(seg(4), gains_ref[3:4, :], bd)
    v_b = seg(5)
    g0 = 6 * WIDTH
    g_a = jax.nn.sigmoid(_dot(h, win_ref[:, g0:g0 + d]))
    g_b = jax.nn.sigmoid(_dot(h, win_ref[:, g0 + d:g0 + 2 * d]))

    if not transposed:
        (qa_ref, ka_ref, va_ref, qb_ref, kb_ref, vb_ref, ga_ref, gb_ref) = outs
        qa_ref[...] = q_a
        ka_ref[...] = k_a
        va_ref[...] = v_a
        qb_ref[...] = q_b
        kb_ref[...] = k_b
        vb_ref[...] = v_b
    else:
        (kta_ref, vta32_ref, kb_ref, vb_ref, kaa_ref, kab_ref, qta_ref, qtb_ref,
         vta_ref, vtb_ref, km_ref, ga_ref, gb_ref) = outs
        kta_ref[...] = k_a.T
        vt_a = v_a.T
        vta32_ref[...] = vt_a
        qt_a, qt_b, vt_b = q_a.T, q_b.T, v_b.T
        for r in range(x.shape[0] // MOBA_BLOCK):
            rows = slice(r * MOBA_BLOCK, (r + 1) * MOBA_BLOCK)
            n = pl.program_id(1) * (x.shape[0] // MOBA_BLOCK) + r
            for g, (ka_g, kb_g) in enumerate(zip(_augmented_keys(k_a[rows], n), _augmented_keys(k_b[rows], n))):
                kaa_ref[r, g] = ka_g
                kab_ref[r, g] = kb_g
            qta_ref[r] = qt_a[:, rows].astype(BF16)
            qtb_ref[r] = qt_b[:, rows].astype(BF16)
            vta_ref[r] = vt_a[:, rows].astype(BF16)
            vtb_ref[r] = vt_b[:, rows].astype(BF16)
            km_ref[r] = jnp.sum(k_a[rows], axis=0, keepdims=True) * (1.0 / MOBA_BLOCK)
        for hd in range(N_HEADS_B):
            lanes = slice(V_DIM_B * hd, V_DIM_B * (hd + 1))
            kb_ref[pl.ds(hd, x.shape[0], stride=N_HEADS_B), :] = k_b[:, lanes]
            vb_ref[pl.ds(hd, x.shape[0], stride=N_HEADS_B), :] = v_b[:, lanes]
    ga_ref[...] = g_a.astype(BF16)
    gb_ref[...] = g_b.astype(BF16)


def _proj(x, mod, nw, w_in, gains, bd, tm, transposed, rider=None):
    b, s, d = x.shape
    r = mod.shape[2]
    d_in = w_in.shape[1]
    row_spec = lambda w: pl.BlockSpec((None, tm, w), lambda i, t, *_: (i, t, 0))
    rows = lambda w, dt=F32: jax.ShapeDtypeStruct((b, s, w), dt)
    if transposed:
        nb, bpt = s // MOBA_BLOCK, tm // MOBA_BLOCK
        blk = lambda shp: pl.BlockSpec((None, bpt) + shp, lambda i, t, *_: (i, t, 0, 0))
        col_spec = pl.BlockSpec((None, WIDTH, tm), lambda i, t, *_: (i, 0, t))
        head_rows = pl.BlockSpec((None, tm * N_HEADS_B, V_DIM_B), lambda i, t, *_: (i, t, 0))
        k_aug = pl.BlockSpec((None, bpt, N_GROUPS, MOBA_BLOCK, LANES), lambda i, t, *_: (i, t, 0, 0, 0))
        out_specs = [col_spec] * 2 + [head_rows] * 2 + [k_aug] * 2 \
            + [blk((WIDTH, MOBA_BLOCK))] * 4 + [blk((1, WIDTH))] + [row_spec(d)] * 2
        out_shape = [jax.ShapeDtypeStruct((b, WIDTH, s), F32)] * 2 \
            + [jax.ShapeDtypeStruct((b, s * N_HEADS_B, V_DIM_B), F32)] * 2 \
            + [jax.ShapeDtypeStruct((b, nb, N_GROUPS, MOBA_BLOCK, LANES), BF16)] * 2 \
            + [jax.ShapeDtypeStruct((b, nb, WIDTH, MOBA_BLOCK), BF16)] * 4 \
            + [jax.ShapeDtypeStruct((b, nb, 1, WIDTH), F32)] + [rows(d, BF16)] * 2
    else:
        out_specs = [row_spec(WIDTH)] * 6 + [row_spec(d)] * 2
        out_shape = [rows(WIDTH)] * 6 + [rows(d, BF16)] * 2
    return _dense_call(
        lambda ins, outs: _proj_kernel(*ins, *outs, transposed=transposed), "mixer_proj", (b, s // tm),
        in_specs=[pl.BlockSpec((None, tm, d), lambda i, t, *_: (i, t, 0)),
                  pl.BlockSpec((None, N_MOD, r, d), lambda i, t, *_: (i, 0, 0, 0)),
                  _const_spec((1, d)), _const_spec((d, d_in)),
                  _const_spec((4, WIDTH)), _const_spec((MXU_DIM, MXU_DIM))],
        out_specs=out_specs, out_shape=out_shape, operands=(x, mod, nw, w_in, gains, bd), rider=rider)


FEATURE_ROWS = SUBLANES
MAX_KEY_BLOCKS = GROUP - FEATURE_ROWS


def _augmented_keys(k, n):
    lane = lax.broadcasted_iota(jnp.int32, (MOBA_BLOCK, LANES), 1)
    key = lax.broadcasted_iota(jnp.int32, (MOBA_BLOCK, LANES), 0).astype(F32)
    feat = jnp.where(lane == GROUP, 1.0, jnp.where(lane == GROUP + 1, key,
                                                    jnp.where(lane == GROUP + FEATURE_ROWS + n, 1.0, 0.0)))
    out = []
    for g in range(N_GROUPS):
        p, half = divmod(g, 2)
        pair = k[:, LANES * p:LANES * (p + 1)]
        k_g = pair if half == 0 else pltpu.roll(pair, GROUP, axis=1)
        out.append(jnp.where(lane < GROUP, k_g, feat).astype(BF16))
    return out


def _augmented_queries(qt, slopes, block_biases):
    qry = lax.broadcasted_iota(jnp.int32, (FEATURE_ROWS, MOBA_BLOCK), 1).astype(F32)
    row = lax.broadcasted_iota(jnp.int32, (FEATURE_ROWS, MOBA_BLOCK), 0)
    out = []
    for g, (slope, bias) in enumerate(zip(slopes, block_biases)):
        head = jnp.where(row == 0, -slope * qry, jnp.where(row == 1, slope, 0.0))
        pad = jnp.zeros((GROUP - FEATURE_ROWS - bias.shape[0], MOBA_BLOCK), F32)
        feat = jnp.concatenate([head, bias, pad], axis=0).astype(BF16)
        out.append(jnp.concatenate([qt[GROUP * g:GROUP * (g + 1), :], feat], axis=0))
    return out


def _pair_scores(k_aug, q_aug, p):
    return [_dot(k_aug[2 * p + half], q_aug[2 * p + half]) for half in range(2)]


def _attend_block(k_aug, vtb, q_aug, causal, m_all, l_all, acc_sc, v_rows, st_first, s0_sc, k_next):
    m_out, l_out = [], []
    n_pairs = N_GROUPS // 2
    ones_rows = jnp.ones((BF16_SUBLANES, MOBA_BLOCK), BF16)
    st_next = _pair_scores(k_aug, q_aug, 0) if st_first is None else st_first
    for p in range(n_pairs):
        st_pair = st_next
        if p + 1 < n_pairs:
            st_next = _pair_scores(k_aug, q_aug, p + 1)
        else:
            nxt = _pair_scores(k_next, q_aug, 0)
            s0_sc[0] = nxt[0]
            s0_sc[1] = nxt[1]
        for half in range(2):
            g = 2 * p + half
            st = st_pair[half]
            if causal is not None:
                st = jnp.where(causal, st, NEG)
            m_old = m_all[g:g + 1, :]
            m_new = jnp.maximum(m_old, jnp.max(st, axis=0, keepdims=True))
            alpha = jnp.exp(m_old - m_new)
            pt = jnp.exp(st - m_new).astype(BF16)
            rows = v_rows(g)
            nr = rows.stop - rows.start
            acc_rows = slice(g * nr, (g + 1) * nr)
            pv = _dot(jnp.concatenate([vtb[rows, :], ones_rows], axis=0), pt)
            l_out.append(alpha * l_all[g:g + 1, :] + pv[nr:nr + 1, :])
            acc_sc[acc_rows, :] = alpha * acc_sc[acc_rows, :] + pv[:nr, :]
            m_out.append(m_new)
    return jnp.concatenate(m_out, axis=0), jnp.concatenate(l_out, axis=0)


def _attend_all_blocks(qi, k_ref, vt_ref, q_aug, acc_sc, s0_sc, v_rows):
    key_i = lax.broadcasted_iota(jnp.int32, (MOBA_BLOCK, MOBA_BLOCK), 0)
    qry_i = lax.broadcasted_iota(jnp.int32, (MOBA_BLOCK, MOBA_BLOCK), 1)
    acc_sc[...] = jnp.zeros(acc_sc.shape, F32)
    m0 = jnp.full((N_GROUPS, MOBA_BLOCK), NEG, F32)
    l0 = jnp.zeros((N_GROUPS, MOBA_BLOCK), F32)
    m1, l1 = _attend_block(k_ref.at[qi], vt_ref[qi], q_aug, key_i <= qry_i, m0, l0, acc_sc, v_rows,
                           None, s0_sc, k_ref.at[0])

    def past(n, carry):
        return _attend_block(k_ref.at[n], vt_ref[n], q_aug, None, *carry, acc_sc, v_rows,
                             [s0_sc[0], s0_sc[1]], s0_sc, k_ref.at[n + 1])

    _, l_fin = lax.fori_loop(0, qi, past, (m1, l1))
    return l_fin


def _block_distance(qi, nbp):
    blk_i = lax.broadcasted_iota(jnp.int32, (nbp, MOBA_BLOCK), 0)
    return blk_i, ((qi - blk_i) * MOBA_BLOCK).astype(F32)


def _moba_prompt_kernel(qt_ref, k_ref, vt_ref, km_ref, o_ref, acc_sc, s0_sc):
    qi = pl.program_id(1)
    nb = k_ref.shape[0]
    nbp = -(-nb // SUBLANES) * SUBLANES
    blk = MOBA_BLOCK
    slopes = _slopes(N_HEADS_A)
    qt = qt_ref[...]

    km = km_ref[...].astype(BF16)
    blk_i, blk_dist = _block_distance(qi, nbp)
    valid = blk_i[:nb] < qi
    biases = []
    for g in range(N_GROUPS):
        gate = _dot(km[:, GROUP * g:GROUP * (g + 1)], qt[GROUP * g:GROUP * (g + 1), :])
        gate = jnp.where(valid, gate, NEG)
        rank = jnp.zeros((nb, blk), jnp.int32)
        for m in range(nb):
            gm = gate[m:m + 1, :]
            beats = (gm > gate) | ((gm == gate) & (m < blk_i[:nb]))
            rank = rank + beats.astype(jnp.int32)
        drop = jnp.where(valid & (rank < MOBA_TOPK), 0.0, NEG)
        if nbp > nb:
            drop = jnp.concatenate([drop, jnp.zeros((nbp - nb, blk), F32)], axis=0)
        biases.append(jnp.where(blk_i == qi, 0.0, drop - slopes[g] * blk_dist))

    q_aug = _augmented_queries(qt, slopes, biases)
    l_fin = _attend_all_blocks(qi, k_ref, vt_ref, q_aug, acc_sc, s0_sc,
                               lambda g: slice(GROUP * g, GROUP * (g + 1)))
    parts = [acc_sc[GROUP * g:GROUP * (g + 1), :] * (1.0 / l_fin[g:g + 1, :]) for g in range(N_GROUPS)]
    o_ref[...] = jnp.concatenate(parts, axis=0).T.astype(BF16)


def _lambda(lam_ref, lam_init):
    a = jnp.sum(lam_ref[0:1, :] * lam_ref[1:2, :], axis=-1, keepdims=True)
    b = jnp.sum(lam_ref[2:3, :] * lam_ref[3:4, :], axis=-1, keepdims=True)
    return jnp.exp(a) - jnp.exp(b) + lam_init


def _diff_prompt_kernel(qt_ref, k_ref, vt_ref, lam_ref, subln_ref, o_ref, acc_sc, s0_sc, *, lam_init):
    qi = pl.program_id(1)
    nb = k_ref.shape[0]
    nbp = -(-nb // SUBLANES) * SUBLANES
    slopes = [s for s in _slopes(N_HEADS_B) for _ in range(2)]
    _, blk_dist = _block_distance(qi, nbp)
    q_aug = _augmented_queries(qt_ref[...], slopes, [-s * blk_dist for s in slopes])
    v_rows = lambda g: slice(V_DIM_B * (g // 2), V_DIM_B * (g // 2 + 1))
    l_fin = _attend_all_blocks(qi, k_ref, vt_ref, q_aug, acc_sc, s0_sc, v_rows)

    lam = _lambda(lam_ref, lam_init)
    parts = []
    for h in range(N_HEADS_B):
        o0 = acc_sc[V_DIM_B * (2 * h):V_DIM_B * (2 * h + 1), :] * (1.0 / l_fin[2 * h:2 * h + 1, :])
        o1 = acc_sc[V_DIM_B * (2 * h + 1):V_DIM_B * (2 * h + 2), :] * (1.0 / l_fin[2 * h + 1:2 * h + 2, :])
        o = o0 - lam * o1
        ms = jnp.mean(o * o, axis=0, keepdims=True)
        parts.append(o * lax.rsqrt(ms + RMS_EPS))
    o_t = jnp.concatenate(parts, axis=0).T
    o_ref[...] = (o_t * subln_ref[...] * (1.0 - lam_init)).astype(BF16)


def _prompt_attention(qt_a, k_aug_a, vt_a, km_a, qt_b, k_aug_b, vt_b, lam_vecs, subln_row, lam_init):
    b, nb = qt_a.shape[:2]
    assert nb <= MAX_KEY_BLOCKS
    blk = MOBA_BLOCK
    s = nb * blk
    q_spec = pl.BlockSpec((None, None, WIDTH, blk), lambda i, t: (i, t, 0, 0))
    k_spec = pl.BlockSpec((None, nb, N_GROUPS, blk, LANES), lambda i, t: (i, 0, 0, 0, 0))
    vt_spec = pl.BlockSpec((None, nb, WIDTH, blk), lambda i, t: (i, 0, 0, 0))
    o_spec = pl.BlockSpec((None, blk, WIDTH), lambda i, t: (i, t, 0))
    o_shape = jax.ShapeDtypeStruct((b, s, WIDTH), BF16)
    s0 = pltpu.VMEM((2, blk, blk), F32)
    o_a = pl.pallas_call(
        _moba_prompt_kernel,
        grid=(b, nb),
        in_specs=[q_spec, k_spec, vt_spec, pl.BlockSpec((None, nb, WIDTH), lambda i, t: (i, 0, 0))],
        out_specs=o_spec,
        out_shape=o_shape,
        scratch_shapes=[pltpu.VMEM((WIDTH, blk), F32), s0],
        compiler_params=_params("arbitrary", "arbitrary"),
        name="moba_prompt",
    )(qt_a, k_aug_a, vt_a, km_a)
    o_b = pl.pallas_call(
        functools.partial(_diff_prompt_kernel, lam_init=lam_init),
        grid=(b, nb),
        in_specs=[q_spec, k_spec, vt_spec, _const_spec((4, HEAD_DIM_B)), _const_spec((1, WIDTH))],
        out_specs=o_spec,
        out_shape=o_shape,
        scratch_shapes=[pltpu.VMEM((N_GROUPS * V_DIM_B, blk), F32), s0],
        compiler_params=_params("arbitrary", "arbitrary"),
        name="diff_prompt",
    )(qt_b, k_aug_b, vt_b, lam_vecs, subln_row)
    return o_a, o_b


def _mix_ffn_kernel(x_ref, oa_ref, ob_ref, ga_ref, gb_ref, mod_ref, wba_ref, wbd_ref, wout_ref,
                    nw_ref, wg_ref, wu_ref, wd_ref, o_ref, *, tf):
    y_a = _dot(oa_ref[...], wba_ref[...])
    y_b = _dot(ob_ref[...], wbd_ref[...])
    mixed = _dot((ga_ref[...].astype(F32) * y_a + gb_ref[...].astype(F32) * y_b).astype(BF16), wout_ref[...])
    x = x_ref[...] + mod_ref[5] * mixed
    o_ref[...] = _ffn_update(x, mod_ref[6], mod_ref[7], mod_ref[8], nw_ref[...],
                             wg_ref, wu_ref, wd_ref, tf)


def _mix_ffn(x, o_a, o_b, g_a, g_b, mod, w_ba, w_bd, w_out, nw, wg, wu, wd, tm, rider=None):
    b, s, d = x.shape
    r = mod.shape[2]
    d_ff = wg.shape[1]
    tf = _pick_ff_tile(d_ff, tm)
    row_spec = lambda w: pl.BlockSpec((None, tm, w), lambda i, t, *_: (i, t, 0))
    return _dense_call(
        lambda ins, outs: _mix_ffn_kernel(*ins, *outs, tf=tf), "mix_ffn", (b, s // tm),
        in_specs=[row_spec(d), row_spec(WIDTH), row_spec(WIDTH), row_spec(d), row_spec(d),
                  pl.BlockSpec((None, N_MOD, r, d), lambda i, t, *_: (i, 0, 0, 0)),
                  _const_spec((WIDTH, d)), _const_spec((WIDTH, d)), _const_spec((d, d)),
                  _const_spec((1, d)),
                  _const_spec((d, d_ff)), _const_spec((d, d_ff)), _const_spec((d_ff, d))],
        out_specs=[row_spec(d)], out_shape=[jax.ShapeDtypeStruct((b, s, d), F32)],
        operands=(x, o_a, o_b, g_a, g_b, mod, w_ba, w_bd, w_out, nw, wg, wu, wd), rider=rider)


def _moba_select_kernel(gate_ref, sel_ref):
    gate = gate_ref[...]
    nblk = gate.shape[1]
    blk_i = lax.broadcasted_iota(jnp.int32, gate.shape, 1)
    rank = jnp.zeros(gate.shape, jnp.int32)
    for m in range(nblk):
        gm = gate[:, m:m + 1]
        beats = (gm > gate) | ((gm == gate) & (m < blk_i))
        rank = rank + beats.astype(jnp.int32)
    lane = lax.broadcasted_iota(jnp.int32, sel_ref.shape, 1)
    sel = jnp.zeros(sel_ref.shape, jnp.int32)
    for j in range(MOBA_TOPK):
        sel = jnp.where(lane == j, jnp.sum(jnp.where(rank == j, blk_i, 0), axis=1, keepdims=True), sel)
    sel_ref[...] = sel


def _moba_attend(sel_ref, cols_ref, s_refs, v_refs, o_ref, s, past_len):
    npg = GATE_PAGES_PER_STEP
    lane = lax.broadcasted_iota(jnp.int32, (1, PAGE_SIZE), 1).astype(F32)
    slopes = _slopes(N_HEADS_A)
    for h in range(N_HEADS_A):
        q_h = cols_ref[0, h]
        tiles = range(h * MOBA_TOPK * PAGES_PER_BLOCK, (h + 1) * MOBA_TOPK * PAGES_PER_BLOCK)
        rows = []
        for j in range(MOBA_TOPK):
            blk = sel_ref[s, j * N_HEADS_A + h]
            for i in range(PAGES_PER_BLOCK):
                page = PAGES_PER_BLOCK * blk + i
                raw = s_refs[0][jnp.clip(page, 0, npg - 1), h:h + 1, :]
                for c in range(1, len(s_refs)):
                    other = s_refs[c][jnp.clip(page - c * npg, 0, npg - 1), h:h + 1, :]
                    raw = jnp.where(page >= c * npg, other, raw)
                dist0 = (past_len - blk * MOBA_BLOCK - i * PAGE_SIZE).astype(F32)
                rows.append(raw - slopes[h] * (dist0 - lane))
        s_self = jnp.sum(q_h * cols_ref[1, h], axis=0, keepdims=True)
        m = s_self
        for r in rows:
            m = jnp.maximum(m, jnp.max(r, axis=1, keepdims=True))
        w_self = jnp.exp(s_self - m)
        l = w_self
        acc = jnp.zeros((HEAD_DIM_A, PAGE_SIZE), F32)
        for t, r in zip(tiles, rows):
            p = jnp.exp(r - m)
            l = l + jnp.sum(p, axis=1, keepdims=True)
            acc = acc + p * v_refs[t][...]
        o = jnp.sum(acc, axis=1, keepdims=True) + w_self * cols_ref[2, h]
        o_ref[h] = o * (1.0 / l)


def _moba_pool_tiles(pool):
    return jnp.transpose(pool, (0, 1, 3, 4, 2))


def _moba_select(gates):
    gate = jnp.concatenate(gates, axis=1)[..., 0]
    db, nblk = gate.shape[:2]
    assert nblk >= MOBA_TOPK
    gate = gate.transpose(0, 2, 1).reshape(db * N_HEADS_A, nblk)
    sel = pl.pallas_call(
        _moba_select_kernel,
        grid=(1,),
        in_specs=[_const_spec(gate.shape)],
        out_specs=_const_spec((db * N_HEADS_A, LANES)),
        out_shape=jax.ShapeDtypeStruct((db * N_HEADS_A, LANES), jnp.int32),
        compiler_params=_params("arbitrary"),
        name="moba_select",
    )(gate)
    return sel[:, :MOBA_TOPK].reshape(db, N_HEADS_A, MOBA_TOPK).transpose(0, 2, 1).reshape(db, -1)


def _attend_rider(page_table, sel, cols, raws, vt_pool, layer, steps_per_batch):
    db, n_pages = page_table.shape
    nblk = n_pages // PAGES_PER_BLOCK
    assert len(raws) * GATE_PAGES_PER_STEP == n_pages
    seq = lambda i, t: i * steps_per_batch + t

    def tile_spec(h, j, i_page):
        def index(i, t, pt, sel_):
            blk = jnp.clip(sel_[seq(i, t), j * N_HEADS_A + h], 0, nblk - 1)
            return (layer, pt[seq(i, t), PAGES_PER_BLOCK * blk + i_page], h, 0, 0)
        return pl.BlockSpec((None, None, None, HEAD_DIM_A, PAGE_SIZE), index)

    tile_specs = [tile_spec(h, j, i_page) for h in range(N_HEADS_A) for j in range(MOBA_TOPK)
                  for i_page in range(PAGES_PER_BLOCK)]
    cols_spec = pl.BlockSpec((None, 3, N_HEADS_A, HEAD_DIM_A, 1), lambda i, t, pt, sel_: (seq(i, t), 0, 0, 0, 0))
    raw_spec = pl.BlockSpec((None, GATE_PAGES_PER_STEP, N_HEADS_A, PAGE_SIZE),
                            lambda i, t, pt, sel_: (seq(i, t), 0, 0, 0))
    nc = len(raws)
    return _Rider(
        prefetch=(page_table, sel),
        in_specs=[cols_spec] + [raw_spec] * nc + tile_specs,
        out_specs=[pl.BlockSpec((None, N_HEADS_A, HEAD_DIM_A, 1), lambda i, t, pt, sel_: (seq(i, t), 0, 0, 0))],
        out_shape=[jax.ShapeDtypeStruct((db, N_HEADS_A, HEAD_DIM_A, 1), F32)],
        operands=(cols,) + tuple(raws) + (vt_pool,) * len(tile_specs),
        body=lambda pf, ins, outs, s: _moba_attend(pf[1], ins[0], ins[1:1 + nc], ins[1 + nc:], outs[0], s,
                                                   n_pages * PAGE_SIZE))


def _dot_nt(a, b):
    return lax.dot_general(a, b, (((1,), (1,)), ((), ())), preferred_element_type=F32)


def _diff_decode_kernel(pt_ref, q_ref, kn_ref, vn_ref, slope_ref, tbias_ref, lam_ref, subln_ref,
                        *rest, past_len, lam_init):
    del pt_ref
    npg = PAGES_PER_STEP
    k_refs, v_refs = rest[:npg], rest[npg:2 * npg]
    o_ref, m_sc, l_sc, acc_sc = rest[2 * npg:]
    j = pl.program_id(1)
    q8 = q_ref[...]
    slope8 = slope_ref[...]
    tbias = tbias_ref[...]
    lane_max = lambda x: jnp.max(x, axis=-1, keepdims=True)

    @pl.when(j == 0)
    def _():
        s_self = jnp.sum(_bf16_round(q8) * _bf16_round(kn_ref[...]), axis=-1, keepdims=True)
        m_sc[...] = jnp.broadcast_to(s_self, m_sc.shape)
        l_sc[...] = jnp.ones(l_sc.shape, F32)
        acc_sc[...] = vn_ref[...]

    q8b = q8.astype(BF16)
    scores = []
    for idx in range(npg):
        dist0 = (past_len - (j * npg + idx) * PAGE_SIZE).astype(F32)
        scores.append(_dot_nt(q8b, k_refs[idx][...].astype(BF16)) + tbias - slope8 * dist0)
    m_old = m_sc[...]
    m_step = lane_max(functools.reduce(jnp.maximum, scores))
    m_new = jnp.maximum(m_old, m_step)
    alpha = jnp.exp(m_old - m_new)
    m1 = m_new[:, 0:1]
    p_sum = jnp.zeros(scores[0].shape, F32)
    pv = jnp.zeros(acc_sc.shape, F32)
    for idx in range(npg):
        p = jnp.exp(scores[idx] - m1)
        p_sum = p_sum + p
        pv = pv + _dot(p.astype(BF16), v_refs[idx][...].astype(BF16))
    l_new = alpha * l_sc[...] + jnp.sum(p_sum, axis=-1, keepdims=True)
    acc_new = alpha * acc_sc[...] + pv
    m_sc[...] = m_new
    l_sc[...] = l_new
    acc_sc[...] = acc_new

    @pl.when(j == pl.num_programs(1) - 1)
    def _():
        o_c = acc_new * (1.0 / l_new)
        o = o_c - _lambda(lam_ref, lam_init) * pltpu.roll(o_c, N_HEADS_B, axis=0)
        ms = jnp.mean(o * o, axis=-1, keepdims=True)
        o_ref[...] = o * lax.rsqrt(ms + RMS_EPS) * subln_ref[...] * (1.0 - lam_init)


def _diff_decode(q, k_new, v_new, k_pool, v_pool, page_table, layer, lam_vecs, subln, lam_init):
    db, n_pages = page_table.shape
    depth, n_phys = k_pool.shape[:2]
    rows = PAGE_SIZE * N_HEADS_B
    pages = lambda pool: pool.reshape(depth * n_phys, rows, V_DIM_B)
    twice = lambda a: jnp.concatenate([a, a], axis=1)
    branch = (np.arange(V_DIM_B) // HEAD_DIM_B)[None, :] == np.arange(2)[:, None]
    q8 = jnp.concatenate([q * branch[0].astype(np.float32), q * branch[1].astype(np.float32)], axis=1)
    slopes = np.tile(np.asarray(_slopes(N_HEADS_B), np.float32), 2)
    col = np.arange(rows)
    own = (col % N_HEADS_B)[None, :] == (np.arange(SUBLANES) % N_HEADS_B)[:, None]
    tbias_np = np.where(own, slopes[:, None] * (col // N_HEADS_B)[None, :], NEG).astype(np.float32)
    seq = pl.BlockSpec((None, SUBLANES, V_DIM_B), lambda s, j, pt: (s, 0, 0))
    cst = lambda shp: pl.BlockSpec(shp, lambda s, j, pt: (0, 0))
    page = lambda i: pl.BlockSpec(
        (None, rows, V_DIM_B), lambda s, j, pt: (layer * n_phys + pt[s, j * PAGES_PER_STEP + i], 0, 0))
    page_specs = [page(i) for i in range(PAGES_PER_STEP)]
    stat = pltpu.VMEM((SUBLANES, LANES), F32)
    grid_spec = pltpu.PrefetchScalarGridSpec(
        num_scalar_prefetch=1,
        grid=(db, n_pages // PAGES_PER_STEP),
        in_specs=[seq, seq, seq, cst((SUBLANES, 1)), cst((SUBLANES, rows)), cst((4, HEAD_DIM_B)), cst((1, V_DIM_B))]
        + page_specs * 2,
        out_specs=seq,
        scratch_shapes=[stat, stat, stat],
    )
    o = pl.pallas_call(
        functools.partial(_diff_decode_kernel, past_len=n_pages * PAGE_SIZE, lam_init=lam_init),
        grid_spec=grid_spec,
        out_shape=jax.ShapeDtypeStruct((db, SUBLANES, V_DIM_B), F32),
        compiler_params=_params("arbitrary", "arbitrary"),
        name="diff_decode",
    )(page_table, q8, twice(k_new), twice(v_new), jnp.asarray(slopes[:, None]), jnp.asarray(tbias_np),
      lam_vecs, subln, *([pages(k_pool)] * PAGES_PER_STEP), *([pages(v_pool)] * PAGES_PER_STEP))
    return o[:, :N_HEADS_B]


def _block_diag_ones():
    i = np.arange(MXU_DIM) // GROUP
    return jnp.asarray((i[:, None] == i[None, :]).astype(np.float32), BF16)


def kernel(x_prompt, x_sample, cache_k_moba, cache_v_moba, cache_k_diff, cache_v_diff, page_table, c_prompt, c_sample, w_ada, b_ada, norm_ffn1, ffn1_w_gate, ffn1_w_up, ffn1_w_down, norm_mix, w_in, qn_moba, kn_moba, qn_diff, kn_diff, lambda_q1, lambda_k1, lambda_q2, lambda_k2, subln_diff, w_branch_moba, w_branch_diff, w_out, norm_ffn2, ffn2_w_gate, ffn2_w_up, ffn2_w_down):
    depth = w_ada.shape[0]
    b, s, d = x_prompt.shape
    db, t_new, _ = x_sample.shape
    assert t_new == 1 and s % TOKEN_TILE == 0 and db % SUBLANES == 0
    n_pages = page_table.shape[1]
    assert n_pages % PAGES_PER_STEP == 0
    bd = _block_diag_ones()
    tile8 = lambda v: jnp.tile(v, WIDTH // v.shape[0]).reshape(1, WIDTH)

    y_p, y_s = x_prompt, x_sample.reshape(1, db, d)
    rows_p, rows_s = [], []
    for l in range(depth):
        lam_init = _lambda_init(l)
        row = lambda v: v[l].reshape(1, -1)
        mod = _ada(jnp.concatenate([c_prompt, c_sample], axis=0), w_ada[l], b_ada[l])
        mod_p = mod[:b].reshape(b, N_MOD, 1, d)
        mod_s = mod[b:].reshape(db, N_MOD, d).transpose(1, 0, 2).reshape(1, N_MOD, db, d)
        w_bf = _to_bf16([w[l] for w in (ffn1_w_gate, ffn1_w_up, ffn1_w_down, w_in)])
        ffn1 = (row(norm_ffn1),) + tuple(w_bf[0:3])
        w_in_bf = w_bf[3]
        late_w = [w[l] for w in (ffn2_w_gate, ffn2_w_up, ffn2_w_down, w_branch_moba, w_branch_diff, w_out)]
        gains = jnp.concatenate([tile8(qn_moba[l]), tile8(kn_moba[l]), tile8(qn_diff[l]), tile8(kn_diff[l])], 0)
        lam_vecs = jnp.stack([lambda_q1[l], lambda_k1[l], lambda_q2[l], lambda_k2[l]])

        (x1s,) = _ffn(y_s, mod_s, *ffn1, k0=0, tm=db)
        (q_as, k_as, v_as, q_bs, k_bs, v_bs, g_as, g_bs) = _proj(
            x1s, mod_s, row(norm_mix), w_in_bf, gains, bd, db, False)
        heads_a = lambda a: a.reshape(db, N_HEADS_A, HEAD_DIM_A)
        heads_b = lambda a: a.reshape(db, N_HEADS_B, V_DIM_B)
        cols = jnp.stack([heads_a(q_as), heads_a(k_as), heads_a(v_as)], axis=1)[..., None]

        steps = s // TOKEN_TILE
        kt_pool, vt_pool = _moba_pool_tiles(cache_k_moba), _moba_pool_tiles(cache_v_moba)
        gate_rider = lambda first: _gate_rider(page_table, cols, kt_pool, l, first, steps)
        x1, gate0, raw0, *late_bf = _ffn(y_p, mod_p, *ffn1, k0=0, tm=TOKEN_TILE,
                                         rider=_both(gate_rider(0), _cast_rider(late_w, steps, b * steps)))
        ffn2 = (row(norm_ffn2),) + tuple(late_bf[0:3])
        mix_w = tuple(late_bf[3:6])
        (kt_a, vt32_a, k_b, v_b, ka_a, ka_b, qt_a, qt_b, vt_a, vt_b, km_a, g_a, g_b, gate1, raw1) = _proj(
            x1, mod_p, row(norm_mix), w_in_bf, gains, bd, TOKEN_TILE, True, rider=gate_rider(GATE_PAGES_PER_STEP))
        o_a, o_b = _prompt_attention(qt_a, ka_a, vt_a, km_a.reshape(b, -1, WIDTH), qt_b, ka_b, vt_b,
                                     lam_vecs, tile8(subln_diff[l]), lam_init)
        attend = _attend_rider(page_table, _moba_select([gate0, gate1]), cols, [raw0, raw1], vt_pool, l, steps)
        y_p, o_as = _mix_ffn(x1, o_a, o_b, g_a, g_b, mod_p, *mix_w, *ffn2, tm=TOKEN_TILE, rider=attend)
        token_major = lambda a: a.reshape(b, N_HEADS_A, HEAD_DIM_A, s).transpose(0, 3, 1, 2)
        rows_p.append((token_major(kt_a), token_major(vt32_a), k_b, v_b))

        o_bs = _diff_decode(heads_b(q_bs), heads_b(k_bs), heads_b(v_bs), cache_k_diff, cache_v_diff,
                            page_table, l, lam_vecs, subln_diff[l].reshape(1, V_DIM_B), lam_init)
        as_rows = lambda a: a.reshape(1, db, WIDTH).astype(BF16)
        (y_s,) = _mix_ffn(x1s, as_rows(o_as), as_rows(o_bs), g_as, g_bs, mod_s, *mix_w, *ffn2, tm=db)
        rows_s.append((k_as, v_as, k_bs, v_bs))

    def stack(rows, i, lead, heads, hd):
        return jnp.stack([r[i].reshape(lead + (heads, hd)) for r in rows])

    lp, ls = (b, s), (db, 1)
    return (y_p, y_s.reshape(db, 1, d),
            stack(rows_p, 0, lp, N_HEADS_A, HEAD_DIM_A), stack(rows_p, 1, lp, N_HEADS_A, HEAD_DIM_A),
            stack(rows_p, 2, lp, N_HEADS_B, V_DIM_B), stack(rows_p, 3, lp, N_HEADS_B, V_DIM_B),
            stack(rows_s, 0, ls, N_HEADS_A, HEAD_DIM_A), stack(rows_s, 1, ls, N_HEADS_A, HEAD_DIM_A),
            stack(rows_s, 2, ls, N_HEADS_B, V_DIM_B), stack(rows_s, 3, ls, N_HEADS_B, V_DIM_B))
```
